```python
import jax, jax.numpy as jnp
from jax import lax
import numpy as np

D_MODEL = 1024
BATCH = 8
SEQ = 2048
DEPTH = 4
DEC_BATCH = 32
DEC_SEQ = 4
PAST_LEN = 8192
PAGE_SIZE = 128

N_MIXERS = 2
N_FOX = (DEPTH + 1) // 2
N_MLSTM = DEPTH // 2
FOX_HEADS = 16
FOX_HEAD_DIM = D_MODEL // FOX_HEADS
FOX_WIDTH = FOX_HEADS * FOX_HEAD_DIM
FOX_IN = 3 * FOX_WIDTH + FOX_HEADS + D_MODEL
Q_BLOCK = 128
ML_HEADS = 8
ML_V_DIM = D_MODEL // ML_HEADS
ML_QK_DIM = ML_V_DIM // 2
ML_QK_WIDTH = ML_HEADS * ML_QK_DIM
ML_V_WIDTH = ML_HEADS * ML_V_DIM
ML_IN = 2 * ML_QK_WIDTH + 2 * ML_V_WIDTH + 2 * ML_HEADS
ML_CHUNK = 64
PEER_HEADS = 8
PEER_NKEYS = 128
PEER_N_EXPERTS = PEER_NKEYS * PEER_NKEYS
PEER_KEY_DIM = 256
PEER_HALF = PEER_KEY_DIM // 2
PEER_TOPK = 16
PEER_TOK_BLOCK = 128
RMS_EPS = 1e-6
FORGET_BIAS = 3.0

kernel_name = 'fox_mlstm_peer_adaln_decode_step'


def rms_norm(x, gain):
    xf = x.astype(jnp.float32)
    y = xf * lax.rsqrt(jnp.mean(xf * xf, axis=-1, keepdims=True) + RMS_EPS)
    return (y * gain.astype(jnp.float32)).astype(x.dtype)


def modulate(x, gain, shift, scale):
    return rms_norm(x, gain) * (1 + scale[:, None, :]) + shift[:, None, :]


def fox_project(h, w_in, b_f, q_gain, k_gain):
    B, T, _ = h.shape
    proj = h @ w_in
    q, k, v, f_logit, g = jnp.split(proj, [FOX_WIDTH, 2 * FOX_WIDTH, 3 * FOX_WIDTH, 3 * FOX_WIDTH + FOX_HEADS], axis=-1)
    q = rms_norm(q.reshape(B, T, FOX_HEADS, FOX_HEAD_DIM), q_gain)
    k = rms_norm(k.reshape(B, T, FOX_HEADS, FOX_HEAD_DIM), k_gain)
    v = v.reshape(B, T, FOX_HEADS, FOX_HEAD_DIM)
    log_f = jax.nn.log_sigmoid((f_logit + b_f).astype(jnp.float32))
    return q, k, v, log_f, g


def fox_attend_prompt(q, k, v, log_f):
    B, S, H, Dh = q.shape
    scale = Dh ** -0.5
    F = jnp.cumsum(log_f, axis=1)
    Fk = F.transpose(0, 2, 1)
    nb = S // Q_BLOCK
    qb = q.reshape(B, nb, Q_BLOCK, H, Dh).transpose(1, 0, 2, 3, 4)
    Fqb = F.reshape(B, nb, Q_BLOCK, H).transpose(1, 0, 3, 2)
    pos_k = jnp.arange(S)

    def block(args):
        i, q_i, Fq_i = args
        s = jnp.einsum('bqhd,bkhd->bhqk', q_i, k).astype(jnp.float32) * scale
        s = s + (Fq_i[..., :, None] - Fk[:, :, None, :])
        pos_q = i * Q_BLOCK + jnp.arange(Q_BLOCK)
        s = jnp.where(pos_k[None, :] <= pos_q[:, None], s, -jnp.inf)
        p = jax.nn.softmax(s, axis=-1)
        return jnp.einsum('bhqk,bkhd->bqhd', p.astype(v.dtype), v)

    out = lax.map(block, (jnp.arange(nb), qb, Fqb))
    return out.transpose(1, 0, 2, 3, 4).reshape(B, S, H, Dh)


def fox_attend_sample(q, k, v, log_f, cache_k, cache_v, cache_logf, page_table, j):
    Bd, T, H, Dh = q.shape
    n_pages = page_table.shape[1]
    scale = Dh ** -0.5
    logf_past = cache_logf[j, page_table].astype(jnp.float32)
    F_past = jnp.cumsum(logf_past.reshape(Bd, n_pages * PAGE_SIZE, H), axis=1)
    F_new = F_past[:, -1:, :] + jnp.cumsum(log_f, axis=1)
    Fq = F_new.transpose(0, 2, 1)
    F_pages = F_past.reshape(Bd, n_pages, PAGE_SIZE, H).transpose(1, 0, 3, 2)

    def page_step(carry, xs):
        m, l, acc = carry
        phys, Fk = xs
        kp = cache_k[j, phys]
        vp = cache_v[j, phys]
        s = jnp.einsum('bthd,bphd->bhtp', q, kp).astype(jnp.float32) * scale
        s = s + (Fq[..., None] - Fk[:, :, None, :])
        m_new = jnp.maximum(m, s.max(-1))
        corr = jnp.exp(m - m_new)
        p = jnp.exp(s - m_new[..., None])
        l = l * corr + p.sum(-1)
        acc = acc * corr[..., None] + jnp.einsum('bhtp,bphd->bhtd', p, vp.astype(jnp.float32))
        return (m_new, l, acc), None

    init = (jnp.full((Bd, H, T), -jnp.inf, jnp.float32),
            jnp.zeros((Bd, H, T), jnp.float32),
            jnp.zeros((Bd, H, T, Dh), jnp.float32))
    (m, l, acc), _ = lax.scan(page_step, init, (page_table.T, F_pages))
    s = jnp.einsum('bthd,bshd->bhts', q, k).astype(jnp.float32) * scale
    s = s + (Fq[..., :, None] - Fq[..., None, :])
    s = jnp.where(jnp.tril(jnp.ones((T, T), bool)), s, -jnp.inf)
    m_new = jnp.maximum(m, s.max(-1))
    corr = jnp.exp(m - m_new)
    p = jnp.exp(s - m_new[..., None])
    l = l * corr + p.sum(-1)
    acc = acc * corr[..., None] + jnp.einsum('bhts,bshd->bhtd', p, v.astype(jnp.float32))
    out = acc / l[..., None]
    return out.transpose(0, 2, 1, 3).astype(q.dtype)


def ml_project(h, w_in, b_i, b_f):
    B, T, _ = h.shape
    proj = h @ w_in
    q, k, v, i_pre, f_pre, o = jnp.split(
        proj, [ML_QK_WIDTH, 2 * ML_QK_WIDTH, 2 * ML_QK_WIDTH + ML_V_WIDTH,
               2 * ML_QK_WIDTH + ML_V_WIDTH + ML_HEADS, 2 * ML_QK_WIDTH + ML_V_WIDTH + 2 * ML_HEADS], axis=-1)
    q = q.reshape(B, T, ML_HEADS, ML_QK_DIM)
    k = k.reshape(B, T, ML_HEADS, ML_QK_DIM) * (ML_QK_DIM ** -0.5)
    v = v.reshape(B, T, ML_HEADS, ML_V_DIM)
    i_log = (i_pre + b_i).astype(jnp.float32)
    f_log = jax.nn.log_sigmoid((f_pre + b_f).astype(jnp.float32))
    return q, k, v, i_log, f_log, o


def mlstm_chunked(q, k, v, i_log, f_log, C0, n0, m0, chunk):
    B, T, H, dk = q.shape
    nc = T // chunk

    def to_chunks(a):
        return a.astype(jnp.float32).reshape(B, nc, chunk, *a.shape[2:]).swapaxes(0, 1)

    xs = (to_chunks(q), to_chunks(k), to_chunks(v), to_chunks(i_log), to_chunks(f_log))
    tri = jnp.tril(jnp.ones((chunk, chunk), bool))

    def step(carry, xs_c):
        C, n, m = carry
        qc, kc, vc, ic, fc = xs_c
        b = jnp.cumsum(fc, axis=1).transpose(0, 2, 1)
        ig = ic.transpose(0, 2, 1)
        Dm = jnp.where(tri, b[..., :, None] - b[..., None, :] + ig[..., None, :], -jnp.inf)
        inter = b + m[..., None]
        m_t = jnp.maximum(inter, Dm.max(-1))
        w_inter = jnp.exp(inter - m_t)
        A = jnp.exp(Dm - m_t[..., None]) * jnp.einsum('blhd,bshd->bhls', qc, kc)
        num = w_inter[..., None] * jnp.einsum('blhd,bhdv->bhlv', qc, C) + jnp.einsum('bhls,bshv->bhlv', A, vc)
        den = w_inter * jnp.einsum('blhd,bhd->bhl', qc, n) + A.sum(-1)
        h = num / jnp.maximum(jnp.abs(den), jnp.exp(-m_t))[..., None]
        bL = b[..., -1]
        g_log = bL[..., None] - b + ig
        m_new = jnp.maximum(bL + m, g_log.max(-1))
        wk = jnp.exp(g_log - m_new[..., None])
        decay = jnp.exp(bL + m - m_new)
        C_new = decay[..., None, None] * C + jnp.einsum('bhl,blhd,blhv->bhdv', wk, kc, vc)
        n_new = decay[..., None] * n + jnp.einsum('bhl,blhd->bhd', wk, kc)
        return (C_new, n_new, m_new), h

    (C, n, m), hs = lax.scan(step, (C0.astype(jnp.float32), n0.astype(jnp.float32), m0.astype(jnp.float32)), xs)
    h = hs.transpose(1, 0, 3, 2, 4).reshape(B, T, H, v.shape[-1])
    return h, C, n, m


def peer_ffn(h, w_q, sub_k1, sub_k2, u_tab, v_tab):
    B, T, D = h.shape
    x = h.reshape(B * T, D)
    n = x.shape[0]
    nb = -(-n // PEER_TOK_BLOCK)
    xp = jnp.pad(x, ((0, nb * PEER_TOK_BLOCK - n), (0, 0))).reshape(nb, PEER_TOK_BLOCK, D)

    def block(xb):
        qh = (xb @ w_q).reshape(PEER_TOK_BLOCK, PEER_HEADS, PEER_KEY_DIM).astype(jnp.float32)
        s1 = jnp.einsum('thd,hkd->thk', qh[..., :PEER_HALF], sub_k1.astype(jnp.float32))
        s2 = jnp.einsum('thd,hkd->thk', qh[..., PEER_HALF:], sub_k2.astype(jnp.float32))
        v1, i1 = lax.top_k(s1, PEER_TOPK)
        v2, i2 = lax.top_k(s2, PEER_TOPK)
        cand = (v1[..., :, None] + v2[..., None, :]).reshape(PEER_TOK_BLOCK, PEER_HEADS, PEER_TOPK * PEER_TOPK)
        cand_idx = (i1[..., :, None] * PEER_NKEYS + i2[..., None, :]).reshape(PEER_TOK_BLOCK, PEER_HEADS, PEER_TOPK * PEER_TOPK)
        top_s, pos = lax.top_k(cand, PEER_TOPK)
        eidx = jnp.take_along_axis(cand_idx, pos, axis=-1)
        g = jax.nn.softmax(top_s, axis=-1)
        u = u_tab[eidx]
        a = jax.nn.gelu(jnp.einsum('thkd,td->thk', u, xb).astype(jnp.float32), approximate=False)
        return jnp.einsum('thk,thkd->td', (g * a).astype(xb.dtype), v_tab[eidx])

    y = lax.map(block, xp).reshape(nb * PEER_TOK_BLOCK, D)[:n]
    return y.reshape(B, T, D)


def _trunk(x, c, past, params):
    (ada_w, ada_b, norm_mix, norm_ffn, fox_w_in, fox_b_f, fox_q_norm, fox_k_norm, fox_w_out,
     ml_w_in, ml_b_i, ml_b_f, ml_h_norm, ml_w_out, peer_w_q, peer_sub_k1, peer_sub_k2, peer_u, peer_v) = params
    B, T, D = x.shape
    new_k, new_v, new_lf, new_C, new_n, new_m = [], [], [], [], [], []
    c_act = jax.nn.silu(c)
    for layer in range(DEPTH):
        mod = c_act @ ada_w[layer] + ada_b[layer]
        sh1, sc1, g1, sh2, sc2, g2 = jnp.split(mod, 6, axis=-1)
        h = modulate(x, norm_mix[layer], sh1, sc1)
        j = layer // N_MIXERS
        if layer % N_MIXERS == 0:
            q, k, v, lf, og = fox_project(h, fox_w_in[j], fox_b_f[j], fox_q_norm[j], fox_k_norm[j])
            if past is None:
                o = fox_attend_prompt(q, k, v, lf)
            else:
                cache_k, cache_v, cache_logf, _, _, _, page_table = past
                o = fox_attend_sample(q, k, v, lf, cache_k, cache_v, cache_logf, page_table, j)
            mix = (o.reshape(B, T, FOX_WIDTH) * jax.nn.sigmoid(og)) @ fox_w_out[j]
            new_k.append(k)
            new_v.append(v)
            new_lf.append(lf)
        else:
            q, k, v, il, fl, og = ml_project(h, ml_w_in[j], ml_b_i[j], ml_b_f[j])
            if past is None:
                C0 = jnp.zeros((B, ML_HEADS, ML_QK_DIM, ML_V_DIM), jnp.float32)
                n0 = jnp.zeros((B, ML_HEADS, ML_QK_DIM), jnp.float32)
                m0 = jnp.zeros((B, ML_HEADS), jnp.float32)
                chunk = ML_CHUNK
            else:
                _, _, _, state_C, state_n, state_m, _ = past
                C0, n0, m0 = state_C[j], state_n[j], state_m[j]
                chunk = T
            hh, C, n_, m_ = mlstm_chunked(q, k, v, il, fl, C0, n0, m0, chunk)
            hh = rms_norm(hh, ml_h_norm[j]).astype(x.dtype)
            mix = (hh.reshape(B, T, ML_V_WIDTH) * jax.nn.sigmoid(og)) @ ml_w_out[j]
            new_C.append(C)
            new_n.append(n_)
            new_m.append(m_)
        x = x + g1[:, None, :] * mix
        h = modulate(x, norm_ffn[layer], sh2, sc2)
        x = x + g2[:, None, :] * peer_ffn(h, peer_w_q[layer], peer_sub_k1[layer], peer_sub_k2[layer], peer_u[layer], peer_v[layer])
    return (x, jnp.stack(new_k), jnp.stack(new_v), jnp.stack(new_lf),
            jnp.stack(new_C), jnp.stack(new_n), jnp.stack(new_m))


def setup_inputs(seed: int = 0) -> dict:
    key = jax.random.key(seed)
    keys = list(jax.random.split(key, 32))

    def nrm(i, shape, s):
        return jax.random.normal(keys[i], shape, jnp.float32) * s

    n_pages = PAST_LEN // PAGE_SIZE
    n_used = DEC_BATCH * n_pages
    n_pool = n_used + (n_used + 3) // 4
    perm = jax.random.permutation(keys[8], n_pool)
    page_table = perm[:n_used].reshape(DEC_BATCH, n_pages).astype(jnp.int32)
    D = D_MODEL
    return {
        'x_prompt': nrm(0, (BATCH, SEQ, D), 1.0),
        'x_sample': nrm(1, (DEC_BATCH, DEC_SEQ, D), 1.0),
        'cache_k': nrm(2, (N_FOX, n_pool, PAGE_SIZE, FOX_HEADS, FOX_HEAD_DIM), 1.0),
        'cache_v': nrm(3, (N_FOX, n_pool, PAGE_SIZE, FOX_HEADS, FOX_HEAD_DIM), 1.0),
        'cache_logf': jax.nn.log_sigmoid(FORGET_BIAS + nrm(4, (N_FOX, n_pool, PAGE_SIZE, FOX_HEADS), 1.0)),
        'state_C': nrm(5, (N_MLSTM, DEC_BATCH, ML_HEADS, ML_QK_DIM, ML_V_DIM), 0.5),
        'state_n': nrm(6, (N_MLSTM, DEC_BATCH, ML_HEADS, ML_QK_DIM), 0.5),
        'state_m': nrm(7, (N_MLSTM, DEC_BATCH, ML_HEADS), 1.0),
        'page_table': page_table,
        'c_prompt': nrm(9, (BATCH, D), 1.0),
        'c_sample': nrm(10, (DEC_BATCH, D), 1.0),
        'ada_w': nrm(11, (DEPTH, D, 6 * D), 0.5 * D ** -0.5),
        'ada_b': nrm(12, (DEPTH, 6 * D), 0.02),
        'norm_mix': 1.0 + nrm(13, (DEPTH, D), 0.05),
        'norm_ffn': 1.0 + nrm(14, (DEPTH, D), 0.05),
        'fox_w_in': nrm(15, (N_FOX, D, FOX_IN), D ** -0.5),
        'fox_b_f': FORGET_BIAS + nrm(16, (N_FOX, FOX_HEADS), 0.5),
        'fox_q_norm': 1.0 + nrm(17, (N_FOX, FOX_HEAD_DIM), 0.05),
        'fox_k_norm': 1.0 + nrm(18, (N_FOX, FOX_HEAD_DIM), 0.05),
        'fox_w_out': nrm(19, (N_FOX, FOX_WIDTH, D), FOX_WIDTH ** -0.5),
        'ml_w_in': nrm(20, (N_MLSTM, D, ML_IN), D ** -0.5),
        'ml_b_i': nrm(21, (N_MLSTM, ML_HEADS), 0.1),
        'ml_b_f': FORGET_BIAS + nrm(22, (N_MLSTM, ML_HEADS), 0.5),
        'ml_h_norm': 1.0 + nrm(23, (N_MLSTM, ML_HEADS, ML_V_DIM), 0.05),
        'ml_w_out': nrm(24, (N_MLSTM, ML_V_WIDTH, D), ML_V_WIDTH ** -0.5),
        'peer_w_q': nrm(25, (DEPTH, D, PEER_HEADS * PEER_KEY_DIM), D ** -0.5),
        'peer_sub_k1': nrm(26, (DEPTH, PEER_HEADS, PEER_NKEYS, PEER_HALF), PEER_HALF ** -0.5),
        'peer_sub_k2': nrm(27, (DEPTH, PEER_HEADS, PEER_NKEYS, PEER_HALF), PEER_HALF ** -0.5),
        'peer_u': nrm(28, (DEPTH, PEER_N_EXPERTS, D), D ** -0.5),
        'peer_v': nrm(29, (DEPTH, PEER_N_EXPERTS, D), D ** -0.5),
    }


def reference(x_prompt, x_sample, cache_k, cache_v, cache_logf, state_C, state_n, state_m, page_table,
              c_prompt, c_sample, ada_w, ada_b, norm_mix, norm_ffn, fox_w_in, fox_b_f, fox_q_norm, fox_k_norm,
              fox_w_out, ml_w_in, ml_b_i, ml_b_f, ml_h_norm, ml_w_out, peer_w_q, peer_sub_k1, peer_sub_k2,
              peer_u, peer_v):
    params = (ada_w, ada_b, norm_mix, norm_ffn, fox_w_in, fox_b_f, fox_q_norm, fox_k_norm, fox_w_out,
              ml_w_in, ml_b_i, ml_b_f, ml_h_norm, ml_w_out, peer_w_q, peer_sub_k1, peer_sub_k2, peer_u, peer_v)
    y_prompt, k_p, v_p, lf_p, C_p, n_p, m_p = _trunk(x_prompt, c_prompt, None, params)
    past = (cache_k, cache_v, cache_logf, state_C, state_n, state_m, page_table)
    y_sample, k_s, v_s, lf_s, C_s, n_s, m_s = _trunk(x_sample, c_sample, past, params)
    return (y_prompt, y_sample, k_p, v_p, lf_p, C_p, n_p, m_p, k_s, v_s, lf_s, C_s, n_s, m_s)
```

```python
import functools
import math

import jax
import jax.numpy as jnp
from jax import lax
from jax.experimental import pallas as pl
from jax.experimental.pallas import tpu as pltpu

F32 = jnp.float32
BF16 = jnp.bfloat16
I32 = jnp.int32

RMS_EPS = 1e-6
LANES = 128
MXU_DIM = 256
VMEM_LIMIT = 52 * 1024 * 1024
PEER_TOPK = 16
PAGE_SIZE = 128
NEG_INF = float("-inf")


def _dot(a, b):
    return jnp.dot(a, b, preferred_element_type=F32)


def _dot_nt(a, b):
    return lax.dot_general(a, b, (((1,), (1,)), ((), ())), preferred_element_type=F32)


def _dot_tn(a, b):
    return lax.dot_general(a, b, (((0,), (0,)), ((), ())), preferred_element_type=F32)


def _split2(a):
    hi = a.astype(BF16)
    lo = (a - hi.astype(F32)).astype(BF16)
    return hi, lo


def _dot3(a, b):
    ah, al = _split2(a)
    bh, bl = _split2(b)
    return _dot(ah, bh) + _dot(ah, bl) + _dot(al, bh)


def _modnorm(x, gain, scale, shift):
    ms = jnp.mean(x * x, axis=-1, keepdims=True)
    return x * lax.rsqrt(ms + RMS_EPS) * gain * (1.0 + scale) + shift


def _log_sigmoid(z):
    return jnp.minimum(z, 0.0) - jnp.log1p(jnp.exp(-jnp.abs(z)))


def _params(*sem):
    return pltpu.CompilerParams(dimension_semantics=sem, vmem_limit_bytes=VMEM_LIMIT)


def _mod_spec(rows, d, tiles_per_group):
    return pl.BlockSpec((None, rows, d), lambda i: (i // tiles_per_group, 0, 0))


def _adaln_kernel(c_ref, w_ref, b_ref, o_ref):
    c = c_ref[...]
    o_ref[...] = _dot3(c * jax.nn.sigmoid(c), w_ref[...]) + b_ref[...]


def adaln(c, ada_w, ada_b, tn=1536):
    L, D, D6 = ada_w.shape
    R = c.shape[0]
    return pl.pallas_call(
        _adaln_kernel,
        grid=(L, D6 // tn),
        in_specs=[pl.BlockSpec((R, D), lambda l, j: (0, 0)),
                  pl.BlockSpec((None, D, tn), lambda l, j: (l, 0, j)),
                  pl.BlockSpec((None, 1, tn), lambda l, j: (l, 0, j))],
        out_specs=pl.BlockSpec((None, R, tn), lambda l, j: (l, 0, j)),
        out_shape=jax.ShapeDtypeStruct((L, R, D6), F32),
        compiler_params=_params("arbitrary", "arbitrary"),
        name="adaln",
    )(c, ada_w, ada_b.reshape(L, 1, D6))


def _segcumsum_kernel(x_ref, o_ref, *, seg):
    C, T = x_ref.shape
    lane = lax.broadcasted_iota(I32, (C, LANES), 1)
    w = min(seg, LANES)
    carry = None
    for blk in range(T // LANES):
        x = x_ref[:, blk * LANES:(blk + 1) * LANES]
        k = 1
        while k < w:
            x = x + jnp.where((lane & (w - 1)) >= k, pltpu.roll(x, k, 1), 0.0)
            k *= 2
        if seg > LANES:
            if blk % (seg // LANES) != 0:
                x = x + carry
            carry = x[:, LANES - 1:LANES]
        o_ref[:, blk * LANES:(blk + 1) * LANES] = x


def seg_cumsum(x, seg):
    R, C, T = x.shape
    assert T % LANES == 0 and (seg & (seg - 1)) == 0
    assert T % seg == 0 and (seg <= LANES or seg % LANES == 0)
    return pl.pallas_call(
        functools.partial(_segcumsum_kernel, seg=seg),
        grid=(R,),
        in_specs=[pl.BlockSpec((None, C, T), lambda r: (r, 0, 0))],
        out_specs=pl.BlockSpec((None, C, T), lambda r: (r, 0, 0)),
        out_shape=jax.ShapeDtypeStruct((R, C, T), F32),
        compiler_params=_params("arbitrary"),
        name="seg_cumsum",
    )(x)


def _pair_headnorm(blk, gain2, hd):
    lane = lax.broadcasted_iota(I32, blk.shape, 1)
    lo = lane < hd
    sq = blk * blk
    s_lo = jnp.sum(jnp.where(lo, sq, 0.0), axis=-1, keepdims=True)
    s_hi = jnp.sum(jnp.where(lo, 0.0, sq), axis=-1, keepdims=True)
    inv = lax.rsqrt(jnp.where(lo, s_lo, s_hi) * (1.0 / hd) + RMS_EPS)
    return blk * inv * gain2


def _fox_proj_kernel(x_ref, sc_ref, sh_ref, gain_ref, w_ref, wfh_ref, wfl_ref, bf_ref, qg_ref, kg_ref,
                     q_ref, k_ref, v_ref, kb_ref, vb_ref, sg_ref, lf_ref, *, n_heads):
    D = x_ref.shape[1]
    hd = D // n_heads
    h = _modnorm(x_ref[...], gain_ref[...], sc_ref[...], sh_ref[...])
    hh, hl = _split2(h)
    qk_scale = hd ** -0.5
    q = _dot(hh, w_ref[:, 0:D])
    for c in range(D // LANES):
        sl = slice(c * LANES, (c + 1) * LANES)
        q_ref[:, sl] = (_pair_headnorm(q[:, sl], qg_ref[...], hd) * qk_scale).astype(BF16)
    k = _dot(hh, w_ref[:, D:2 * D])
    for c in range(D // LANES):
        sl = slice(c * LANES, (c + 1) * LANES)
        kn = _pair_headnorm(k[:, sl], kg_ref[...], hd)
        k_ref[:, sl] = kn
        kb_ref[:, sl] = kn.astype(BF16)
    v = _dot(hh, w_ref[:, 2 * D:3 * D])
    v_ref[...] = v
    vb_ref[...] = v.astype(BF16)
    sg_ref[...] = jax.nn.sigmoid(_dot(hh, w_ref[:, 3 * D:4 * D]))
    f = _dot(hh, wfh_ref[...]) + _dot(hh, wfl_ref[...]) + _dot(hl, wfh_ref[...])
    lf_ref[...] = _log_sigmoid(f[:, 0:n_heads] + bf_ref[...])


def fox_project(x, scale, shift, gain, w_in, b_f, q_gain, k_gain, *, tm, tiles_per_group):
    N, D = x.shape
    H = b_f.shape[0]
    hd = D // H
    assert 2 * hd == LANES
    R = scale.shape[1]
    w_main = jnp.concatenate([w_in[:, :3 * D], w_in[:, 3 * D + H:]], axis=1).astype(BF16)
    w_f = jnp.pad(w_in[:, 3 * D:3 * D + H], ((0, 0), (0, LANES - H)))
    wfh = w_f.astype(BF16)
    wfl = (w_f - wfh.astype(F32)).astype(BF16)
    const = lambda i: (0, 0)
    row = lambda i: (i, 0)
    outs = pl.pallas_call(
        functools.partial(_fox_proj_kernel, n_heads=H),
        grid=(N // tm,),
        in_specs=[pl.BlockSpec((tm, D), row),
                  _mod_spec(R, D, tiles_per_group), _mod_spec(R, D, tiles_per_group),
                  pl.BlockSpec((1, D), const),
                  pl.BlockSpec((D, 4 * D), const),
                  pl.BlockSpec((D, LANES), const), pl.BlockSpec((D, LANES), const),
                  pl.BlockSpec((1, H), const),
                  pl.BlockSpec((1, LANES), const), pl.BlockSpec((1, LANES), const)],
        out_specs=[pl.BlockSpec((tm, D), row)] * 6 + [pl.BlockSpec((tm, H), row)],
        out_shape=[jax.ShapeDtypeStruct((N, D), BF16),
                   jax.ShapeDtypeStruct((N, D), F32), jax.ShapeDtypeStruct((N, D), F32),
                   jax.ShapeDtypeStruct((N, D), BF16), jax.ShapeDtypeStruct((N, D), BF16),
                   jax.ShapeDtypeStruct((N, D), F32),
                   jax.ShapeDtypeStruct((N, H), F32)],
        compiler_params=_params("arbitrary"),
        name="fox_project",
    )(x, scale, shift, gain.reshape(1, D), w_main, wfh, wfl, b_f.reshape(1, H),
      jnp.tile(q_gain, 2).reshape(1, LANES), jnp.tile(k_gain, 2).reshape(1, LANES))
    return outs


def _fox_attn_kernel(q_ref, k_ref, v_ref, fc_ref, fr_ref, o_ref, *, hd):
    TQ = q_ref.shape[0]
    qi = pl.program_id(2)
    q = q_ref[...]
    lane = lax.broadcasted_iota(I32, (TQ, LANES), 1)
    row = lax.broadcasted_iota(I32, (TQ, TQ), 0)
    col = lax.broadcasted_iota(I32, (TQ, TQ), 1)
    outs = []
    for hh in range(2):
        qm = jnp.where((lane >= hh * hd) & (lane < (hh + 1) * hd), q, jnp.zeros_like(q))
        fq = fc_ref[:, hh:hh + 1]

        def step(j, carry, masked, qm=qm, fq=fq, hh=hh):
            m, l, acc = carry
            start = pl.multiple_of(j * TQ, TQ)
            kj = k_ref[pl.ds(start, TQ), :]
            vj = v_ref[pl.ds(start, TQ), :]
            s = _dot_nt(qm, kj) + (fq - fr_ref[hh:hh + 1, pl.ds(start, TQ)])
            if masked:
                s = jnp.where(col <= row, s, NEG_INF)
            m_new = jnp.maximum(m, jnp.max(s, axis=-1, keepdims=True))
            alpha = jnp.exp(m - m_new)
            p = jnp.exp(s - m_new)
            l = alpha * l + jnp.sum(p, axis=-1, keepdims=True)
            acc = alpha * acc + _dot(p.astype(BF16), vj)
            return m_new, l, acc

        init = (jnp.full((TQ, 1), NEG_INF, F32), jnp.zeros((TQ, 1), F32), jnp.zeros((TQ, LANES), F32))
        carry = lax.fori_loop(0, qi, functools.partial(step, masked=False), init)
        _, l, acc = step(qi, carry, True)
        outs.append(acc / l)
    o_ref[...] = jnp.where(lane < hd, outs[0], outs[1])


def fox_attend_prompt(q, kb, vb, F, *, n_heads, tq=256):
    B, S, D = q.shape
    hd = D // n_heads
    HP = n_heads // 2
    fcol = F.reshape(B, S, HP, 2).transpose(0, 2, 1, 3)
    frow = fcol.transpose(0, 1, 3, 2)
    return pl.pallas_call(
        functools.partial(_fox_attn_kernel, hd=hd),
        grid=(B, HP, S // tq),
        in_specs=[pl.BlockSpec((None, tq, LANES), lambda b, h, i: (b, i, h)),
                  pl.BlockSpec((None, S, LANES), lambda b, h, i: (b, 0, h)),
                  pl.BlockSpec((None, S, LANES), lambda b, h, i: (b, 0, h)),
                  pl.BlockSpec((None, None, tq, 2), lambda b, h, i: (b, h, i, 0)),
                  pl.BlockSpec((None, None, 2, S), lambda b, h, i: (b, h, 0, 0))],
        out_specs=pl.BlockSpec((None, tq, LANES), lambda b, h, i: (b, i, h)),
        out_shape=jax.ShapeDtypeStruct((B, S, D), F32),
        compiler_params=_params("arbitrary", "arbitrary", "arbitrary"),
        name="fox_attend_prompt",
    )(q, kb, vb, fcol, frow)


def _fox_decode_kernel(pt_ref, qbd_ref, k_ref, v_ref, lft_ref, kn_ref, vn_ref, cn_ref, o_ref,
                       m_scr, l_scr, acc_scr, carry_scr, *, n_heads, n_new, hd):
    del pt_ref
    p = pl.program_id(1)
    n_pages = pl.num_programs(1)
    R = qbd_ref.shape[0]
    D = acc_scr.shape[1]

    @pl.when(p == 0)
    def _():
        m_scr[...] = jnp.full(m_scr.shape, NEG_INF, F32)
        l_scr[...] = jnp.zeros(l_scr.shape, F32)
        acc_scr[...] = jnp.zeros(acc_scr.shape, F32)
        carry_scr[...] = jnp.zeros(carry_scr.shape, F32)

    qbd = qbd_ref[...]

    def update(s, vb):
        m = m_scr[...]
        m_new = jnp.maximum(m, jnp.max(s, axis=-1, keepdims=True))
        alpha = jnp.exp(m - m_new)
        pr = jnp.exp(s - m_new)
        l_scr[...] = alpha * l_scr[...] + jnp.sum(pr, axis=-1, keepdims=True)
        acc_scr[...] = alpha * acc_scr[...] + _dot(pr.astype(BF16), vb)
        m_scr[...] = m_new

    x = lft_ref[...]
    lane = lax.broadcasted_iota(I32, x.shape, 1)
    k = 1
    while k < PAGE_SIZE:
        x = x + jnp.where(lane >= k, pltpu.roll(x, k, 1), 0.0)
        k *= 2
    fk = x + carry_scr[...]
    carry_scr[...] = fk[:, PAGE_SIZE - 1:PAGE_SIZE]
    s = _dot_nt(qbd, k_ref[...].astype(BF16)) - jnp.concatenate([fk] * n_new, axis=0)
    update(s, v_ref[...].astype(BF16))

    @pl.when(p == n_pages - 1)
    def _():
        fn = cn_ref[...] + carry_scr[...]
        s2 = _dot_nt(qbd, kn_ref[...]) - jnp.concatenate([fn] * n_new, axis=0)
        key = lax.broadcasted_iota(I32, s2.shape, 1)
        t = lax.broadcasted_iota(I32, s2.shape, 0) // n_heads
        s2 = jnp.where(key <= t, s2, NEG_INF)
        update(s2, vn_ref[...])
        out = acc_scr[...] / l_scr[...]
        rh = lax.broadcasted_iota(I32, (R, D), 0) % n_heads
        ch = lax.broadcasted_iota(I32, (R, D), 1) // hd
        out = jnp.where(rh == ch, out, 0.0)
        o_ref[...] = jnp.sum(out.reshape(n_new, n_heads, D), axis=1)


def fox_attend_sample(q, kb_new, vb_new, lf_new, cache_k, cache_v, cache_logf, page_table, j, *, n_heads):
    Bd, T, D = q.shape
    H = n_heads
    hd = D // H
    n_pages = page_table.shape[1]
    n_fox, n_pool, P = cache_k.shape[:3]
    assert P == PAGE_SIZE
    ck = cache_k.reshape(n_fox, n_pool, P, D)
    cv = cache_v.reshape(n_fox, n_pool, P, D)
    clt = cache_logf[j].transpose(0, 2, 1)
    q4 = q.reshape(Bd, T, H, hd)
    eye = jnp.eye(H, dtype=q.dtype)
    qbd = (q4[:, :, :, None, :] * eye[None, None, :, :, None]).reshape(Bd, T * H, D)
    pad = ((0, 0), (0, PAGE_SIZE - T), (0, 0))
    kn = jnp.pad(kb_new, pad)
    vn = jnp.pad(vb_new, pad)
    cn = seg_cumsum(jnp.pad(lf_new.transpose(0, 2, 1), ((0, 0), (0, 0), (0, PAGE_SIZE - T))), PAGE_SIZE)
    pt = page_table.reshape(-1).astype(I32)
    grid_spec = pltpu.PrefetchScalarGridSpec(
        num_scalar_prefetch=1,
        grid=(Bd, n_pages),
        in_specs=[pl.BlockSpec((None, T * H, D), lambda b, p, pt: (b, 0, 0)),
                  pl.BlockSpec((None, None, P, D), lambda b, p, pt: (j, pt[b * n_pages + p], 0, 0)),
                  pl.BlockSpec((None, None, P, D), lambda b, p, pt: (j, pt[b * n_pages + p], 0, 0)),
                  pl.BlockSpec((None, H, P), lambda b, p, pt: (pt[b * n_pages + p], 0, 0)),
                  pl.BlockSpec((None, P, D), lambda b, p, pt: (b, 0, 0)),
                  pl.BlockSpec((None, P, D), lambda b, p, pt: (b, 0, 0)),
                  pl.BlockSpec((None, H, P), lambda b, p, pt: (b, 0, 0))],
        out_specs=pl.BlockSpec((None, T, D), lambda b, p, pt: (b, 0, 0)),
        scratch_shapes=[pltpu.VMEM((T * H, 1), F32), pltpu.VMEM((T * H, 1), F32),
                        pltpu.VMEM((T * H, D), F32), pltpu.VMEM((H, 1), F32)],
    )
    return pl.pallas_call(
        functools.partial(_fox_decode_kernel, n_heads=H, n_new=T, hd=hd),
        grid_spec=grid_spec,
        out_shape=jax.ShapeDtypeStruct((Bd, T, D), F32),
        compiler_params=_params("arbitrary", "arbitrary"),
        name="fox_attend_sample",
    )(pt, qbd, ck, cv, clt, kn, vn, cn)


def _outproj_kernel(a_ref, gate_ref, w_ref, x_ref, g_ref, hn_ref, o_ref, *, head_norm):
    a = a_ref[...]
    D = a.shape[1]
    if head_norm:
        blocks = []
        for c in range(D // LANES):
            sl = slice(c * LANES, (c + 1) * LANES)
            blk = a[:, sl]
            ms = jnp.mean(blk * blk, axis=-1, keepdims=True)
            blocks.append(blk * lax.rsqrt(ms + RMS_EPS) * hn_ref[:, sl])
        a = jnp.concatenate(blocks, axis=-1)
    y = (a * gate_ref[...]).astype(BF16)
    o_ref[...] = x_ref[...] + g_ref[...] * _dot(y, w_ref[...])


def out_project(a, gate, w_out, x, g, head_gain, *, tm, tiles_per_group, head_norm):
    N, D = x.shape
    R = g.shape[1]
    row = lambda i: (i, 0)
    const = lambda i: (0, 0)
    return pl.pallas_call(
        functools.partial(_outproj_kernel, head_norm=head_norm),
        grid=(N // tm,),
        in_specs=[pl.BlockSpec((tm, D), row), pl.BlockSpec((tm, D), row),
                  pl.BlockSpec((D, D), const), pl.BlockSpec((tm, D), row),
                  _mod_spec(R, D, tiles_per_group), pl.BlockSpec((1, D), const)],
        out_specs=pl.BlockSpec((tm, D), row),
        out_shape=jax.ShapeDtypeStruct((N, D), F32),
        compiler_params=_params("arbitrary"),
        name="out_project",
    )(a, gate, w_out.astype(BF16), x, g, head_gain.reshape(1, D))


def _ml_proj_kernel(x_ref, sc_ref, sh_ref, gain_ref, w_ref, wgh_ref, wgl_ref, bi_ref, bf_ref,
                    q_ref, k_ref, v_ref, so_ref, il_ref, fl_ref, *, n_heads, qk_width):
    D = x_ref.shape[1]
    QW = qk_width
    h = _modnorm(x_ref[...], gain_ref[...], sc_ref[...], sh_ref[...])
    hh, hl = _split2(h)
    k_scale = (QW // n_heads) ** -0.5
    q_ref[...] = _dot(hh, w_ref[:, 0:QW]).astype(BF16)
    k_ref[...] = (_dot(hh, w_ref[:, QW:2 * QW]) * k_scale).astype(BF16)
    v_ref[...] = _dot(hh, w_ref[:, 2 * QW:2 * QW + D])
    so_ref[...] = jax.nn.sigmoid(_dot(hh, w_ref[:, 2 * QW + D:2 * QW + 2 * D]))
    gts = _dot(hh, wgh_ref[...]) + _dot(hh, wgl_ref[...]) + _dot(hl, wgh_ref[...])
    il_ref[...] = gts[:, 0:n_heads] + bi_ref[...]
    fl_ref[...] = _log_sigmoid(gts[:, n_heads:2 * n_heads] + bf_ref[...])


def ml_project(x, scale, shift, gain, w_in, b_i, b_f, *, tm, tiles_per_group, v_width):
    N, D = x.shape
    H = b_i.shape[0]
    QW = (w_in.shape[1] - 2 * v_width - 2 * H) // 2
    R = scale.shape[1]
    g0 = 2 * QW + v_width
    w_main = jnp.concatenate([w_in[:, :g0], w_in[:, g0 + 2 * H:]], axis=1).astype(BF16)
    w_g = jnp.pad(w_in[:, g0:g0 + 2 * H], ((0, 0), (0, LANES - 2 * H)))
    wgh = w_g.astype(BF16)
    wgl = (w_g - wgh.astype(F32)).astype(BF16)
    const = lambda i: (0, 0)
    row = lambda i: (i, 0)
    WM = w_main.shape[1]
    return pl.pallas_call(
        functools.partial(_ml_proj_kernel, n_heads=H, qk_width=QW),
        grid=(N // tm,),
        in_specs=[pl.BlockSpec((tm, D), row),
                  _mod_spec(R, D, tiles_per_group), _mod_spec(R, D, tiles_per_group),
                  pl.BlockSpec((1, D), const),
                  pl.BlockSpec((D, WM), const),
                  pl.BlockSpec((D, LANES), const), pl.BlockSpec((D, LANES), const),
                  pl.BlockSpec((1, H), const), pl.BlockSpec((1, H), const)],
        out_specs=[pl.BlockSpec((tm, QW), row), pl.BlockSpec((tm, QW), row),
                   pl.BlockSpec((tm, v_width), row), pl.BlockSpec((tm, v_width), row),
                   pl.BlockSpec((tm, H), row), pl.BlockSpec((tm, H), row)],
        out_shape=[jax.ShapeDtypeStruct((N, QW), BF16), jax.ShapeDtypeStruct((N, QW), BF16),
                   jax.ShapeDtypeStruct((N, v_width), F32), jax.ShapeDtypeStruct((N, v_width), F32),
                   jax.ShapeDtypeStruct((N, H), F32), jax.ShapeDtypeStruct((N, H), F32)],
        compiler_params=_params("arbitrary"),
        name="ml_project",
    )(x, scale, shift, gain.reshape(1, D), w_main, wgh, wgl, b_i.reshape(1, H), b_f.reshape(1, H))


def _mlstm_kernel(q_ref, k_ref, v_ref, ic_ref, ir_ref, bc_ref, br_ref, c0_ref, n0_ref, m0_ref,
                  h_ref, c_ref, n_ref, m_ref, *, n_heads, dk):
    L = q_ref.shape[0]
    ci = pl.program_id(1)

    @pl.when(ci == 0)
    def _():
        c_ref[...] = c0_ref[...]
        n_ref[...] = n0_ref[...]
        m_ref[...] = m0_ref[...]

    q = q_ref[...]
    k = k_ref[...]
    ic = ic_ref[...]
    ir = ir_ref[...]
    bc = bc_ref[...]
    br = br_ref[...]
    m_all = m_ref[...]
    tri = lax.broadcasted_iota(I32, (L, L), 1) <= lax.broadcasted_iota(I32, (L, L), 0)
    lane = lax.broadcasted_iota(I32, (L, LANES), 1)
    lane1 = lax.broadcasted_iota(I32, (1, LANES), 1)
    rsub = lax.broadcasted_iota(I32, (LANES, LANES), 0)
    dv = LANES
    for pr in range(n_heads // 2):
        sl = slice(pr * LANES, (pr + 1) * LANES)
        qp = q[:, sl]
        kp = k[:, sl]
        kpf = kp.astype(F32)
        qpf = qp.astype(F32)
        c_pair = c_ref[pr]
        n_pair = n_ref[pr]
        cb = c_pair.astype(BF16)
        new_c, new_n = [], []
        for hh in range(2):
            h = 2 * pr + hh
            sel = (lane >= hh * dk) & (lane < (hh + 1) * dk)
            qm = jnp.where(sel, qp, jnp.zeros_like(qp))
            b_l = bc[:, h:h + 1]
            b_s = br[h:h + 1, :]
            ig_s = ir[h:h + 1, :]
            ig_l = ic[:, h:h + 1]
            m = m_all[:, h:h + 1]
            dm = jnp.where(tri, b_l - b_s + ig_s, NEG_INF)
            inter = b_l + m
            m_t = jnp.maximum(inter, jnp.max(dm, axis=-1, keepdims=True))
            w_inter = jnp.exp(inter - m_t)
            a = jnp.exp(dm - m_t) * _dot_nt(qm, kp)
            vh = v_ref[:, h * dv:(h + 1) * dv]
            num = w_inter * _dot(qm, cb) + _dot(a.astype(BF16), vh.astype(BF16))
            qn = jnp.sum(jnp.where(sel, qpf, 0.0) * n_pair, axis=-1, keepdims=True)
            den = w_inter * qn + jnp.sum(a, axis=-1, keepdims=True)
            h_ref[:, h * dv:(h + 1) * dv] = num / jnp.maximum(jnp.abs(den), jnp.exp(-m_t))
            b_last = b_l[L - 1:L, :]
            g_row = b_last - b_s + ig_s
            m_new = jnp.maximum(b_last + m, jnp.max(g_row, axis=-1, keepdims=True))
            wk = jnp.exp(b_last - b_l + ig_l - m_new)
            decay = jnp.exp(b_last + m - m_new)
            upd = _dot_tn(kp, (wk * vh).astype(BF16))
            new_c.append(decay * c_pair + upd)
            new_n.append(decay * n_pair + jnp.sum(wk * kpf, axis=0, keepdims=True))
            m_ref[:, h:h + 1] = m_new
        c_ref[pr] = jnp.where(rsub < dk, new_c[0], new_c[1])
        n_ref[pr] = jnp.where(lane1 < dk, new_n[0], new_n[1])


def mlstm_chunked(q, k, v, i_log, f_log, C0, n0, m0, chunk):
    B, T, _ = q.shape
    H = i_log.shape[-1]
    dk = C0.shape[2]
    dv = C0.shape[3]
    assert 2 * dk == LANES and dv == LANES
    nc = T // chunk
    L = chunk
    Tp = -(-T // LANES) * LANES
    fT = jnp.pad(f_log.transpose(0, 2, 1), ((0, 0), (0, 0), (0, Tp - T)))
    bT = seg_cumsum(fT, chunk)[:, :, :T]
    br = bT.reshape(B, H, nc, L).transpose(0, 2, 1, 3)
    bc = br.transpose(0, 1, 3, 2)
    ic = i_log.reshape(B, nc, L, H)
    ir = ic.transpose(0, 1, 3, 2)
    HP = H // 2
    c0 = C0.reshape(B, HP, 2 * dk, dv)
    n0p = n0.reshape(B, HP, 1, 2 * dk)
    m0p = m0.reshape(B, 1, H)
    QW = H * dk
    VW = H * dv
    col_spec = pl.BlockSpec((None, None, L, H), lambda b, c: (b, c, 0, 0))
    row_spec = pl.BlockSpec((None, None, H, L), lambda b, c: (b, c, 0, 0))
    st = lambda b, c: (b, 0, 0, 0)
    h, Cn, nn, mn = pl.pallas_call(
        functools.partial(_mlstm_kernel, n_heads=H, dk=dk),
        grid=(B, nc),
        in_specs=[pl.BlockSpec((None, L, QW), lambda b, c: (b, c, 0)),
                  pl.BlockSpec((None, L, QW), lambda b, c: (b, c, 0)),
                  pl.BlockSpec((None, L, VW), lambda b, c: (b, c, 0)),
                  col_spec, row_spec, col_spec, row_spec,
                  pl.BlockSpec((None, HP, 2 * dk, dv), st),
                  pl.BlockSpec((None, HP, 1, 2 * dk), st),
                  pl.BlockSpec((None, 1, H), lambda b, c: (b, 0, 0))],
        out_specs=[pl.BlockSpec((None, L, VW), lambda b, c: (b, c, 0)),
                   pl.BlockSpec((None, HP, 2 * dk, dv), st),
                   pl.BlockSpec((None, HP, 1, 2 * dk), st),
                   pl.BlockSpec((None, 1, H), lambda b, c: (b, 0, 0))],
        out_shape=[jax.ShapeDtypeStruct((B, T, VW), F32),
                   jax.ShapeDtypeStruct((B, HP, 2 * dk, dv), F32),
                   jax.ShapeDtypeStruct((B, HP, 1, 2 * dk), F32),
                   jax.ShapeDtypeStruct((B, 1, H), F32)],
        compiler_params=_params("arbitrary", "arbitrary"),
        name="mlstm_chunked",
    )(q, k, v, ic, ir, bc, br, c0, n0p, m0p)
    return h, Cn.reshape(B, H, dk, dv), nn.reshape(B, H, dk), mn.reshape(B, H)


def _staircase(n):
    return [(a, n // (a + 1)) for a in range(n)]


def _extract_topk(s, payload, n):
    R = s.shape[0]
    ridx = lax.broadcasted_iota(I32, s.shape, 0)
    vals, pays = [], []
    for _ in range(n):
        m = jnp.max(s, axis=0, keepdims=True)
        am = jnp.min(jnp.where(s == m, ridx, R), axis=0, keepdims=True)
        hit = ridx == am
        if payload is None:
            pays.append(am)
        else:
            pays.append(jnp.max(jnp.where(hit, payload, -1), axis=0, keepdims=True))
        vals.append(m)
        s = jnp.where(hit, NEG_INF, s)
    return vals, pays


def _peer_topk_kernel(x_ref, sc_ref, sh_ref, gain_ref, wqh_ref, wql_ref, k1h_ref, k1l_ref, k2h_ref, k2l_ref,
                      hb_ref, e_ref, g_ref, q_scr, v1_scr, i1_scr, v2_scr, i2_scr, *, n_heads, n_keys):
    TM = x_ref.shape[0]
    K = PEER_TOPK
    half = k1h_ref.shape[2]
    h = _modnorm(x_ref[...], gain_ref[...], sc_ref[...], sh_ref[...])
    hh, hl = _split2(h)
    hb_ref[...] = hh
    q_scr[...] = _dot(hh, wqh_ref[...]) + _dot(hh, wql_ref[...]) + _dot(hl, wqh_ref[...])
    sub8 = lax.broadcasted_iota(I32, (8, LANES), 0)

    def head_body(hd, _):
        c0 = pl.multiple_of(hd * 2 * half, 2 * half)
        c1 = pl.multiple_of(hd * 2 * half + half, half)
        qa_h, qa_l = _split2(q_scr[:, pl.ds(c0, half)])
        qb_h, qb_l = _split2(q_scr[:, pl.ds(c1, half)])
        k1h = k1h_ref[hd]
        k1l = k1l_ref[hd]
        k2h = k2h_ref[hd]
        k2l = k2l_ref[hd]
        s1 = _dot_nt(k1h, qa_h) + _dot_nt(k1h, qa_l) + _dot_nt(k1l, qa_h)
        s2 = _dot_nt(k2h, qb_h) + _dot_nt(k2h, qb_l) + _dot_nt(k2l, qb_h)
        for blk in range(TM // LANES):
            ls = slice(blk * LANES, (blk + 1) * LANES)
            for s, v_scr, i_scr in ((s1, v1_scr, i1_scr), (s2, v2_scr, i2_scr)):
                vals, idxs = _extract_topk(s[:, ls], None, K)
                for r in range(K):
                    v_scr[r:r + 1, :] = vals[r]
                    i_scr[r:r + 1, :] = idxs[r]
            cand, pay = [], []
            for a, cnt in _staircase(K):
                if cnt == 1:
                    break
                for b0 in range(0, cnt, 8):
                    nb = min(8, cnt - b0)
                    cv = v1_scr[a:a + 1, :] + v2_scr[b0:b0 + 8, :]
                    pv = i1_scr[a:a + 1, :] * n_keys + i2_scr[b0:b0 + 8, :]
                    if nb < 8:
                        cv = jnp.where(sub8 < nb, cv, NEG_INF)
                    cand.append(cv)
                    pay.append(pv)
            a1 = K // 2
            cand.append(v1_scr[a1:K, :] + v2_scr[0:1, :])
            pay.append(i1_scr[a1:K, :] * n_keys + i2_scr[0:1, :])
            top_s, top_e = _extract_topk(jnp.concatenate(cand, axis=0), jnp.concatenate(pay, axis=0), K)
            ex = [jnp.exp(t - top_s[0]) for t in top_s]
            den = ex[0]
            for t in ex[1:]:
                den = den + t
            inv = 1.0 / den
            for r in range(K):
                i1_scr[r:r + 1, :] = top_e[r]
                v1_scr[r:r + 1, :] = ex[r] * inv
            out_rows = pl.ds(pl.multiple_of(hd * K, K), K)
            e_ref[out_rows, ls] = i1_scr[...]
            g_ref[out_rows, ls] = v1_scr[...]
        return 0

    lax.fori_loop(0, n_heads, head_body, 0)


def peer_retrieve(x, scale, shift, gain, w_q, sub_k1, sub_k2, *, tm, tiles_per_group):
    N, D = x.shape
    HP, NK, half = sub_k1.shape
    assert NK == LANES and half == LANES and PEER_TOPK % 8 == 0
    R = scale.shape[1]
    QW = w_q.shape[1]
    J = HP * PEER_TOPK
    wqh = w_q.astype(BF16)
    wql = (w_q - wqh.astype(F32)).astype(BF16)
    k1h = sub_k1.astype(BF16)
    k1l = (sub_k1 - k1h.astype(F32)).astype(BF16)
    k2h = sub_k2.astype(BF16)
    k2l = (sub_k2 - k2h.astype(F32)).astype(BF16)
    const = lambda i: (0, 0)
    const3 = lambda i: (0, 0, 0)
    kspec = pl.BlockSpec((HP, NK, half), const3)
    return pl.pallas_call(
        functools.partial(_peer_topk_kernel, n_heads=HP, n_keys=NK),
        grid=(N // tm,),
        in_specs=[pl.BlockSpec((tm, D), lambda i: (i, 0)),
                  _mod_spec(R, D, tiles_per_group), _mod_spec(R, D, tiles_per_group),
                  pl.BlockSpec((1, D), const),
                  pl.BlockSpec((D, QW), const), pl.BlockSpec((D, QW), const),
                  kspec, kspec, kspec, kspec],
        out_specs=[pl.BlockSpec((tm, D), lambda i: (i, 0)),
                   pl.BlockSpec((J, tm), lambda i: (0, i)),
                   pl.BlockSpec((J, tm), lambda i: (0, i))],
        out_shape=[jax.ShapeDtypeStruct((N, D), BF16),
                   jax.ShapeDtypeStruct((J, N), I32),
                   jax.ShapeDtypeStruct((J, N), F32)],
        scratch_shapes=[pltpu.VMEM((tm, QW), F32),
                        pltpu.VMEM((PEER_TOPK, LANES), F32), pltpu.VMEM((PEER_TOPK, LANES), I32),
                        pltpu.VMEM((PEER_TOPK, LANES), F32), pltpu.VMEM((PEER_TOPK, LANES), I32)],
        compiler_params=_params("arbitrary"),
        name="peer_retrieve",
    )(x, scale, shift, gain.reshape(1, D), wqh, wql, k1h, k1l, k2h, k2l)


def _peer_mix_kernel(hb_ref, i1_ref, i2_ref, g_ref, u_ref, v_ref, x_ref, g2_ref, o_ref,
                     a_scr, w_scr, hbuf_scr, y_scr, *, n_keys):
    TM = hb_ref.shape[0]
    TE = u_ref.shape[0]
    J = i1_ref.shape[1]
    NE = pl.num_programs(1) // 2
    j = pl.program_id(1)
    NP2 = n_keys // 2
    W2 = 2 * n_keys

    @pl.when(j == 0)
    def _():
        a_scr[...] = jnp.zeros(a_scr.shape, F32)

    @pl.when(j < NE)
    def _():
        act = _dot_nt(hb_ref[...], u_ref[...])
        i1 = i1_ref[...]
        i2 = i2_ref[...]
        acc = a_scr[...]
        for c in range(TE // n_keys):
            r = j * (TE // n_keys) + c
            picked = jnp.take_along_axis(act[:, c * n_keys:(c + 1) * n_keys], i2, axis=1)
            acc = jnp.where(i1 == r, picked, acc)
        a_scr[...] = acc

    @pl.when(j == NE)
    def _():
        a = a_scr[...]
        gelu = 0.5 * a * (1.0 + lax.erf(a * math.sqrt(0.5)))
        w_scr[...] = g_ref[...] * gelu
        y_scr[...] = jnp.zeros(y_scr.shape, F32)
        rows = lax.broadcasted_iota(I32, (NP2, J), 0)
        cols = lax.broadcasted_iota(I32, (W2, J), 0)

        def token_body(t, _):
            i1 = i1_ref[pl.ds(t, 1), :]
            i2 = i2_ref[pl.ds(t, 1), :]
            w = w_scr[pl.ds(t, 1), :]
            p1 = jnp.where(rows == (i1 >> 1), w, 0.0).astype(BF16)
            p2 = jnp.where(cols == ((i1 & 1) * n_keys + i2), 1.0, 0.0).astype(BF16)
            grid = _dot_nt(p1, p2)
            dst = pl.ds(pl.multiple_of(t * NP2, NP2), NP2)
            hbuf_scr[0, dst, :] = grid[:, 0:n_keys]
            hbuf_scr[1, dst, :] = grid[:, n_keys:W2]
            return 0

        lax.fori_loop(0, TM, token_body, 0)

    @pl.when(j >= NE)
    def _():
        jj = j - NE
        y = y_scr[...]
        for c in range(TE // W2):
            rp = jj * (TE // W2) + c
            src = pl.ds(rp, TM, stride=NP2)
            lhs = jnp.concatenate([hbuf_scr[0, src, :], hbuf_scr[1, src, :]], axis=1)
            y = y + _dot(lhs.astype(BF16), v_ref[c * W2:(c + 1) * W2, :])
        y_scr[...] = y

    @pl.when(j == 2 * NE - 1)
    def _():
        o_ref[...] = x_ref[...] + g2_ref[...] * y_scr[...]


def peer_mix(hb, i1, i2, g, u_tab, v_tab, x, g2, *, tm, te, tiles_per_group, n_keys):
    N, D = x.shape
    E = u_tab.shape[0]
    J = i1.shape[1]
    NE = E // te
    R = g2.shape[1]
    assert te % (2 * n_keys) == 0 and n_keys == LANES
    return pl.pallas_call(
        functools.partial(_peer_mix_kernel, n_keys=n_keys),
        grid=(N // tm, 2 * NE),
        in_specs=[pl.BlockSpec((tm, D), lambda i, j: (i, 0)),
                  pl.BlockSpec((tm, J), lambda i, j: (i, 0)),
                  pl.BlockSpec((tm, J), lambda i, j: (i, 0)),
                  pl.BlockSpec((tm, J), lambda i, j: (i, 0)),
                  pl.BlockSpec((te, D), lambda i, j: (jnp.minimum(j, NE - 1), 0)),
                  pl.BlockSpec((te, D), lambda i, j: (jnp.maximum(j - NE, 0), 0)),
                  pl.BlockSpec((tm, D), lambda i, j: (i, 0)),
                  pl.BlockSpec((None, R, D), lambda i, j: (i // tiles_per_group, 0, 0))],
        out_specs=pl.BlockSpec((tm, D), lambda i, j: (i, 0)),
        out_shape=jax.ShapeDtypeStruct((N, D), F32),
        scratch_shapes=[pltpu.VMEM((tm, J), F32), pltpu.VMEM((tm, J), F32),
                        pltpu.VMEM((2, tm * (n_keys // 2), n_keys), F32),
                        pltpu.VMEM((tm, D), F32)],
        compiler_params=_params("arbitrary", "arbitrary"),
        name="peer_mix",
    )(hb, i1, i2, g, u_tab, v_tab, x, g2)


def _trunk(x3, mods, past, params, tables, *, tm):
    (norm_mix, norm_ffn, fox_w_in, fox_b_f, fox_q_norm, fox_k_norm, fox_w_out,
     ml_w_in, ml_b_i, ml_b_f, ml_h_norm, ml_w_out, peer_w_q, peer_sub_k1, peer_sub_k2) = params
    u_bf, v_bf = tables
    B, T, D = x3.shape
    N = B * T
    depth = norm_mix.shape[0]
    n_fox_heads = fox_b_f.shape[1]
    n_keys = peer_sub_k1.shape[2]
    x = x3.reshape(N, D)
    if T % tm == 0:
        tpg, rows = T // tm, 1
        expand = lambda a: a.reshape(B, 1, D)
    else:
        assert N == tm
        tpg, rows = 1, N
        expand = lambda a: jnp.repeat(a, T, axis=0).reshape(1, N, D)
    new_k, new_v, new_lf, new_C, new_n, new_m = [], [], [], [], [], []
    for layer in range(depth):
        sh1, sc1, g1, sh2, sc2, g2 = [expand(a) for a in jnp.split(mods[layer], 6, axis=-1)]
        j = layer // 2
        if layer % 2 == 0:
            q, k, v, kb, vb, sg, lf = fox_project(
                x, sc1, sh1, norm_mix[layer], fox_w_in[j], fox_b_f[j], fox_q_norm[j], fox_k_norm[j],
                tm=tm, tiles_per_group=tpg)
            if past is None:
                F = seg_cumsum(lf.reshape(B, T, n_fox_heads).transpose(0, 2, 1), T).transpose(0, 2, 1)
                o = fox_attend_prompt(q.reshape(B, T, D), kb.reshape(B, T, D), vb.reshape(B, T, D), F,
                                      n_heads=n_fox_heads)
            else:
                cache_k, cache_v, cache_logf, _, _, _, page_table = past
                o = fox_attend_sample(q.reshape(B, T, D), kb.reshape(B, T, D), vb.reshape(B, T, D),
                                      lf.reshape(B, T, n_fox_heads), cache_k, cache_v, cache_logf,
                                      page_table, j, n_heads=n_fox_heads)
            x = out_project(o.reshape(N, D), sg, fox_w_out[j], x, g1, jnp.ones((D,), F32),
                            tm=tm, tiles_per_group=tpg, head_norm=False)
            new_k.append(k.reshape(B, T, n_fox_heads, -1))
            new_v.append(v.reshape(B, T, n_fox_heads, -1))
            new_lf.append(lf.reshape(B, T, n_fox_heads))
        else:
            n_ml_heads = ml_b_i.shape[1]
            v_width = ml_w_out.shape[1]
            q, k, v, so, il, fl = ml_project(x, sc1, sh1, norm_mix[layer], ml_w_in[j], ml_b_i[j], ml_b_f[j],
                                             tm=tm, tiles_per_group=tpg, v_width=v_width)
            QW = q.shape[1]
            dk = QW // n_ml_heads
            dv = v_width // n_ml_heads
            q3, k3, v3 = q.reshape(B, T, QW), k.reshape(B, T, QW), v.reshape(B, T, v_width)
            il3, fl3 = il.reshape(B, T, n_ml_heads), fl.reshape(B, T, n_ml_heads)
            if past is None:
                C0 = jnp.zeros((B, n_ml_heads, dk, dv), F32)
                n0 = jnp.zeros((B, n_ml_heads, dk), F32)
                m0 = jnp.zeros((B, n_ml_heads), F32)
                hh, C, n_, m_ = mlstm_chunked(q3, k3, v3, il3, fl3, C0, n0, m0, 64)
            else:
                _, _, _, state_C, state_n, state_m, _ = past
                Tp = 16
                pad3 = ((0, 0), (0, Tp - T), (0, 0))
                hh, C, n_, m_ = mlstm_chunked(
                    jnp.pad(q3, pad3), jnp.pad(k3, pad3), jnp.pad(v3, pad3),
                    jnp.pad(il3, pad3, constant_values=-1e30), jnp.pad(fl3, pad3),
                    state_C[j], state_n[j], state_m[j], Tp)
                hh = hh[:, :T]
            x = out_project(hh.reshape(N, v_width), so, ml_w_out[j], x, g1, ml_h_norm[j].reshape(-1),
                            tm=tm, tiles_per_group=tpg, head_norm=True)
            new_C.append(C)
            new_n.append(n_)
            new_m.append(m_)
        hb, e, g = peer_retrieve(x, sc2, sh2, norm_ffn[layer], peer_w_q[layer], peer_sub_k1[layer],
                                 peer_sub_k2[layer], tm=tm, tiles_per_group=tpg)
        eT = e.T
        x = peer_mix(hb, eT // n_keys, eT % n_keys, g.T, u_bf[layer], v_bf[layer], x, g2,
                     tm=tm, te=512, tiles_per_group=tpg, n_keys=n_keys)
    return (x.reshape(B, T, D), jnp.stack(new_k), jnp.stack(new_v), jnp.stack(new_lf),
            jnp.stack(new_C), jnp.stack(new_n), jnp.stack(new_m))


def kernel(x_prompt, x_sample, cache_k, cache_v, cache_logf, state_C, state_n, state_m, page_table,
           c_prompt, c_sample, ada_w, ada_b, norm_mix, norm_ffn, fox_w_in, fox_b_f, fox_q_norm, fox_k_norm,
           fox_w_out, ml_w_in, ml_b_i, ml_b_f, ml_h_norm, ml_w_out, peer_w_q, peer_sub_k1, peer_sub_k2,
           peer_u, peer_v):
    params = (norm_mix, norm_ffn, fox_w_in, fox_b_f, fox_q_norm, fox_k_norm, fox_w_out,
              ml_w_in, ml_b_i, ml_b_f, ml_h_norm, ml_w_out, peer_w_q, peer_sub_k1, peer_sub_k2)
    tables = (peer_u.astype(BF16), peer_v.astype(BF16))
    Bp = c_prompt.shape[0]
    mods = adaln(jnp.concatenate([c_prompt, c_sample], axis=0), ada_w, ada_b)
    y_p, k_p, v_p, lf_p, C_p, n_p, m_p = _trunk(x_prompt, mods[:, :Bp], None, params, tables, tm=256)
    past = (cache_k, cache_v, cache_logf, state_C, state_n, state_m, page_table)
    y_s, k_s, v_s, lf_s, C_s, n_s, m_s = _trunk(x_sample, mods[:, Bp:], past, params, tables,
                                               tm=x_sample.shape[0] * x_sample.shape[1])
    return (y_p, y_s, k_p, v_p, lf_p, C_p, n_p, m_p, k_s, v_s, lf_s, C_s, n_s, m_s)
```

```python
import functools
import math

import jax
import jax.numpy as jnp
from jax import lax
from jax.experimental import pallas as pl
from jax.experimental.pallas import tpu as pltpu

F32 = jnp.float32
BF16 = jnp.bfloat16
I32 = jnp.int32

RMS_EPS = 1e-6
LANES = 128
MXU_DIM = 256
VMEM_LIMIT = 52 * 1024 * 1024
PEER_TOPK = 16
PAGE_SIZE = 128
NEG_INF = float("-inf")


def _dot(a, b):
    return jnp.dot(a, b, preferred_element_type=F32)


def _dot_nt(a, b):
    return lax.dot_general(a, b, (((1,), (1,)), ((), ())), preferred_element_type=F32)


def _dot_tn(a, b):
    return lax.dot_general(a, b, (((0,), (0,)), ((), ())), preferred_element_type=F32)


def _split2(a):
    hi = a.astype(BF16)
    lo = (a - hi.astype(F32)).astype(BF16)
    return hi, lo


def _dot3(a, b):
    ah, al = _split2(a)
    bh, bl = _split2(b)
    return _dot(ah, bh) + _dot(ah, bl) + _dot(al, bh)


def _modnorm(x, gain, scale, shift):
    ms = jnp.mean(x * x, axis=-1, keepdims=True)
    return x * lax.rsqrt(ms + RMS_EPS) * gain * (1.0 + scale) + shift


def _log_sigmoid(z):
    return jnp.minimum(z, 0.0) - jnp.log1p(jnp.exp(-jnp.abs(z)))


def _params(*sem):
    return pltpu.CompilerParams(dimension_semantics=sem, vmem_limit_bytes=VMEM_LIMIT)


def _mod_spec(rows, d, tiles_per_group):
    return pl.BlockSpec((None, rows, d), lambda i: (i // tiles_per_group, 0, 0))


def _adaln_kernel(c_ref, w_ref, b_ref, o_ref):
    c = c_ref[...]
    o_ref[...] = _dot3(c * jax.nn.sigmoid(c), w_ref[...]) + b_ref[...]


def adaln(c, ada_w, ada_b, tn=1536):
    L, D, D6 = ada_w.shape
    R = c.shape[0]
    return pl.pallas_call(
        _adaln_kernel,
        grid=(L, D6 // tn),
        in_specs=[pl.BlockSpec((R, D), lambda l, j: (0, 0)),
                  pl.BlockSpec((None, D, tn), lambda l, j: (l, 0, j)),
                  pl.BlockSpec((None, 1, tn), lambda l, j: (l, 0, j))],
        out_specs=pl.BlockSpec((None, R, tn), lambda l, j: (l, 0, j)),
        out_shape=jax.ShapeDtypeStruct((L, R, D6), F32),
        compiler_params=_params("arbitrary", "arbitrary"),
        name="adaln",
    )(c, ada_w, ada_b.reshape(L, 1, D6))


def _segcumsum_kernel(x_ref, o_ref, *, seg):
    C, T = x_ref.shape
    lane = lax.broadcasted_iota(I32, (C, LANES), 1)
    w = min(seg, LANES)
    carry = None
    for blk in range(T // LANES):
        x = x_ref[:, blk * LANES:(blk + 1) * LANES]
        k = 1
        while k < w:
            x = x + jnp.where((lane & (w - 1)) >= k, pltpu.roll(x, k, 1), 0.0)
            k *= 2
        if seg > LANES:
            if blk % (seg // LANES) != 0:
                x = x + carry
            carry = x[:, LANES - 1:LANES]
        o_ref[:, blk * LANES:(blk + 1) * LANES] = x


def seg_cumsum(x, seg):
    R, C, T = x.shape
    assert T % LANES == 0 and (seg & (seg - 1)) == 0
    assert T % seg == 0 and (seg <= LANES or seg % LANES == 0)
    return pl.pallas_call(
        functools.partial(_segcumsum_kernel, seg=seg),
        grid=(R,),
        in_specs=[pl.BlockSpec((None, C, T), lambda r: (r, 0, 0))],
        out_specs=pl.BlockSpec((None, C, T), lambda r: (r, 0, 0)),
        out_shape=jax.ShapeDtypeStruct((R, C, T), F32),
        compiler_params=_params("arbitrary"),
        name="seg_cumsum",
    )(x)


def _pair_headnorm(blk, gain2, hd):
    lane = lax.broadcasted_iota(I32, blk.shape, 1)
    lo = lane < hd
    sq = blk * blk
    s_lo = jnp.sum(jnp.where(lo, sq, 0.0), axis=-1, keepdims=True)
    s_hi = jnp.sum(jnp.where(lo, 0.0, sq), axis=-1, keepdims=True)
    inv = lax.rsqrt(jnp.where(lo, s_lo, s_hi) * (1.0 / hd) + RMS_EPS)
    return blk * inv * gain2


def _fox_proj_kernel(x_ref, sc_ref, sh_ref, gain_ref, w_ref, wfh_ref, wfl_ref, bf_ref, qg_ref, kg_ref,
                     q_ref, k_ref, v_ref, kb_ref, vb_ref, sg_ref, lf_ref, *, n_heads):
    D = x_ref.shape[1]
    hd = D // n_heads
    h = _modnorm(x_ref[...], gain_ref[...], sc_ref[...], sh_ref[...])
    hh, hl = _split2(h)
    qk_scale = hd ** -0.5
    q = _dot(hh, w_ref[:, 0:D])
    for c in range(D // LANES):
        sl = slice(c * LANES, (c + 1) * LANES)
        q_ref[:, sl] = (_pair_headnorm(q[:, sl], qg_ref[...], hd) * qk_scale).astype(BF16)
    k = _dot(hh, w_ref[:, D:2 * D])
    for c in range(D // LANES):
        sl = slice(c * LANES, (c + 1) * LANES)
        kn = _pair_headnorm(k[:, sl], kg_ref[...], hd)
        k_ref[:, sl] = kn
        kb_ref[:, sl] = kn.astype(BF16)
    v = _dot(hh, w_ref[:, 2 * D:3 * D])
    v_ref[...] = v
    vb_ref[...] = v.astype(BF16)
    sg_ref[...] = jax.nn.sigmoid(_dot(hh, w_ref[:, 3 * D:4 * D]))
    f = _dot(hh, wfh_ref[...]) + _dot(hh, wfl_ref[...]) + _dot(hl, wfh_ref[...])
    lf_ref[...] = _log_sigmoid(f[:, 0:n_heads] + bf_ref[...])


def fox_project(x, scale, shift, gain, w_in, b_f, q_gain, k_gain, *, tm, tiles_per_group):
    N, D = x.shape
    H = b_f.shape[0]
    hd = D // H
    assert 2 * hd == LANES
    R = scale.shape[1]
    w_main = jnp.concatenate([w_in[:, :3 * D], w_in[:, 3 * D + H:]], axis=1).astype(BF16)
    w_f = jnp.pad(w_in[:, 3 * D:3 * D + H], ((0, 0), (0, LANES - H)))
    wfh = w_f.astype(BF16)
    wfl = (w_f - wfh.astype(F32)).astype(BF16)
    const = lambda i: (0, 0)
    row = lambda i: (i, 0)
    outs = pl.pallas_call(
        functools.partial(_fox_proj_kernel, n_heads=H),
        grid=(N // tm,),
        in_specs=[pl.BlockSpec((tm, D), row),
                  _mod_spec(R, D, tiles_per_group), _mod_spec(R, D, tiles_per_group),
                  pl.BlockSpec((1, D), const),
                  pl.BlockSpec((D, 4 * D), const),
                  pl.BlockSpec((D, LANES), const), pl.BlockSpec((D, LANES), const),
                  pl.BlockSpec((1, H), const),
                  pl.BlockSpec((1, LANES), const), pl.BlockSpec((1, LANES), const)],
        out_specs=[pl.BlockSpec((tm, D), row)] * 6 + [pl.BlockSpec((tm, H), row)],
        out_shape=[jax.ShapeDtypeStruct((N, D), BF16),
                   jax.ShapeDtypeStruct((N, D), F32), jax.ShapeDtypeStruct((N, D), F32),
                   jax.ShapeDtypeStruct((N, D), BF16), jax.ShapeDtypeStruct((N, D), BF16),
                   jax.ShapeDtypeStruct((N, D), F32),
                   jax.ShapeDtypeStruct((N, H), F32)],
        compiler_params=_params("arbitrary"),
        name="fox_project",
    )(x, scale, shift, gain.reshape(1, D), w_main, wfh, wfl, b_f.reshape(1, H),
      jnp.tile(q_gain, 2).reshape(1, LANES), jnp.tile(k_gain, 2).reshape(1, LANES))
    return outs


def _fox_attn_kernel(q_ref, k_ref, v_ref, fc_ref, fr_ref, o_ref, *, hd, n_sub):
    TQ = q_ref.shape[0]
    SQ = TQ // n_sub
    qi = pl.program_id(2)
    lane = lax.broadcasted_iota(I32, (SQ, LANES), 1)
    row = lax.broadcasted_iota(I32, (SQ, TQ), 0)
    col = lax.broadcasted_iota(I32, (SQ, TQ), 1)
    chains = [(hh, sb) for hh in range(2) for sb in range(n_sub)]
    qms, fqs = [], []
    for hh, sb in chains:
        q = q_ref[sb * SQ:(sb + 1) * SQ, :]
        qms.append(jnp.where((lane >= hh * hd) & (lane < (hh + 1) * hd), q, jnp.zeros_like(q)))
        fqs.append(fc_ref[sb * SQ:(sb + 1) * SQ, hh:hh + 1])

    def block_scores(j):
        start = pl.multiple_of(j * TQ, TQ)
        kj = k_ref[pl.ds(start, TQ), :]
        return tuple(_dot_nt(qms[c], kj) + (fqs[c] - fr_ref[hh:hh + 1, pl.ds(start, TQ)])
                     for c, (hh, sb) in enumerate(chains))

    def update(j, state, scores, masked):
        vj = v_ref[pl.ds(pl.multiple_of(j * TQ, TQ), TQ), :]
        new = []
        for c, (hh, sb) in enumerate(chains):
            m, l, acc = state[c]
            s = scores[c]
            if masked:
                s = jnp.where(col <= row + sb * SQ, s, NEG_INF)
            m_new = jnp.maximum(m, jnp.max(s, axis=-1, keepdims=True))
            alpha = jnp.exp(m - m_new)
            p = jnp.exp(s - m_new)
            l = alpha * l + jnp.sum(p, axis=-1, keepdims=True)
            acc = alpha * acc + _dot(p.astype(BF16), vj)
            new.append((m_new, l, acc))
        return tuple(new)

    def step(j, carry):
        state, scores = carry
        nxt = block_scores(j + 1)
        return update(j, state, scores, False), nxt

    init = tuple((jnp.full((SQ, 1), NEG_INF, F32), jnp.zeros((SQ, 1), F32), jnp.zeros((SQ, LANES), F32))
                 for _ in chains)
    state, scores = lax.fori_loop(0, qi, step, (init, block_scores(0)))
    final = update(qi, state, scores, True)
    for sb in range(n_sub):
        o0 = final[sb][2] / final[sb][1]
        o1 = final[n_sub + sb][2] / final[n_sub + sb][1]
        o_ref[sb * SQ:(sb + 1) * SQ, :] = jnp.where(lane < hd, o0, o1)


def fox_attend_prompt(q, kb, vb, F, *, n_heads, tq=256):
    B, S, D = q.shape
    hd = D // n_heads
    HP = n_heads // 2
    fcol = F.reshape(B, S, HP, 2).transpose(0, 2, 1, 3)
    frow = fcol.transpose(0, 1, 3, 2)
    return pl.pallas_call(
        functools.partial(_fox_attn_kernel, hd=hd, n_sub=2),
        grid=(B, HP, S // tq),
        in_specs=[pl.BlockSpec((None, tq, LANES), lambda b, h, i: (b, i, h)),
                  pl.BlockSpec((None, S, LANES), lambda b, h, i: (b, 0, h)),
                  pl.BlockSpec((None, S, LANES), lambda b, h, i: (b, 0, h)),
                  pl.BlockSpec((None, None, tq, 2), lambda b, h, i: (b, h, i, 0)),
                  pl.BlockSpec((None, None, 2, S), lambda b, h, i: (b, h, 0, 0))],
        out_specs=pl.BlockSpec((None, tq, LANES), lambda b, h, i: (b, i, h)),
        out_shape=jax.ShapeDtypeStruct((B, S, D), F32),
        compiler_params=_params("arbitrary", "arbitrary", "arbitrary"),
        name="fox_attend_prompt",
    )(q, kb, vb, fcol, frow)


def _fox_decode_kernel(pt_ref, qbd_ref, *refs, n_heads, pp):
    del pt_ref
    kt_refs, vt_refs, cf_refs = refs[:pp], refs[pp:2 * pp], refs[2 * pp:3 * pp]
    kn_ref, vn_ref, cn_ref, o_ref, m_scr, l_scr, acc_scr, carry_scr = refs[3 * pp:]
    H = n_heads
    step = pl.program_id(1)
    n_steps = pl.num_programs(1)
    R, D = qbd_ref.shape
    T = R // H
    P = cf_refs[0].shape[1]

    @pl.when(step == 0)
    def _():
        m_scr[...] = jnp.full(m_scr.shape, NEG_INF, F32)
        l_scr[...] = jnp.zeros(l_scr.shape, F32)
        acc_scr[...] = jnp.zeros(acc_scr.shape, F32)
        carry_scr[...] = jnp.zeros(carry_scr.shape, F32)

    qbd = qbd_ref[...]

    def merge(m_i, l_i, acc_i):
        m = m_scr[...]
        m_new = jnp.maximum(m, m_i)
        a_old = jnp.exp(m - m_new)
        a_new = jnp.exp(m_i - m_new)
        l_scr[...] = a_old * l_scr[...] + a_new * l_i
        acc_scr[...] = a_old * acc_scr[...] + a_new * acc_i
        m_scr[...] = m_new

    def scores(s, fk):
        return s - jnp.concatenate([fk] * T, axis=0)

    carry = carry_scr[...]
    ss = []
    for i in range(pp):
        cf = cf_refs[i][...]
        ss.append(scores(_dot(qbd, kt_refs[i][...].astype(BF16)), cf + carry))
        carry = carry + cf[:, P - 1:P]
    carry_scr[...] = carry
    m_s = jnp.max(ss[0], axis=-1, keepdims=True)
    for s in ss[1:]:
        m_s = jnp.maximum(m_s, jnp.max(s, axis=-1, keepdims=True))
    l_s = jnp.zeros_like(m_s)
    acc_s = jnp.zeros(acc_scr.shape, F32)
    for i, s in enumerate(ss):
        pr = jnp.exp(s - m_s)
        l_s = l_s + jnp.sum(pr, axis=-1, keepdims=True)
        acc_s = acc_s + _dot_nt(pr.astype(BF16), vt_refs[i][...].astype(BF16))
    merge(m_s, l_s, acc_s)

    @pl.when(step == n_steps - 1)
    def _():
        s2 = scores(_dot_nt(qbd, kn_ref[...]), cn_ref[...] + carry_scr[...])
        key = lax.broadcasted_iota(I32, s2.shape, 1)
        t = lax.broadcasted_iota(I32, s2.shape, 0) // H
        s2 = jnp.where(key <= t, s2, NEG_INF)
        m2 = jnp.max(s2, axis=-1, keepdims=True)
        p2 = jnp.exp(s2 - m2)
        merge(m2, jnp.sum(p2, axis=-1, keepdims=True), _dot(p2.astype(BF16), vn_ref[...]))
        out = acc_scr[...] / l_scr[...]
        rh = lax.broadcasted_iota(I32, (R, D), 0) % H
        ch = lax.broadcasted_iota(I32, (R, D), 1) // (D // H)
        out = jnp.where(rh == ch, out, 0.0)
        o_ref[...] = jnp.sum(out.reshape(T, H, D), axis=1)


def fox_attend_sample(q, kb_new, vb_new, lf_new, cache_kt, cache_vt, cache_cf, page_table, j, *, n_heads):
    Bd, T, D = q.shape
    H = n_heads
    hd = D // H
    n_pages = page_table.shape[1]
    n_fox, n_pool, _, P = cache_cf.shape
    R = T * H
    pp = 4 if n_pages % 4 == 0 else 1
    assert T <= P
    q4 = q.reshape(Bd, T, H, hd)
    eye = jnp.eye(H, dtype=q.dtype)
    qbd = (q4[:, :, :, None, :] * eye[None, None, :, :, None]).reshape(Bd, R, D)
    pad = ((0, 0), (0, P - T), (0, 0))
    kn = jnp.pad(kb_new, pad)
    vn = jnp.pad(vb_new, pad)
    cn = seg_cumsum(jnp.pad(lf_new.transpose(0, 2, 1), ((0, 0), (0, 0), (0, P - T))), P)
    pt = page_table.reshape(-1).astype(I32)

    def page_spec(shape, i):
        return pl.BlockSpec((None, None) + shape, lambda b, s, pt: (j, pt[b * n_pages + s * pp + i], 0, 0))

    per_seq = lambda shape: pl.BlockSpec((None,) + shape, lambda b, s, pt: (b, 0, 0))
    grid_spec = pltpu.PrefetchScalarGridSpec(
        num_scalar_prefetch=1,
        grid=(Bd, n_pages // pp),
        in_specs=([per_seq((R, D))]
                  + [page_spec((D, P), i) for i in range(pp)]
                  + [page_spec((D, P), i) for i in range(pp)]
                  + [page_spec((H, P), i) for i in range(pp)]
                  + [per_seq((P, D)), per_seq((P, D)), per_seq((H, P))]),
        out_specs=per_seq((T, D)),
        scratch_shapes=[pltpu.VMEM((R, 1), F32), pltpu.VMEM((R, 1), F32),
                        pltpu.VMEM((R, D), F32), pltpu.VMEM((H, 1), F32)],
    )
    return pl.pallas_call(
        functools.partial(_fox_decode_kernel, n_heads=H, pp=pp),
        grid_spec=grid_spec,
        out_shape=jax.ShapeDtypeStruct((Bd, T, D), F32),
        compiler_params=_params("arbitrary", "arbitrary"),
        name="fox_attend_sample",
    )(pt, qbd, *([cache_kt] * pp), *([cache_vt] * pp), *([cache_cf] * pp), kn, vn, cn)


def prepare_cache(cache_k, cache_v, cache_logf):
    n_fox, n_pool, P, H, hd = cache_k.shape
    kt = cache_k.transpose(0, 1, 3, 4, 2).reshape(n_fox, n_pool, H * hd, P)
    vt = cache_v.transpose(0, 1, 3, 4, 2).reshape(n_fox, n_pool, H * hd, P)
    rows = n_fox * n_pool * H
    group = 2048 if rows % 2048 == 0 else H
    cf = seg_cumsum(cache_logf.transpose(0, 1, 3, 2).reshape(rows // group, group, P), P)
    return kt, vt, cf.reshape(n_fox, n_pool, H, P)


def _outproj_kernel(a_ref, gate_ref, w_ref, x_ref, g_ref, hn_ref, o_ref, *, head_norm):
    a = a_ref[...]
    D = a.shape[1]
    if head_norm:
        blocks = []
        for c in range(D // LANES):
            sl = slice(c * LANES, (c + 1) * LANES)
            blk = a[:, sl]
            ms = jnp.mean(blk * blk, axis=-1, keepdims=True)
            blocks.append(blk * lax.rsqrt(ms + RMS_EPS) * hn_ref[:, sl])
        a = jnp.concatenate(blocks, axis=-1)
    y = (a * gate_ref[...]).astype(BF16)
    o_ref[...] = x_ref[...] + g_ref[...] * _dot(y, w_ref[...])


def out_project(a, gate, w_out, x, g, head_gain, *, tm, tiles_per_group, head_norm):
    N, D = x.shape
    R = g.shape[1]
    row = lambda i: (i, 0)
    const = lambda i: (0, 0)
    return pl.pallas_call(
        functools.partial(_outproj_kernel, head_norm=head_norm),
        grid=(N // tm,),
        in_specs=[pl.BlockSpec((tm, D), row), pl.BlockSpec((tm, D), row),
                  pl.BlockSpec((D, D), const), pl.BlockSpec((tm, D), row),
                  _mod_spec(R, D, tiles_per_group), pl.BlockSpec((1, D), const)],
        out_specs=pl.BlockSpec((tm, D), row),
        out_shape=jax.ShapeDtypeStruct((N, D), F32),
        compiler_params=_params("arbitrary"),
        name="out_project",
    )(a, gate, w_out.astype(BF16), x, g, head_gain.reshape(1, D))


def _ml_proj_kernel(x_ref, sc_ref, sh_ref, gain_ref, w_ref, wgh_ref, wgl_ref, bi_ref, bf_ref,
                    q_ref, k_ref, v_ref, so_ref, il_ref, fl_ref, *, n_heads, qk_width):
    D = x_ref.shape[1]
    QW = qk_width
    h = _modnorm(x_ref[...], gain_ref[...], sc_ref[...], sh_ref[...])
    hh, hl = _split2(h)
    k_scale = (QW // n_heads) ** -0.5
    q_ref[...] = _dot(hh, w_ref[:, 0:QW]).astype(BF16)
    k_ref[...] = (_dot(hh, w_ref[:, QW:2 * QW]) * k_scale).astype(BF16)
    v_ref[...] = _dot(hh, w_ref[:, 2 * QW:2 * QW + D])
    so_ref[...] = jax.nn.sigmoid(_dot(hh, w_ref[:, 2 * QW + D:2 * QW + 2 * D]))
    gts = _dot(hh, wgh_ref[...]) + _dot(hh, wgl_ref[...]) + _dot(hl, wgh_ref[...])
    il_ref[...] = gts[:, 0:n_heads] + bi_ref[...]
    fl_ref[...] = _log_sigmoid(gts[:, n_heads:2 * n_heads] + bf_ref[...])


def ml_project(x, scale, shift, gain, w_in, b_i, b_f, *, tm, tiles_per_group, v_width):
    N, D = x.shape
    H = b_i.shape[0]
    QW = (w_in.shape[1] - 2 * v_width - 2 * H) // 2
    R = scale.shape[1]
    g0 = 2 * QW + v_width
    w_main = jnp.concatenate([w_in[:, :g0], w_in[:, g0 + 2 * H:]], axis=1).astype(BF16)
    w_g = jnp.pad(w_in[:, g0:g0 + 2 * H], ((0, 0), (0, LANES - 2 * H)))
    wgh = w_g.astype(BF16)
    wgl = (w_g - wgh.astype(F32)).astype(BF16)
    const = lambda i: (0, 0)
    row = lambda i: (i, 0)
    WM = w_main.shape[1]
    return pl.pallas_call(
        functools.partial(_ml_proj_kernel, n_heads=H, qk_width=QW),
        grid=(N // tm,),
        in_specs=[pl.BlockSpec((tm, D), row),
                  _mod_spec(R, D, tiles_per_group), _mod_spec(R, D, tiles_per_group),
                  pl.BlockSpec((1, D), const),
                  pl.BlockSpec((D, WM), const),
                  pl.BlockSpec((D, LANES), const), pl.BlockSpec((D, LANES), const),
                  pl.BlockSpec((1, H), const), pl.BlockSpec((1, H), const)],
        out_specs=[pl.BlockSpec((tm, QW), row), pl.BlockSpec((tm, QW), row),
                   pl.BlockSpec((tm, v_width), row), pl.BlockSpec((tm, v_width), row),
                   pl.BlockSpec((tm, H), row), pl.BlockSpec((tm, H), row)],
        out_shape=[jax.ShapeDtypeStruct((N, QW), BF16), jax.ShapeDtypeStruct((N, QW), BF16),
                   jax.ShapeDtypeStruct((N, v_width), F32), jax.ShapeDtypeStruct((N, v_width), F32),
                   jax.ShapeDtypeStruct((N, H), F32), jax.ShapeDtypeStruct((N, H), F32)],
        compiler_params=_params("arbitrary"),
        name="ml_project",
    )(x, scale, shift, gain.reshape(1, D), w_main, wgh, wgl, b_i.reshape(1, H), b_f.reshape(1, H))


def _mlstm_kernel(q_ref, k_ref, v_ref, ic_ref, ir_ref, bc_ref, br_ref, c0_ref, n0_ref, m0_ref,
                  h_ref, c_ref, n_ref, m_ref, *, n_heads, dk):
    L = q_ref.shape[0]
    ci = pl.program_id(1)

    @pl.when(ci == 0)
    def _():
        c_ref[...] = c0_ref[...]
        n_ref[...] = n0_ref[...]
        m_ref[...] = m0_ref[...]

    q = q_ref[...]
    k = k_ref[...]
    ic = ic_ref[...]
    ir = ir_ref[...]
    bc = bc_ref[...]
    br = br_ref[...]
    m_all = m_ref[...]
    tri = lax.broadcasted_iota(I32, (L, L), 1) <= lax.broadcasted_iota(I32, (L, L), 0)
    lane = lax.broadcasted_iota(I32, (L, LANES), 1)
    lane1 = lax.broadcasted_iota(I32, (1, LANES), 1)
    rsub = lax.broadcasted_iota(I32, (LANES, LANES), 0)
    dv = LANES
    for pr in range(n_heads // 2):
        sl = slice(pr * LANES, (pr + 1) * LANES)
        qp = q[:, sl]
        kp = k[:, sl]
        kpf = kp.astype(F32)
        qpf = qp.astype(F32)
        c_pair = c_ref[pr]
        n_pair = n_ref[pr]
        cb = c_pair.astype(BF16)
        new_c, new_n = [], []
        for hh in range(2):
            h = 2 * pr + hh
            sel = (lane >= hh * dk) & (lane < (hh + 1) * dk)
            qm = jnp.where(sel, qp, jnp.zeros_like(qp))
            b_l = bc[:, h:h + 1]
            b_s = br[h:h + 1, :]
            ig_s = ir[h:h + 1, :]
            ig_l = ic[:, h:h + 1]
            m = m_all[:, h:h + 1]
            dm = jnp.where(tri, b_l - b_s + ig_s, NEG_INF)
            inter = b_l + m
            m_t = jnp.maximum(inter, jnp.max(dm, axis=-1, keepdims=True))
            w_inter = jnp.exp(inter - m_t)
            a = jnp.exp(dm - m_t) * _dot_nt(qm, kp)
            vh = v_ref[:, h * dv:(h + 1) * dv]
            num = w_inter * _dot(qm, cb) + _dot(a.astype(BF16), vh.astype(BF16))
            qn = jnp.sum(jnp.where(sel, qpf, 0.0) * n_pair, axis=-1, keepdims=True)
            den = w_inter * qn + jnp.sum(a, axis=-1, keepdims=True)
            h_ref[:, h * dv:(h + 1) * dv] = num / jnp.maximum(jnp.abs(den), jnp.exp(-m_t))
            b_last = b_l[L - 1:L, :]
            g_row = b_last - b_s + ig_s
            m_new = jnp.maximum(b_last + m, jnp.max(g_row, axis=-1, keepdims=True))
            wk = jnp.exp(b_last - b_l + ig_l - m_new)
            decay = jnp.exp(b_last + m - m_new)
            upd = _dot_tn(kp, (wk * vh).astype(BF16))
            new_c.append(decay * c_pair + upd)
            new_n.append(decay * n_pair + jnp.sum(wk * kpf, axis=0, keepdims=True))
            m_ref[:, h:h + 1] = m_new
        c_ref[pr] = jnp.where(rsub < dk, new_c[0], new_c[1])
        n_ref[pr] = jnp.where(lane1 < dk, new_n[0], new_n[1])


def mlstm_chunked(q, k, v, i_log, f_log, C0, n0, m0, chunk):
    B, T, _ = q.shape
    H = i_log.shape[-1]
    dk = C0.shape[2]
    dv = C0.shape[3]
    assert 2 * dk == LANES and dv == LANES
    nc = T // chunk
    L = chunk
    Tp = -(-T // LANES) * LANES
    fT = jnp.pad(f_log.transpose(0, 2, 1), ((0, 0), (0, 0), (0, Tp - T)))
    bT = seg_cumsum(fT, chunk)[:, :, :T]
    br = bT.reshape(B, H, nc, L).transpose(0, 2, 1, 3)
    bc = br.transpose(0, 1, 3, 2)
    ic = i_log.reshape(B, nc, L, H)
    ir = ic.transpose(0, 1, 3, 2)
    HP = H // 2
    c0 = C0.reshape(B, HP, 2 * dk, dv)
    n0p = n0.reshape(B, HP, 1, 2 * dk)
    m0p = m0.reshape(B, 1, H)
    QW = H * dk
    VW = H * dv
    col_spec = pl.BlockSpec((None, None, L, H), lambda b, c: (b, c, 0, 0))
    row_spec = pl.BlockSpec((None, None, H, L), lambda b, c: (b, c, 0, 0))
    st = lambda b, c: (b, 0, 0, 0)
    h, Cn, nn, mn = pl.pallas_call(
        functools.partial(_mlstm_kernel, n_heads=H, dk=dk),
        grid=(B, nc),
        in_specs=[pl.BlockSpec((None, L, QW), lambda b, c: (b, c, 0)),
                  pl.BlockSpec((None, L, QW), lambda b, c: (b, c, 0)),
                  pl.BlockSpec((None, L, VW), lambda b, c: (b, c, 0)),
                  col_spec, row_spec, col_spec, row_spec,
                  pl.BlockSpec((None, HP, 2 * dk, dv), st),
                  pl.BlockSpec((None, HP, 1, 2 * dk), st),
                  pl.BlockSpec((None, 1, H), lambda b, c: (b, 0, 0))],
        out_specs=[pl.BlockSpec((None, L, VW), lambda b, c: (b, c, 0)),
                   pl.BlockSpec((None, HP, 2 * dk, dv), st),
                   pl.BlockSpec((None, HP, 1, 2 * dk), st),
                   pl.BlockSpec((None, 1, H), lambda b, c: (b, 0, 0))],
        out_shape=[jax.ShapeDtypeStruct((B, T, VW), F32),
                   jax.ShapeDtypeStruct((B, HP, 2 * dk, dv), F32),
                   jax.ShapeDtypeStruct((B, HP, 1, 2 * dk), F32),
                   jax.ShapeDtypeStruct((B, 1, H), F32)],
        compiler_params=_params("arbitrary", "arbitrary"),
        name="mlstm_chunked",
    )(q, k, v, ic, ir, bc, br, c0, n0p, m0p)
    return h, Cn.reshape(B, H, dk, dv), nn.reshape(B, H, dk), mn.reshape(B, H)


def _staircase(n):
    return [(a, n // (a + 1)) for a in range(n)]


def _extract_topk(s, payload, n):
    R = s.shape[0]
    ridx = lax.broadcasted_iota(I32, s.shape, 0)
    vals, pays = [], []
    for _ in range(n):
        m = jnp.max(s, axis=0, keepdims=True)
        am = jnp.min(jnp.where(s == m, ridx, R), axis=0, keepdims=True)
        hit = ridx == am
        if payload is None:
            pays.append(am)
        else:
            pays.append(jnp.max(jnp.where(hit, payload, -1), axis=0, keepdims=True))
        vals.append(m)
        s = jnp.where(hit, NEG_INF, s)
    return vals, pays


def _peer_topk_kernel(x_ref, sc_ref, sh_ref, gain_ref, wqh_ref, wql_ref, k1h_ref, k1l_ref, k2h_ref, k2l_ref,
                      hb_ref, e_ref, g_ref, q_scr, v1_scr, i1_scr, v2_scr, i2_scr, *, n_heads, n_keys):
    TM = x_ref.shape[0]
    K = PEER_TOPK
    half = k1h_ref.shape[2]
    h = _modnorm(x_ref[...], gain_ref[...], sc_ref[...], sh_ref[...])
    hh, hl = _split2(h)
    hb_ref[...] = hh
    q_scr[...] = _dot(hh, wqh_ref[...]) + _dot(hh, wql_ref[...]) + _dot(hl, wqh_ref[...])
    sub8 = lax.broadcasted_iota(I32, (8, LANES), 0)

    def head_body(hd, _):
        c0 = pl.multiple_of(hd * 2 * half, 2 * half)
        c1 = pl.multiple_of(hd * 2 * half + half, half)
        qa_h, qa_l = _split2(q_scr[:, pl.ds(c0, half)])
        qb_h, qb_l = _split2(q_scr[:, pl.ds(c1, half)])
        k1h = k1h_ref[hd]
        k1l = k1l_ref[hd]
        k2h = k2h_ref[hd]
        k2l = k2l_ref[hd]
        s1 = _dot_nt(k1h, qa_h) + _dot_nt(k1h, qa_l) + _dot_nt(k1l, qa_h)
        s2 = _dot_nt(k2h, qb_h) + _dot_nt(k2h, qb_l) + _dot_nt(k2l, qb_h)
        for blk in range(TM // LANES):
            ls = slice(blk * LANES, (blk + 1) * LANES)
            for s, v_scr, i_scr in ((s1, v1_scr, i1_scr), (s2, v2_scr, i2_scr)):
                vals, idxs = _extract_topk(s[:, ls], None, K)
                for r in range(K):
                    v_scr[r:r + 1, :] = vals[r]
                    i_scr[r:r + 1, :] = idxs[r]
            cand, pay = [], []
            for a, cnt in _staircase(K):
                if cnt == 1:
                    break
                for b0 in range(0, cnt, 8):
                    nb = min(8, cnt - b0)
                    cv = v1_scr[a:a + 1, :] + v2_scr[b0:b0 + 8, :]
                    pv = i1_scr[a:a + 1, :] * n_keys + i2_scr[b0:b0 + 8, :]
                    if nb < 8:
                        cv = jnp.where(sub8 < nb, cv, NEG_INF)
                    cand.append(cv)
                    pay.append(pv)
            a1 = K // 2
            cand.append(v1_scr[a1:K, :] + v2_scr[0:1, :])
            pay.append(i1_scr[a1:K, :] * n_keys + i2_scr[0:1, :])
            top_s, top_e = _extract_topk(jnp.concatenate(cand, axis=0), jnp.concatenate(pay, axis=0), K)
            ex = [jnp.exp(t - top_s[0]) for t in top_s]
            den = ex[0]
            for t in ex[1:]:
                den = den + t
            inv = 1.0 / den
            for r in range(K):
                i1_scr[r:r + 1, :] = top_e[r]
                v1_scr[r:r + 1, :] = ex[r] * inv
            out_rows = pl.ds(pl.multiple_of(hd * K, K), K)
            e_ref[out_rows, ls] = i1_scr[...]
            g_ref[out_rows, ls] = v1_scr[...]
        return 0

    lax.fori_loop(0, n_heads, head_body, 0)


def peer_retrieve(x, scale, shift, gain, w_q, sub_k1, sub_k2, *, tm, tiles_per_group):
    N, D = x.shape
    HP, NK, half = sub_k1.shape
    assert NK == LANES and half == LANES and PEER_TOPK % 8 == 0
    R = scale.shape[1]
    QW = w_q.shape[1]
    J = HP * PEER_TOPK
    wqh = w_q.astype(BF16)
    wql = (w_q - wqh.astype(F32)).astype(BF16)
    k1h = sub_k1.astype(BF16)
    k1l = (sub_k1 - k1h.astype(F32)).astype(BF16)
    k2h = sub_k2.astype(BF16)
    k2l = (sub_k2 - k2h.astype(F32)).astype(BF16)
    const = lambda i: (0, 0)
    const3 = lambda i: (0, 0, 0)
    kspec = pl.BlockSpec((HP, NK, half), const3)
    return pl.pallas_call(
        functools.partial(_peer_topk_kernel, n_heads=HP, n_keys=NK),
        grid=(N // tm,),
        in_specs=[pl.BlockSpec((tm, D), lambda i: (i, 0)),
                  _mod_spec(R, D, tiles_per_group), _mod_spec(R, D, tiles_per_group),
                  pl.BlockSpec((1, D), const),
                  pl.BlockSpec((D, QW), const), pl.BlockSpec((D, QW), const),
                  kspec, kspec, kspec, kspec],
        out_specs=[pl.BlockSpec((tm, D), lambda i: (i, 0)),
                   pl.BlockSpec((J, tm), lambda i: (0, i)),
                   pl.BlockSpec((J, tm), lambda i: (0, i))],
        out_shape=[jax.ShapeDtypeStruct((N, D), BF16),
                   jax.ShapeDtypeStruct((J, N), I32),
                   jax.ShapeDtypeStruct((J, N), F32)],
        scratch_shapes=[pltpu.VMEM((tm, QW), F32),
                        pltpu.VMEM((PEER_TOPK, LANES), F32), pltpu.VMEM((PEER_TOPK, LANES), I32),
                        pltpu.VMEM((PEER_TOPK, LANES), F32), pltpu.VMEM((PEER_TOPK, LANES), I32)],
        compiler_params=_params("arbitrary"),
        name="peer_retrieve",
    )(x, scale, shift, gain.reshape(1, D), wqh, wql, k1h, k1l, k2h, k2l)


def _peer_mix_kernel(hb_ref, i1_ref, i2_ref, g_ref, ut_ref, v_ref, x_ref, g2_ref, o_ref,
                     a_scr, w_scr, hbuf_scr, y_scr, *, n_keys, tg):
    TM = hb_ref.shape[0]
    TE = ut_ref.shape[1]
    J = i1_ref.shape[1]
    NE = pl.num_programs(1) // 2
    j = pl.program_id(1)
    NP2 = n_keys // 2
    W2 = 2 * n_keys

    @pl.when(j == 0)
    def _():
        a_scr[...] = jnp.zeros(a_scr.shape, F32)

    @pl.when(j < NE)
    def _():
        act = _dot(hb_ref[...], ut_ref[...])
        i1 = i1_ref[...]
        i2 = i2_ref[...]
        acc = a_scr[...]
        for c in range(TE // n_keys):
            r = j * (TE // n_keys) + c
            picked = jnp.take_along_axis(act[:, c * n_keys:(c + 1) * n_keys], i2, axis=1)
            acc = jnp.where(i1 == r, picked, acc)
        a_scr[...] = acc

    @pl.when(j == NE)
    def _():
        a = a_scr[...]
        gelu = 0.5 * a * (1.0 + lax.erf(a * math.sqrt(0.5)))
        w_scr[...] = g_ref[...] * gelu
        y_scr[...] = jnp.zeros(y_scr.shape, F32)
        rows = lax.broadcasted_iota(I32, (NP2, J), 0)
        cols = lax.broadcasted_iota(I32, (W2, J), 0)

        def group_body(gi, _):
            t0 = pl.multiple_of(gi * tg, tg)
            grids = []
            for u in range(tg):
                i1 = i1_ref[pl.ds(t0 + u, 1), :]
                i2 = i2_ref[pl.ds(t0 + u, 1), :]
                w = w_scr[pl.ds(t0 + u, 1), :]
                p1 = jnp.where(rows == (i1 >> 1), w, 0.0).astype(BF16)
                p2 = jnp.where(cols == ((i1 & 1) * n_keys + i2), 1.0, 0.0).astype(BF16)
                grids.append(_dot_nt(p1, p2))
            hbuf_scr[:, pl.ds(t0, tg), :] = pltpu.einshape("tpc->ptc", jnp.stack(grids, axis=0))
            return 0

        lax.fori_loop(0, TM // tg, group_body, 0)

    @pl.when(j >= NE)
    def _():
        jj = j - NE
        y = y_scr[...]
        for c in range(TE // W2):
            lhs = hbuf_scr[jj * (TE // W2) + c]
            y = y + _dot(lhs.astype(BF16), v_ref[c * W2:(c + 1) * W2, :])
        y_scr[...] = y

    @pl.when(j == 2 * NE - 1)
    def _():
        o_ref[...] = x_ref[...] + g2_ref[...] * y_scr[...]


def peer_mix(hb, i1, i2, g, ut_tab, v_tab, x, g2, *, tm, te, tiles_per_group, n_keys):
    N, D = x.shape
    E = v_tab.shape[0]
    J = i1.shape[1]
    NE = E // te
    R = g2.shape[1]
    tg = 8
    assert te % (2 * n_keys) == 0 and n_keys == LANES and tm % tg == 0
    return pl.pallas_call(
        functools.partial(_peer_mix_kernel, n_keys=n_keys, tg=tg),
        grid=(N // tm, 2 * NE),
        in_specs=[pl.BlockSpec((tm, D), lambda i, j: (i, 0)),
                  pl.BlockSpec((tm, J), lambda i, j: (i, 0)),
                  pl.BlockSpec((tm, J), lambda i, j: (i, 0)),
                  pl.BlockSpec((tm, J), lambda i, j: (i, 0)),
                  pl.BlockSpec((D, te), lambda i, j: (0, jnp.minimum(j, NE - 1))),
                  pl.BlockSpec((te, D), lambda i, j: (jnp.maximum(j - NE, 0), 0)),
                  pl.BlockSpec((tm, D), lambda i, j: (i, 0)),
                  pl.BlockSpec((None, R, D), lambda i, j: (i // tiles_per_group, 0, 0))],
        out_specs=pl.BlockSpec((tm, D), lambda i, j: (i, 0)),
        out_shape=jax.ShapeDtypeStruct((N, D), F32),
        scratch_shapes=[pltpu.VMEM((tm, J), F32), pltpu.VMEM((tm, J), F32),
                        pltpu.VMEM((n_keys // 2, tm, 2 * n_keys), F32),
                        pltpu.VMEM((tm, D), F32)],
        compiler_params=_params("arbitrary", "arbitrary"),
        name="peer_mix",
    )(hb, i1, i2, g, ut_tab, v_tab, x, g2)


def _trunk(x3, mods, past, params, tables, *, tm):
    (norm_mix, norm_ffn, fox_w_in, fox_b_f, fox_q_norm, fox_k_norm, fox_w_out,
     ml_w_in, ml_b_i, ml_b_f, ml_h_norm, ml_w_out, peer_w_q, peer_sub_k1, peer_sub_k2) = params
    ut_bf, v_bf = tables
    B, T, D = x3.shape
    N = B * T
    depth = norm_mix.shape[0]
    n_fox_heads = fox_b_f.shape[1]
    n_keys = peer_sub_k1.shape[2]
    x = x3.reshape(N, D)
    if T % tm == 0:
        tpg, rows = T // tm, 1
        expand = lambda a: a.reshape(B, 1, D)
    else:
        assert N == tm
        tpg, rows = 1, N
        expand = lambda a: jnp.repeat(a, T, axis=0).reshape(1, N, D)
    new_k, new_v, new_lf, new_C, new_n, new_m = [], [], [], [], [], []
    for layer in range(depth):
        sh1, sc1, g1, sh2, sc2, g2 = [expand(a) for a in jnp.split(mods[layer], 6, axis=-1)]
        j = layer // 2
        if layer % 2 == 0:
            q, k, v, kb, vb, sg, lf = fox_project(
                x, sc1, sh1, norm_mix[layer], fox_w_in[j], fox_b_f[j], fox_q_norm[j], fox_k_norm[j],
                tm=tm, tiles_per_group=tpg)
            if past is None:
                F = seg_cumsum(lf.reshape(B, T, n_fox_heads).transpose(0, 2, 1), T).transpose(0, 2, 1)
                o = fox_attend_prompt(q.reshape(B, T, D), kb.reshape(B, T, D), vb.reshape(B, T, D), F,
                                      n_heads=n_fox_heads)
            else:
                cache_kt, cache_vt, cache_cf, _, _, _, page_table = past
                o = fox_attend_sample(q.reshape(B, T, D), kb.reshape(B, T, D), vb.reshape(B, T, D),
                                      lf.reshape(B, T, n_fox_heads), cache_kt, cache_vt, cache_cf,
                                      page_table, j, n_heads=n_fox_heads)
            x = out_project(o.reshape(N, D), sg, fox_w_out[j], x, g1, jnp.ones((D,), F32),
                            tm=tm, tiles_per_group=tpg, head_norm=False)
            new_k.append(k.reshape(B, T, n_fox_heads, -1))
            new_v.append(v.reshape(B, T, n_fox_heads, -1))
            new_lf.append(lf.reshape(B, T, n_fox_heads))
        else:
            n_ml_heads = ml_b_i.shape[1]
            v_width = ml_w_out.shape[1]
            q, k, v, so, il, fl = ml_project(x, sc1, sh1, norm_mix[layer], ml_w_in[j], ml_b_i[j], ml_b_f[j],
                                             tm=tm, tiles_per_group=tpg, v_width=v_width)
            QW = q.shape[1]
            dk = QW // n_ml_heads
            dv = v_width // n_ml_heads
            q3, k3, v3 = q.reshape(B, T, QW), k.reshape(B, T, QW), v.reshape(B, T, v_width)
            il3, fl3 = il.reshape(B, T, n_ml_heads), fl.reshape(B, T, n_ml_heads)
            if past is None:
                C0 = jnp.zeros((B, n_ml_heads, dk, dv), F32)
                n0 = jnp.zeros((B, n_ml_heads, dk), F32)
                m0 = jnp.zeros((B, n_ml_heads), F32)
                hh, C, n_, m_ = mlstm_chunked(q3, k3, v3, il3, fl3, C0, n0, m0, 64)
            else:
                _, _, _, state_C, state_n, state_m, _ = past
                Tp = 16
                pad3 = ((0, 0), (0, Tp - T), (0, 0))
                hh, C, n_, m_ = mlstm_chunked(
                    jnp.pad(q3, pad3), jnp.pad(k3, pad3), jnp.pad(v3, pad3),
                    jnp.pad(il3, pad3, constant_values=-1e30), jnp.pad(fl3, pad3),
                    state_C[j], state_n[j], state_m[j], Tp)
                hh = hh[:, :T]
            x = out_project(hh.reshape(N, v_width), so, ml_w_out[j], x, g1, ml_h_norm[j].reshape(-1),
                            tm=tm, tiles_per_group=tpg, head_norm=True)
            new_C.append(C)
            new_n.append(n_)
            new_m.append(m_)
        hb, e, g = peer_retrieve(x, sc2, sh2, norm_ffn[layer], peer_w_q[layer], peer_sub_k1[layer],
                                 peer_sub_k2[layer], tm=tm, tiles_per_group=tpg)
        eT = e.T
        x = peer_mix(hb, eT // n_keys, eT % n_keys, g.T, ut_bf[layer], v_bf[layer], x, g2,
                     tm=tm, te=2048, tiles_per_group=tpg, n_keys=n_keys)
    return (x.reshape(B, T, D), jnp.stack(new_k), jnp.stack(new_v), jnp.stack(new_lf),
            jnp.stack(new_C), jnp.stack(new_n), jnp.stack(new_m))


def kernel(x_prompt, x_sample, cache_k, cache_v, cache_logf, state_C, state_n, state_m, page_table,
           c_prompt, c_sample, ada_w, ada_b, norm_mix, norm_ffn, fox_w_in, fox_b_f, fox_q_norm, fox_k_norm,
           fox_w_out, ml_w_in, ml_b_i, ml_b_f, ml_h_norm, ml_w_out, peer_w_q, peer_sub_k1, peer_sub_k2,
           peer_u, peer_v):
    params = (norm_mix, norm_ffn, fox_w_in, fox_b_f, fox_q_norm, fox_k_norm, fox_w_out,
              ml_w_in, ml_b_i, ml_b_f, ml_h_norm, ml_w_out, peer_w_q, peer_sub_k1, peer_sub_k2)
    tables = (peer_u.astype(BF16).transpose(0, 2, 1), peer_v.astype(BF16))
    Bp = c_prompt.shape[0]
    mods = adaln(jnp.concatenate([c_prompt, c_sample], axis=0), ada_w, ada_b)
    y_p, k_p, v_p, lf_p, C_p, n_p, m_p = _trunk(x_prompt, mods[:, :Bp], None, params, tables, tm=256)
    past = prepare_cache(cache_k, cache_v, cache_logf) + (state_C, state_n, state_m, page_table)
    y_s, k_s, v_s, lf_s, C_s, n_s, m_s = _trunk(x_sample, mods[:, Bp:], past, params, tables,
                                               tm=x_sample.shape[0] * x_sample.shape[1])
    return (y_p, y_s, k_p, v_p, lf_p, C_p, n_p, m_p, k_s, v_s, lf_s, C_s, n_s, m_s)
```

```python
import functools
import math

import jax
import jax.numpy as jnp
from jax import lax
from jax.experimental import pallas as pl
from jax.experimental.pallas import tpu as pltpu

F32 = jnp.float32
BF16 = jnp.bfloat16
I32 = jnp.int32

RMS_EPS = 1e-6
LANES = 128
MXU_DIM = 256
VMEM_LIMIT = 52 * 1024 * 1024
PEER_TOPK = 16
PAGE_SIZE = 128
NEG_INF = float("-inf")


def _dot(a, b):
    return jnp.dot(a, b, preferred_element_type=F32)


def _dot_nt(a, b):
    return lax.dot_general(a, b, (((1,), (1,)), ((), ())), preferred_element_type=F32)


def _dot_tn(a, b):
    return lax.dot_general(a, b, (((0,), (0,)), ((), ())), preferred_element_type=F32)


def _split2(a):
    hi = a.astype(BF16)
    lo = (a - hi.astype(F32)).astype(BF16)
    return hi, lo


def _dot3(a, b):
    ah, al = _split2(a)
    bh, bl = _split2(b)
    return _dot(ah, bh) + _dot(ah, bl) + _dot(al, bh)


def _modnorm(x, gain, scale, shift):
    ms = jnp.mean(x * x, axis=-1, keepdims=True)
    return x * lax.rsqrt(ms + RMS_EPS) * gain * (1.0 + scale) + shift


def _log_sigmoid(z):
    return jnp.minimum(z, 0.0) - jnp.log1p(jnp.exp(-jnp.abs(z)))


def _params(*sem):
    return pltpu.CompilerParams(dimension_semantics=sem, vmem_limit_bytes=VMEM_LIMIT)


def _mod_spec(rows, d, tiles_per_group):
    return pl.BlockSpec((None, rows, d), lambda i: (i // tiles_per_group, 0, 0))


def _adaln_kernel(c_ref, w_ref, b_ref, o_ref):
    c = c_ref[...]
    o_ref[...] = _dot3(c * jax.nn.sigmoid(c), w_ref[...]) + b_ref[...]


def adaln(c, ada_w, ada_b, tn=1536):
    L, D, D6 = ada_w.shape
    R = c.shape[0]
    return pl.pallas_call(
        _adaln_kernel,
        grid=(L, D6 // tn),
        in_specs=[pl.BlockSpec((R, D), lambda l, j: (0, 0)),
                  pl.BlockSpec((None, D, tn), lambda l, j: (l, 0, j)),
                  pl.BlockSpec((None, 1, tn), lambda l, j: (l, 0, j))],
        out_specs=pl.BlockSpec((None, R, tn), lambda l, j: (l, 0, j)),
        out_shape=jax.ShapeDtypeStruct((L, R, D6), F32),
        compiler_params=_params("arbitrary", "arbitrary"),
        name="adaln",
    )(c, ada_w, ada_b.reshape(L, 1, D6))


def _segcumsum_kernel(x_ref, o_ref, *, seg):
    C, T = x_ref.shape
    lane = lax.broadcasted_iota(I32, (C, LANES), 1)
    w = min(seg, LANES)
    carry = None
    for blk in range(T // LANES):
        x = x_ref[:, blk * LANES:(blk + 1) * LANES]
        k = 1
        while k < w:
            x = x + jnp.where((lane & (w - 1)) >= k, pltpu.roll(x, k, 1), 0.0)
            k *= 2
        if seg > LANES:
            if blk % (seg // LANES) != 0:
                x = x + carry
            carry = x[:, LANES - 1:LANES]
        o_ref[:, blk * LANES:(blk + 1) * LANES] = x


def seg_cumsum(x, seg):
    R, C, T = x.shape
    assert T % LANES == 0 and (seg & (seg - 1)) == 0
    assert T % seg == 0 and (seg <= LANES or seg % LANES == 0)
    return pl.pallas_call(
        functools.partial(_segcumsum_kernel, seg=seg),
        grid=(R,),
        in_specs=[pl.BlockSpec((None, C, T), lambda r: (r, 0, 0))],
        out_specs=pl.BlockSpec((None, C, T), lambda r: (r, 0, 0)),
        out_shape=jax.ShapeDtypeStruct((R, C, T), F32),
        compiler_params=_params("arbitrary"),
        name="seg_cumsum",
    )(x)


def _pair_headnorm(blk, gain2, hd):
    lane = lax.broadcasted_iota(I32, blk.shape, 1)
    lo = lane < hd
    sq = blk * blk
    s_lo = jnp.sum(jnp.where(lo, sq, 0.0), axis=-1, keepdims=True)
    s_hi = jnp.sum(jnp.where(lo, 0.0, sq), axis=-1, keepdims=True)
    inv = lax.rsqrt(jnp.where(lo, s_lo, s_hi) * (1.0 / hd) + RMS_EPS)
    return blk * inv * gain2


def _fox_proj_kernel(x_ref, sc_ref, sh_ref, gain_ref, w_ref, wfh_ref, wfl_ref, bf_ref, qg_ref, kg_ref,
                     q_ref, k_ref, v_ref, kb_ref, vb_ref, sg_ref, lf_ref, *, n_heads):
    D = x_ref.shape[1]
    hd = D // n_heads
    h = _modnorm(x_ref[...], gain_ref[...], sc_ref[...], sh_ref[...])
    hh, hl = _split2(h)
    qk_scale = hd ** -0.5
    q = _dot(hh, w_ref[:, 0:D])
    for c in range(D // LANES):
        sl = slice(c * LANES, (c + 1) * LANES)
        q_ref[:, sl] = (_pair_headnorm(q[:, sl], qg_ref[...], hd) * qk_scale).astype(BF16)
    k = _dot(hh, w_ref[:, D:2 * D])
    for c in range(D // LANES):
        sl = slice(c * LANES, (c + 1) * LANES)
        kn = _pair_headnorm(k[:, sl], kg_ref[...], hd)
        k_ref[:, sl] = kn
        kb_ref[:, sl] = kn.astype(BF16)
    v = _dot(hh, w_ref[:, 2 * D:3 * D])
    v_ref[...] = v
    vb_ref[...] = v.astype(BF16)
    sg_ref[...] = jax.nn.sigmoid(_dot(hh, w_ref[:, 3 * D:4 * D]))
    f = _dot(hh, wfh_ref[...]) + _dot(hh, wfl_ref[...]) + _dot(hl, wfh_ref[...])
    lf_ref[...] = _log_sigmoid(f[:, 0:n_heads] + bf_ref[...])


def fox_project(x, scale, shift, gain, w_in, b_f, q_gain, k_gain, *, tm, tiles_per_group):
    N, D = x.shape
    H = b_f.shape[0]
    hd = D // H
    assert 2 * hd == LANES
    R = scale.shape[1]
    w_main = jnp.concatenate([w_in[:, :3 * D], w_in[:, 3 * D + H:]], axis=1).astype(BF16)
    w_f = jnp.pad(w_in[:, 3 * D:3 * D + H], ((0, 0), (0, LANES - H)))
    wfh = w_f.astype(BF16)
    wfl = (w_f - wfh.astype(F32)).astype(BF16)
    const = lambda i: (0, 0)
    row = lambda i: (i, 0)
    outs = pl.pallas_call(
        functools.partial(_fox_proj_kernel, n_heads=H),
        grid=(N // tm,),
        in_specs=[pl.BlockSpec((tm, D), row),
                  _mod_spec(R, D, tiles_per_group), _mod_spec(R, D, tiles_per_group),
                  pl.BlockSpec((1, D), const),
                  pl.BlockSpec((D, 4 * D), const),
                  pl.BlockSpec((D, LANES), const), pl.BlockSpec((D, LANES), const),
                  pl.BlockSpec((1, H), const),
                  pl.BlockSpec((1, LANES), const), pl.BlockSpec((1, LANES), const)],
        out_specs=[pl.BlockSpec((tm, D), row)] * 6 + [pl.BlockSpec((tm, H), row)],
        out_shape=[jax.ShapeDtypeStruct((N, D), BF16),
                   jax.ShapeDtypeStruct((N, D), F32), jax.ShapeDtypeStruct((N, D), F32),
                   jax.ShapeDtypeStruct((N, D), BF16), jax.ShapeDtypeStruct((N, D), BF16),
                   jax.ShapeDtypeStruct((N, D), F32),
                   jax.ShapeDtypeStruct((N, H), F32)],
        compiler_params=_params("arbitrary"),
        name="fox_project",
    )(x, scale, shift, gain.reshape(1, D), w_main, wfh, wfl, b_f.reshape(1, H),
      jnp.tile(q_gain, 2).reshape(1, LANES), jnp.tile(k_gain, 2).reshape(1, LANES))
    return outs


def _fox_attn_kernel(q_ref, k_ref, v_ref, fc_ref, fr_ref, o_ref, *, hd, n_sub):
    TQ = q_ref.shape[0]
    SQ = TQ // n_sub
    qi = pl.program_id(2)
    lane = lax.broadcasted_iota(I32, (SQ, LANES), 1)
    row = lax.broadcasted_iota(I32, (SQ, TQ), 0)
    col = lax.broadcasted_iota(I32, (SQ, TQ), 1)
    chains = [(hh, sb) for hh in range(2) for sb in range(n_sub)]
    qms, fqs = [], []
    for hh, sb in chains:
        q = q_ref[sb * SQ:(sb + 1) * SQ, :]
        qms.append(jnp.where((lane >= hh * hd) & (lane < (hh + 1) * hd), q, jnp.zeros_like(q)))
        fqs.append(fc_ref[sb * SQ:(sb + 1) * SQ, hh:hh + 1])

    def block_scores(j):
        start = pl.multiple_of(j * TQ, TQ)
        kj = k_ref[pl.ds(start, TQ), :]
        return tuple(_dot_nt(qms[c], kj) + (fqs[c] - fr_ref[hh:hh + 1, pl.ds(start, TQ)])
                     for c, (hh, sb) in enumerate(chains))

    def update(j, state, scores, masked):
        vj = v_ref[pl.ds(pl.multiple_of(j * TQ, TQ), TQ), :]
        new = []
        for c, (hh, sb) in enumerate(chains):
            m, l, acc = state[c]
            s = scores[c]
            if masked:
                s = jnp.where(col <= row + sb * SQ, s, NEG_INF)
            m_new = jnp.maximum(m, jnp.max(s, axis=-1, keepdims=True))
            alpha = jnp.exp(m - m_new)
            p = jnp.exp(s - m_new)
            l = alpha * l + jnp.sum(p, axis=-1, keepdims=True)
            acc = alpha * acc + _dot(p.astype(BF16), vj)
            new.append((m_new, l, acc))
        return tuple(new)

    def step(j, carry):
        state, scores = carry
        nxt = block_scores(j + 1)
        return update(j, state, scores, False), nxt

    init = tuple((jnp.full((SQ, 1), NEG_INF, F32), jnp.zeros((SQ, 1), F32), jnp.zeros((SQ, LANES), F32))
                 for _ in chains)
    state, scores = lax.fori_loop(0, qi, step, (init, block_scores(0)))
    final = update(qi, state, scores, True)
    for sb in range(n_sub):
        o0 = final[sb][2] / final[sb][1]
        o1 = final[n_sub + sb][2] / final[n_sub + sb][1]
        o_ref[sb * SQ:(sb + 1) * SQ, :] = jnp.where(lane < hd, o0, o1)


def fox_attend_prompt(q, kb, vb, F, *, n_heads, tq=256):
    B, S, D = q.shape
    hd = D // n_heads
    HP = n_heads // 2
    fcol = F.reshape(B, S, HP, 2).transpose(0, 2, 1, 3)
    frow = fcol.transpose(0, 1, 3, 2)
    n_sub = 2
    return pl.pallas_call(
        functools.partial(_fox_attn_kernel, hd=hd, n_sub=n_sub),
        grid=(B, HP, S // tq),
        in_specs=[pl.BlockSpec((None, tq, LANES), lambda b, h, i: (b, i, h)),
                  pl.BlockSpec((None, S, LANES), lambda b, h, i: (b, 0, h)),
                  pl.BlockSpec((None, S, LANES), lambda b, h, i: (b, 0, h)),
                  pl.BlockSpec((None, None, tq, 2), lambda b, h, i: (b, h, i, 0)),
                  pl.BlockSpec((None, None, 2, S), lambda b, h, i: (b, h, 0, 0))],
        out_specs=pl.BlockSpec((None, tq, LANES), lambda b, h, i: (b, i, h)),
        out_shape=jax.ShapeDtypeStruct((B, S, D), F32),
        compiler_params=_params("arbitrary", "arbitrary", "arbitrary"),
        name="fox_attend_prompt",
    )(q, kb, vb, fcol, frow)


def _fox_decode_kernel(pt_ref, qbd_ref, *refs, n_heads, pp):
    del pt_ref
    kt_refs, vt_refs, cf_refs = refs[:pp], refs[pp:2 * pp], refs[2 * pp:3 * pp]
    kn_ref, vn_ref, cn_ref, o_ref, m_scr, l_scr, acc_scr, carry_scr = refs[3 * pp:]
    H = n_heads
    step = pl.program_id(1)
    n_steps = pl.num_programs(1)
    R, D = qbd_ref.shape
    T = R // H
    P = cf_refs[0].shape[1]

    @pl.when(step == 0)
    def _():
        m_scr[...] = jnp.full(m_scr.shape, NEG_INF, F32)
        l_scr[...] = jnp.zeros(l_scr.shape, F32)
        acc_scr[...] = jnp.zeros(acc_scr.shape, F32)
        carry_scr[...] = jnp.zeros(carry_scr.shape, F32)

    qbd = qbd_ref[...]

    def merge(m_i, l_i, acc_i):
        m = m_scr[...]
        m_new = jnp.maximum(m, m_i)
        a_old = jnp.exp(m - m_new)
        a_new = jnp.exp(m_i - m_new)
        l_scr[...] = a_old * l_scr[...] + a_new * l_i
        acc_scr[...] = a_old * acc_scr[...] + a_new * acc_i
        m_scr[...] = m_new

    def scores(s, fk):
        return s - jnp.concatenate([fk] * T, axis=0)

    carry = carry_scr[...]
    ss = []
    for i in range(pp):
        cf = cf_refs[i][...]
        ss.append(scores(_dot(qbd, kt_refs[i][...].astype(BF16)), cf + carry))
        carry = carry + cf[:, P - 1:P]
    carry_scr[...] = carry
    m_s = jnp.max(ss[0], axis=-1, keepdims=True)
    for s in ss[1:]:
        m_s = jnp.maximum(m_s, jnp.max(s, axis=-1, keepdims=True))
    l_s = jnp.zeros_like(m_s)
    acc_s = jnp.zeros(acc_scr.shape, F32)
    for i, s in enumerate(ss):
        pr = jnp.exp(s - m_s)
        l_s = l_s + jnp.sum(pr, axis=-1, keepdims=True)
        acc_s = acc_s + _dot_nt(pr.astype(BF16), vt_refs[i][...].astype(BF16))
    merge(m_s, l_s, acc_s)

    @pl.when(step == n_steps - 1)
    def _():
        s2 = scores(_dot_nt(qbd, kn_ref[...]), cn_ref[...] + carry_scr[...])
        key = lax.broadcasted_iota(I32, s2.shape, 1)
        t = lax.broadcasted_iota(I32, s2.shape, 0) // H
        s2 = jnp.where(key <= t, s2, NEG_INF)
        m2 = jnp.max(s2, axis=-1, keepdims=True)
        p2 = jnp.exp(s2 - m2)
        merge(m2, jnp.sum(p2, axis=-1, keepdims=True), _dot(p2.astype(BF16), vn_ref[...]))
        out = acc_scr[...] / l_scr[...]
        rh = lax.broadcasted_iota(I32, (R, D), 0) % H
        ch = lax.broadcasted_iota(I32, (R, D), 1) // (D // H)
        out = jnp.where(rh == ch, out, 0.0)
        o_ref[...] = jnp.sum(out.reshape(T, H, D), axis=1)


def fox_attend_sample(q, kb_new, vb_new, lf_new, cache_kt, cache_vt, cache_cf, page_table, j, *, n_heads):
    Bd, T, D = q.shape
    H = n_heads
    hd = D // H
    n_pages = page_table.shape[1]
    n_fox, n_pool, _, P = cache_cf.shape
    R = T * H
    pp = 4 if n_pages % 4 == 0 else 1
    assert T <= P
    q4 = q.reshape(Bd, T, H, hd)
    eye = jnp.eye(H, dtype=q.dtype)
    qbd = (q4[:, :, :, None, :] * eye[None, None, :, :, None]).reshape(Bd, R, D)
    pad = ((0, 0), (0, P - T), (0, 0))
    kn = jnp.pad(kb_new, pad)
    vn = jnp.pad(vb_new, pad)
    cn = seg_cumsum(jnp.pad(lf_new.transpose(0, 2, 1), ((0, 0), (0, 0), (0, P - T))), P)
    pt = page_table.reshape(-1).astype(I32)

    def page_spec(shape, i):
        return pl.BlockSpec((None, None) + shape, lambda b, s, pt: (j, pt[b * n_pages + s * pp + i], 0, 0))

    per_seq = lambda shape: pl.BlockSpec((None,) + shape, lambda b, s, pt: (b, 0, 0))
    grid_spec = pltpu.PrefetchScalarGridSpec(
        num_scalar_prefetch=1,
        grid=(Bd, n_pages // pp),
        in_specs=([per_seq((R, D))]
                  + [page_spec((D, P), i) for i in range(pp)]
                  + [page_spec((D, P), i) for i in range(pp)]
                  + [page_spec((H, P), i) for i in range(pp)]
                  + [per_seq((P, D)), per_seq((P, D)), per_seq((H, P))]),
        out_specs=per_seq((T, D)),
        scratch_shapes=[pltpu.VMEM((R, 1), F32), pltpu.VMEM((R, 1), F32),
                        pltpu.VMEM((R, D), F32), pltpu.VMEM((H, 1), F32)],
    )
    return pl.pallas_call(
        functools.partial(_fox_decode_kernel, n_heads=H, pp=pp),
        grid_spec=grid_spec,
        out_shape=jax.ShapeDtypeStruct((Bd, T, D), F32),
        compiler_params=_params("arbitrary", "arbitrary"),
        name="fox_attend_sample",
    )(pt, qbd, *([cache_kt] * pp), *([cache_vt] * pp), *([cache_cf] * pp), kn, vn, cn)


def prepare_cache(cache_k, cache_v, cache_logf):
    n_fox, n_pool, P, H, hd = cache_k.shape
    kt = cache_k.transpose(0, 1, 3, 4, 2).reshape(n_fox, n_pool, H * hd, P)
    vt = cache_v.transpose(0, 1, 3, 4, 2).reshape(n_fox, n_pool, H * hd, P)
    rows = n_fox * n_pool * H
    group = 2048 if rows % 2048 == 0 else H
    cf = seg_cumsum(cache_logf.transpose(0, 1, 3, 2).reshape(rows // group, group, P), P)
    return kt, vt, cf.reshape(n_fox, n_pool, H, P)


def _outproj_kernel(a_ref, gate_ref, w_ref, x_ref, g_ref, hn_ref, o_ref, *, head_norm):
    a = a_ref[...]
    D = a.shape[1]
    if head_norm:
        blocks = []
        for c in range(D // LANES):
            sl = slice(c * LANES, (c + 1) * LANES)
            blk = a[:, sl]
            ms = jnp.mean(blk * blk, axis=-1, keepdims=True)
            blocks.append(blk * lax.rsqrt(ms + RMS_EPS) * hn_ref[:, sl])
        a = jnp.concatenate(blocks, axis=-1)
    y = (a * gate_ref[...]).astype(BF16)
    o_ref[...] = x_ref[...] + g_ref[...] * _dot(y, w_ref[...])


def out_project(a, gate, w_out, x, g, head_gain, *, tm, tiles_per_group, head_norm):
    N, D = x.shape
    R = g.shape[1]
    row = lambda i: (i, 0)
    const = lambda i: (0, 0)
    return pl.pallas_call(
        functools.partial(_outproj_kernel, head_norm=head_norm),
        grid=(N // tm,),
        in_specs=[pl.BlockSpec((tm, D), row), pl.BlockSpec((tm, D), row),
                  pl.BlockSpec((D, D), const), pl.BlockSpec((tm, D), row),
                  _mod_spec(R, D, tiles_per_group), pl.BlockSpec((1, D), const)],
        out_specs=pl.BlockSpec((tm, D), row),
        out_shape=jax.ShapeDtypeStruct((N, D), F32),
        compiler_params=_params("arbitrary"),
        name="out_project",
    )(a, gate, w_out.astype(BF16), x, g, head_gain.reshape(1, D))


def _ml_proj_kernel(x_ref, sc_ref, sh_ref, gain_ref, w_ref, wgh_ref, wgl_ref, bi_ref, bf_ref,
                    q_ref, k_ref, v_ref, so_ref, il_ref, fl_ref, *, n_heads, qk_width):
    D = x_ref.shape[1]
    QW = qk_width
    h = _modnorm(x_ref[...], gain_ref[...], sc_ref[...], sh_ref[...])
    hh, hl = _split2(h)
    k_scale = (QW // n_heads) ** -0.5
    q_ref[...] = _dot(hh, w_ref[:, 0:QW]).astype(BF16)
    k_ref[...] = (_dot(hh, w_ref[:, QW:2 * QW]) * k_scale).astype(BF16)
    v_ref[...] = _dot(hh, w_ref[:, 2 * QW:2 * QW + D])
    so_ref[...] = jax.nn.sigmoid(_dot(hh, w_ref[:, 2 * QW + D:2 * QW + 2 * D]))
    gts = _dot(hh, wgh_ref[...]) + _dot(hh, wgl_ref[...]) + _dot(hl, wgh_ref[...])
    il_ref[...] = gts[:, 0:n_heads] + bi_ref[...]
    fl_ref[...] = _log_sigmoid(gts[:, n_heads:2 * n_heads] + bf_ref[...])


def ml_project(x, scale, shift, gain, w_in, b_i, b_f, *, tm, tiles_per_group, v_width):
    N, D = x.shape
    H = b_i.shape[0]
    QW = (w_in.shape[1] - 2 * v_width - 2 * H) // 2
    R = scale.shape[1]
    g0 = 2 * QW + v_width
    w_main = jnp.concatenate([w_in[:, :g0], w_in[:, g0 + 2 * H:]], axis=1).astype(BF16)
    w_g = jnp.pad(w_in[:, g0:g0 + 2 * H], ((0, 0), (0, LANES - 2 * H)))
    wgh = w_g.astype(BF16)
    wgl = (w_g - wgh.astype(F32)).astype(BF16)
    const = lambda i: (0, 0)
    row = lambda i: (i, 0)
    WM = w_main.shape[1]
    return pl.pallas_call(
        functools.partial(_ml_proj_kernel, n_heads=H, qk_width=QW),
        grid=(N // tm,),
        in_specs=[pl.BlockSpec((tm, D), row),
                  _mod_spec(R, D, tiles_per_group), _mod_spec(R, D, tiles_per_group),
                  pl.BlockSpec((1, D), const),
                  pl.BlockSpec((D, WM), const),
                  pl.BlockSpec((D, LANES), const), pl.BlockSpec((D, LANES), const),
                  pl.BlockSpec((1, H), const), pl.BlockSpec((1, H), const)],
        out_specs=[pl.BlockSpec((tm, QW), row), pl.BlockSpec((tm, QW), row),
                   pl.BlockSpec((tm, v_width), row), pl.BlockSpec((tm, v_width), row),
                   pl.BlockSpec((tm, H), row), pl.BlockSpec((tm, H), row)],
        out_shape=[jax.ShapeDtypeStruct((N, QW), BF16), jax.ShapeDtypeStruct((N, QW), BF16),
                   jax.ShapeDtypeStruct((N, v_width), F32), jax.ShapeDtypeStruct((N, v_width), F32),
                   jax.ShapeDtypeStruct((N, H), F32), jax.ShapeDtypeStruct((N, H), F32)],
        compiler_params=_params("arbitrary"),
        name="ml_project",
    )(x, scale, shift, gain.reshape(1, D), w_main, wgh, wgl, b_i.reshape(1, H), b_f.reshape(1, H))


def _mlstm_kernel(q_ref, k_ref, v_ref, ic_ref, ir_ref, bc_ref, br_ref, c0_ref, n0_ref, m0_ref,
                  h_ref, c_ref, n_ref, m_ref, *, n_heads, dk):
    L = q_ref.shape[0]
    ci = pl.program_id(1)

    @pl.when(ci == 0)
    def _():
        c_ref[...] = c0_ref[...]
        n_ref[...] = n0_ref[...]
        m_ref[...] = m0_ref[...]

    q = q_ref[...]
    k = k_ref[...]
    ic = ic_ref[...]
    ir = ir_ref[...]
    bc = bc_ref[...]
    br = br_ref[...]
    m_all = m_ref[...]
    tri = lax.broadcasted_iota(I32, (L, L), 1) <= lax.broadcasted_iota(I32, (L, L), 0)
    lane = lax.broadcasted_iota(I32, (L, LANES), 1)
    lane1 = lax.broadcasted_iota(I32, (1, LANES), 1)
    rsub = lax.broadcasted_iota(I32, (LANES, LANES), 0)
    dv = LANES
    for pr in range(n_heads // 2):
        sl = slice(pr * LANES, (pr + 1) * LANES)
        qp = q[:, sl]
        kp = k[:, sl]
        kpf = kp.astype(F32)
        qpf = qp.astype(F32)
        c_pair = c_ref[pr]
        n_pair = n_ref[pr]
        cb = c_pair.astype(BF16)
        new_c, new_n = [], []
        for hh in range(2):
            h = 2 * pr + hh
            sel = (lane >= hh * dk) & (lane < (hh + 1) * dk)
            qm = jnp.where(sel, qp, jnp.zeros_like(qp))
            b_l = bc[:, h:h + 1]
            b_s = br[h:h + 1, :]
            ig_s = ir[h:h + 1, :]
            ig_l = ic[:, h:h + 1]
            m = m_all[:, h:h + 1]
            dm = jnp.where(tri, b_l - b_s + ig_s, NEG_INF)
            inter = b_l + m
            m_t = jnp.maximum(inter, jnp.max(dm, axis=-1, keepdims=True))
            w_inter = jnp.exp(inter - m_t)
            a = jnp.exp(dm - m_t) * _dot_nt(qm, kp)
            vh = v_ref[:, h * dv:(h + 1) * dv]
            num = w_inter * _dot(qm, cb) + _dot(a.astype(BF16), vh.astype(BF16))
            qn = jnp.sum(jnp.where(sel, qpf, 0.0) * n_pair, axis=-1, keepdims=True)
            den = w_inter * qn + jnp.sum(a, axis=-1, keepdims=True)
            h_ref[:, h * dv:(h + 1) * dv] = num / jnp.maximum(jnp.abs(den), jnp.exp(-m_t))
            b_last = b_l[L - 1:L, :]
            g_row = b_last - b_s + ig_s
            m_new = jnp.maximum(b_last + m, jnp.max(g_row, axis=-1, keepdims=True))
            wk = jnp.exp(b_last - b_l + ig_l - m_new)
            decay = jnp.exp(b_last + m - m_new)
            upd = _dot_tn(kp, (wk * vh).astype(BF16))
            new_c.append(decay * c_pair + upd)
            new_n.append(decay * n_pair + jnp.sum(wk * kpf, axis=0, keepdims=True))
            m_ref[:, h:h + 1] = m_new
        c_ref[pr] = jnp.where(rsub < dk, new_c[0], new_c[1])
        n_ref[pr] = jnp.where(lane1 < dk, new_n[0], new_n[1])


def mlstm_chunked(q, k, v, i_log, f_log, C0, n0, m0, chunk):
    B, T, _ = q.shape
    H = i_log.shape[-1]
    dk = C0.shape[2]
    dv = C0.shape[3]
    assert 2 * dk == LANES and dv == LANES
    nc = T // chunk
    L = chunk
    Tp = -(-T // LANES) * LANES
    fT = jnp.pad(f_log.transpose(0, 2, 1), ((0, 0), (0, 0), (0, Tp - T)))
    bT = seg_cumsum(fT, chunk)[:, :, :T]
    br = bT.reshape(B, H, nc, L).transpose(0, 2, 1, 3)
    bc = br.transpose(0, 1, 3, 2)
    ic = i_log.reshape(B, nc, L, H)
    ir = ic.transpose(0, 1, 3, 2)
    HP = H // 2
    c0 = C0.reshape(B, HP, 2 * dk, dv)
    n0p = n0.reshape(B, HP, 1, 2 * dk)
    m0p = m0.reshape(B, 1, H)
    QW = H * dk
    VW = H * dv
    col_spec = pl.BlockSpec((None, None, L, H), lambda b, c: (b, c, 0, 0))
    row_spec = pl.BlockSpec((None, None, H, L), lambda b, c: (b, c, 0, 0))
    st = lambda b, c: (b, 0, 0, 0)
    h, Cn, nn, mn = pl.pallas_call(
        functools.partial(_mlstm_kernel, n_heads=H, dk=dk),
        grid=(B, nc),
        in_specs=[pl.BlockSpec((None, L, QW), lambda b, c: (b, c, 0)),
                  pl.BlockSpec((None, L, QW), lambda b, c: (b, c, 0)),
                  pl.BlockSpec((None, L, VW), lambda b, c: (b, c, 0)),
                  col_spec, row_spec, col_spec, row_spec,
                  pl.BlockSpec((None, HP, 2 * dk, dv), st),
                  pl.BlockSpec((None, HP, 1, 2 * dk), st),
                  pl.BlockSpec((None, 1, H), lambda b, c: (b, 0, 0))],
        out_specs=[pl.BlockSpec((None, L, VW), lambda b, c: (b, c, 0)),
                   pl.BlockSpec((None, HP, 2 * dk, dv), st),
                   pl.BlockSpec((None, HP, 1, 2 * dk), st),
                   pl.BlockSpec((None, 1, H), lambda b, c: (b, 0, 0))],
        out_shape=[jax.ShapeDtypeStruct((B, T, VW), F32),
                   jax.ShapeDtypeStruct((B, HP, 2 * dk, dv), F32),
                   jax.ShapeDtypeStruct((B, HP, 1, 2 * dk), F32),
                   jax.ShapeDtypeStruct((B, 1, H), F32)],
        compiler_params=_params("arbitrary", "arbitrary"),
        name="mlstm_chunked",
    )(q, k, v, ic, ir, bc, br, c0, n0p, m0p)
    return h, Cn.reshape(B, H, dk, dv), nn.reshape(B, H, dk), mn.reshape(B, H)


def _staircase(n):
    return [(a, n // (a + 1)) for a in range(n)]


def _extract_topk(s, payload, n):
    R = s.shape[0]
    ridx = lax.broadcasted_iota(I32, s.shape, 0).astype(F32)
    vals, pays = [], []
    for it in range(n):
        m = jnp.max(s, axis=0, keepdims=True)
        am = jnp.min(jnp.where(s == m, ridx, float(R)), axis=0, keepdims=True)
        hit = ridx == am
        if payload is None:
            pays.append(am)
        else:
            pays.append(jnp.max(jnp.where(hit, payload, -1.0), axis=0, keepdims=True))
        vals.append(m)
        if it + 1 < n:
            s = jnp.where(hit, NEG_INF, s)
    return vals, pays


def _peer_topk_kernel(x_ref, sc_ref, sh_ref, gain_ref, wqh_ref, wql_ref, k1h_ref, k1l_ref, k2h_ref, k2l_ref,
                      hb_ref, e_ref, g_ref, q_scr, v1_scr, i1_scr, v2_scr, i2_scr, *, n_heads, n_keys):
    TM = x_ref.shape[0]
    K = PEER_TOPK
    half = k1h_ref.shape[2]
    h = _modnorm(x_ref[...], gain_ref[...], sc_ref[...], sh_ref[...])
    hh, hl = _split2(h)
    hb_ref[...] = hh
    q_scr[...] = _dot(hh, wqh_ref[...]) + _dot(hh, wql_ref[...]) + _dot(hl, wqh_ref[...])
    sub8 = lax.broadcasted_iota(I32, (8, LANES), 0)

    def head_body(hd, _):
        c0 = pl.multiple_of(hd * 2 * half, 2 * half)
        c1 = pl.multiple_of(hd * 2 * half + half, half)
        qa_h, qa_l = _split2(q_scr[:, pl.ds(c0, half)])
        qb_h, qb_l = _split2(q_scr[:, pl.ds(c1, half)])
        k1h = k1h_ref[hd]
        k1l = k1l_ref[hd]
        k2h = k2h_ref[hd]
        k2l = k2l_ref[hd]
        s1 = _dot_nt(k1h, qa_h) + _dot_nt(k1h, qa_l) + _dot_nt(k1l, qa_h)
        s2 = _dot_nt(k2h, qb_h) + _dot_nt(k2h, qb_l) + _dot_nt(k2l, qb_h)
        for blk in range(TM // LANES):
            ls = slice(blk * LANES, (blk + 1) * LANES)
            for s, v_scr, i_scr in ((s1, v1_scr, i1_scr), (s2, v2_scr, i2_scr)):
                vals, idxs = _extract_topk(s[:, ls], None, K)
                for r in range(K):
                    v_scr[r:r + 1, :] = vals[r]
                    i_scr[r:r + 1, :] = idxs[r]
            cand, pay = [], []
            for a, cnt in _staircase(K):
                if cnt == 1:
                    break
                for b0 in range(0, cnt, 8):
                    nb = min(8, cnt - b0)
                    cv = v1_scr[a:a + 1, :] + v2_scr[b0:b0 + 8, :]
                    pv = i1_scr[a:a + 1, :] * n_keys + i2_scr[b0:b0 + 8, :]
                    if nb < 8:
                        cv = jnp.where(sub8 < nb, cv, NEG_INF)
                    cand.append(cv)
                    pay.append(pv)
            a1 = K // 2
            cand.append(v1_scr[a1:K, :] + v2_scr[0:1, :])
            pay.append(i1_scr[a1:K, :] * n_keys + i2_scr[0:1, :])
            top_s, top_e = _extract_topk(jnp.concatenate(cand, axis=0), jnp.concatenate(pay, axis=0), K)
            ex = [jnp.exp(t - top_s[0]) for t in top_s]
            den = ex[0]
            for t in ex[1:]:
                den = den + t
            inv = 1.0 / den
            for r in range(K):
                i1_scr[r:r + 1, :] = top_e[r]
                v1_scr[r:r + 1, :] = ex[r] * inv
            out_rows = pl.ds(pl.multiple_of(hd * K, K), K)
            e_ref[out_rows, ls] = i1_scr[...].astype(I32)
            g_ref[out_rows, ls] = v1_scr[...]
        return 0

    lax.fori_loop(0, n_heads, head_body, 0)


def peer_retrieve(x, scale, shift, gain, w_q, sub_k1, sub_k2, *, tm, tiles_per_group):
    N, D = x.shape
    HP, NK, half = sub_k1.shape
    assert NK == LANES and half == LANES and PEER_TOPK % 8 == 0
    R = scale.shape[1]
    QW = w_q.shape[1]
    J = HP * PEER_TOPK
    wqh = w_q.astype(BF16)
    wql = (w_q - wqh.astype(F32)).astype(BF16)
    k1h = sub_k1.astype(BF16)
    k1l = (sub_k1 - k1h.astype(F32)).astype(BF16)
    k2h = sub_k2.astype(BF16)
    k2l = (sub_k2 - k2h.astype(F32)).astype(BF16)
    const = lambda i: (0, 0)
    const3 = lambda i: (0, 0, 0)
    kspec = pl.BlockSpec((HP, NK, half), const3)
    return pl.pallas_call(
        functools.partial(_peer_topk_kernel, n_heads=HP, n_keys=NK),
        grid=(N // tm,),
        in_specs=[pl.BlockSpec((tm, D), lambda i: (i, 0)),
                  _mod_spec(R, D, tiles_per_group), _mod_spec(R, D, tiles_per_group),
                  pl.BlockSpec((1, D), const),
                  pl.BlockSpec((D, QW), const), pl.BlockSpec((D, QW), const),
                  kspec, kspec, kspec, kspec],
        out_specs=[pl.BlockSpec((tm, D), lambda i: (i, 0)),
                   pl.BlockSpec((J, tm), lambda i: (0, i)),
                   pl.BlockSpec((J, tm), lambda i: (0, i))],
        out_shape=[jax.ShapeDtypeStruct((N, D), BF16),
                   jax.ShapeDtypeStruct((J, N), I32),
                   jax.ShapeDtypeStruct((J, N), F32)],
        scratch_shapes=[pltpu.VMEM((tm, QW), F32),
                        pltpu.VMEM((PEER_TOPK, LANES), F32), pltpu.VMEM((PEER_TOPK, LANES), F32),
                        pltpu.VMEM((PEER_TOPK, LANES), F32), pltpu.VMEM((PEER_TOPK, LANES), F32)],
        compiler_params=_params("arbitrary"),
        name="peer_retrieve",
    )(x, scale, shift, gain.reshape(1, D), wqh, wql, k1h, k1l, k2h, k2l)


def _peer_mix_kernel(hb_ref, i1_ref, i2_ref, g_ref, ut_ref, v_ref, x_ref, g2_ref, o_ref,
                     a_scr, w_scr, hbuf_scr, y_scr, *, n_keys, tg):
    TM = hb_ref.shape[0]
    TE = ut_ref.shape[1]
    J = i1_ref.shape[1]
    NE = pl.num_programs(1) // 2
    j = pl.program_id(1)
    NP2 = n_keys // 2
    W2 = 2 * n_keys

    @pl.when(j == 0)
    def _():
        a_scr[...] = jnp.zeros(a_scr.shape, F32)

    @pl.when(j < NE)
    def _():
        act = _dot(hb_ref[...], ut_ref[...])
        i1 = i1_ref[...]
        i2 = i2_ref[...]
        acc = a_scr[...]
        for c in range(TE // n_keys):
            r = j * (TE // n_keys) + c
            picked = jnp.take_along_axis(act[:, c * n_keys:(c + 1) * n_keys], i2, axis=1)
            acc = jnp.where(i1 == r, picked, acc)
        a_scr[...] = acc

    @pl.when(j == NE)
    def _():
        a = a_scr[...]
        gelu = 0.5 * a * (1.0 + lax.erf(a * math.sqrt(0.5)))
        w_scr[...] = g_ref[...] * gelu
        y_scr[...] = jnp.zeros(y_scr.shape, F32)
        rows = lax.broadcasted_iota(I32, (NP2, J), 0)
        cols = lax.broadcasted_iota(I32, (W2, J), 0)

        def group_body(gi, _):
            t0 = pl.multiple_of(gi * tg, tg)
            grids = []
            for u in range(tg):
                i1 = i1_ref[pl.ds(t0 + u, 1), :]
                i2 = i2_ref[pl.ds(t0 + u, 1), :]
                w = w_scr[pl.ds(t0 + u, 1), :]
                p1 = jnp.where(rows == (i1 >> 1), w, 0.0).astype(BF16)
                p2 = jnp.where(cols == ((i1 & 1) * n_keys + i2), 1.0, 0.0).astype(BF16)
                grids.append(_dot_nt(p1, p2))
            hbuf_scr[:, pl.ds(t0, tg), :] = pltpu.einshape("tpc->ptc", jnp.stack(grids, axis=0)).astype(BF16)
            return 0

        lax.fori_loop(0, TM // tg, group_body, 0)

    @pl.when(j >= NE)
    def _():
        jj = j - NE
        y = y_scr[...]
        for c in range(TE // W2):
            y = y + _dot(hbuf_scr[jj * (TE // W2) + c], v_ref[c * W2:(c + 1) * W2, :])
        y_scr[...] = y

    @pl.when(j == 2 * NE - 1)
    def _():
        o_ref[...] = x_ref[...] + g2_ref[...] * y_scr[...]


def peer_mix(hb, i1, i2, g, ut_tab, v_tab, x, g2, *, tm, te, tiles_per_group, n_keys):
    N, D = x.shape
    E = v_tab.shape[0]
    J = i1.shape[1]
    NE = E // te
    R = g2.shape[1]
    tg = 16
    assert te % (2 * n_keys) == 0 and n_keys == LANES and tm % tg == 0
    return pl.pallas_call(
        functools.partial(_peer_mix_kernel, n_keys=n_keys, tg=tg),
        grid=(N // tm, 2 * NE),
        in_specs=[pl.BlockSpec((tm, D), lambda i, j: (i, 0)),
                  pl.BlockSpec((tm, J), lambda i, j: (i, 0)),
                  pl.BlockSpec((tm, J), lambda i, j: (i, 0)),
                  pl.BlockSpec((tm, J), lambda i, j: (i, 0)),
                  pl.BlockSpec((D, te), lambda i, j: (0, jnp.minimum(j, NE - 1))),
                  pl.BlockSpec((te, D), lambda i, j: (jnp.maximum(j - NE, 0), 0)),
                  pl.BlockSpec((tm, D), lambda i, j: (i, 0)),
                  pl.BlockSpec((None, R, D), lambda i, j: (i // tiles_per_group, 0, 0))],
        out_specs=pl.BlockSpec((tm, D), lambda i, j: (i, 0)),
        out_shape=jax.ShapeDtypeStruct((N, D), F32),
        scratch_shapes=[pltpu.VMEM((tm, J), F32), pltpu.VMEM((tm, J), F32),
                        pltpu.VMEM((n_keys // 2, tm, 2 * n_keys), BF16),
                        pltpu.VMEM((tm, D), F32)],
        compiler_params=_params("arbitrary", "arbitrary"),
        name="peer_mix",
    )(hb, i1, i2, g, ut_tab, v_tab, x, g2)


def _trunk(x3, mods, past, params, tables, *, tm):
    (norm_mix, norm_ffn, fox_w_in, fox_b_f, fox_q_norm, fox_k_norm, fox_w_out,
     ml_w_in, ml_b_i, ml_b_f, ml_h_norm, ml_w_out, peer_w_q, peer_sub_k1, peer_sub_k2) = params
    ut_bf, v_bf = tables
    B, T, D = x3.shape
    N = B * T
    depth = norm_mix.shape[0]
    n_fox_heads = fox_b_f.shape[1]
    n_keys = peer_sub_k1.shape[2]
    x = x3.reshape(N, D)
    if T % tm == 0:
        tpg, rows = T // tm, 1
        expand = lambda a: a.reshape(B, 1, D)
    else:
        assert N == tm
        tpg, rows = 1, N
        expand = lambda a: jnp.repeat(a, T, axis=0).reshape(1, N, D)
    new_k, new_v, new_lf, new_C, new_n, new_m = [], [], [], [], [], []
    for layer in range(depth):
        sh1, sc1, g1, sh2, sc2, g2 = [expand(a) for a in jnp.split(mods[layer], 6, axis=-1)]
        j = layer // 2
        if layer % 2 == 0:
            q, k, v, kb, vb, sg, lf = fox_project(
                x, sc1, sh1, norm_mix[layer], fox_w_in[j], fox_b_f[j], fox_q_norm[j], fox_k_norm[j],
                tm=tm, tiles_per_group=tpg)
            if past is None:
                F = seg_cumsum(lf.reshape(B, T, n_fox_heads).transpose(0, 2, 1), T).transpose(0, 2, 1)
                o = fox_attend_prompt(q.reshape(B, T, D), kb.reshape(B, T, D), vb.reshape(B, T, D), F,
                                      n_heads=n_fox_heads)
            else:
                cache_kt, cache_vt, cache_cf, _, _, _, page_table = past
                o = fox_attend_sample(q.reshape(B, T, D), kb.reshape(B, T, D), vb.reshape(B, T, D),
                                      lf.reshape(B, T, n_fox_heads), cache_kt, cache_vt, cache_cf,
                                      page_table, j, n_heads=n_fox_heads)
            x = out_project(o.reshape(N, D), sg, fox_w_out[j], x, g1, jnp.ones((D,), F32),
                            tm=tm, tiles_per_group=tpg, head_norm=False)
            new_k.append(k.reshape(B, T, n_fox_heads, -1))
            new_v.append(v.reshape(B, T, n_fox_heads, -1))
            new_lf.append(lf.reshape(B, T, n_fox_heads))
        else:
            n_ml_heads = ml_b_i.shape[1]
            v_width = ml_w_out.shape[1]
            q, k, v, so, il, fl = ml_project(x, sc1, sh1, norm_mix[layer], ml_w_in[j], ml_b_i[j], ml_b_f[j],
                                             tm=tm, tiles_per_group=tpg, v_width=v_width)
            QW = q.shape[1]
            dk = QW // n_ml_heads
            dv = v_width // n_ml_heads
            q3, k3, v3 = q.reshape(B, T, QW), k.reshape(B, T, QW), v.reshape(B, T, v_width)
            il3, fl3 = il.reshape(B, T, n_ml_heads), fl.reshape(B, T, n_ml_heads)
            if past is None:
                C0 = jnp.zeros((B, n_ml_heads, dk, dv), F32)
                n0 = jnp.zeros((B, n_ml_heads, dk), F32)
                m0 = jnp.zeros((B, n_ml_heads), F32)
                hh, C, n_, m_ = mlstm_chunked(q3, k3, v3, il3, fl3, C0, n0, m0, 64)
            else:
                _, _, _, state_C, state_n, state_m, _ = past
                Tp = 16
                pad3 = ((0, 0), (0, Tp - T), (0, 0))
                hh, C, n_, m_ = mlstm_chunked(
                    jnp.pad(q3, pad3), jnp.pad(k3, pad3), jnp.pad(v3, pad3),
                    jnp.pad(il3, pad3, constant_values=-1e30), jnp.pad(fl3, pad3),
                    state_C[j], state_n[j], state_m[j], Tp)
                hh = hh[:, :T]
            x = out_project(hh.reshape(N, v_width), so, ml_w_out[j], x, g1, ml_h_norm[j].reshape(-1),
                            tm=tm, tiles_per_group=tpg, head_norm=True)
            new_C.append(C)
            new_n.append(n_)
            new_m.append(m_)
        hb, e, g = peer_retrieve(x, sc2, sh2, norm_ffn[layer], peer_w_q[layer], peer_sub_k1[layer],
                                 peer_sub_k2[layer], tm=tm, tiles_per_group=tpg)
        eT = e.T
        mix_tm = 2 * tm if tpg % 2 == 0 else tm
        x = peer_mix(hb, eT // n_keys, eT % n_keys, g.T, ut_bf[layer], v_bf[layer], x, g2,
                     tm=mix_tm, te=2048, tiles_per_group=tpg * tm // mix_tm, n_keys=n_keys)
    return (x.reshape(B, T, D), jnp.stack(new_k), jnp.stack(new_v), jnp.stack(new_lf),
            jnp.stack(new_C), jnp.stack(new_n), jnp.stack(new_m))


def kernel(x_prompt, x_sample, cache_k, cache_v, cache_logf, state_C, state_n, state_m, page_table,
           c_prompt, c_sample, ada_w, ada_b, norm_mix, norm_ffn, fox_w_in, fox_b_f, fox_q_norm, fox_k_norm,
           fox_w_out, ml_w_in, ml_b_i, ml_b_f, ml_h_norm, ml_w_out, peer_w_q, peer_sub_k1, peer_sub_k2,
           peer_u, peer_v):
    params = (norm_mix, norm_ffn, fox_w_in, fox_b_f, fox_q_norm, fox_k_norm, fox_w_out,
              ml_w_in, ml_b_i, ml_b_f, ml_h_norm, ml_w_out, peer_w_q, peer_sub_k1, peer_sub_k2)
    tables = (peer_u.astype(BF16).transpose(0, 2, 1), peer_v.astype(BF16))
    Bp = c_prompt.shape[0]
    mods = adaln(jnp.concatenate([c_prompt, c_sample], axis=0), ada_w, ada_b)
    y_p, k_p, v_p, lf_p, C_p, n_p, m_p = _trunk(x_prompt, mods[:, :Bp], None, params, tables, tm=256)
    past = prepare_cache(cache_k, cache_v, cache_logf) + (state_C, state_n, state_m, page_table)
    y_s, k_s, v_s, lf_s, C_s, n_s, m_s = _trunk(x_sample, mods[:, Bp:], past, params, tables,
                                               tm=x_sample.shape[0] * x_sample.shape[1])
    return (y_p, y_s, k_p, v_p, lf_p, C_p, n_p, m_p, k_s, v_s, lf_s, C_s, n_s, m_s)
```

```python
import functools
import math

import jax
import jax.numpy as jnp
from jax import lax
from jax.experimental import pallas as pl
from jax.experimental.pallas import tpu as pltpu

F32 = jnp.float32
BF16 = jnp.bfloat16
I32 = jnp.int32

RMS_EPS = 1e-6
LANES = 128
MXU_DIM = 256
VMEM_LIMIT = 52 * 1024 * 1024
PEER_TOPK = 16
PAGE_SIZE = 128
NEG_INF = float("-inf")


def _dot(a, b):
    return jnp.dot(a, b, preferred_element_type=F32)


def _dot_nt(a, b):
    return lax.dot_general(a, b, (((1,), (1,)), ((), ())), preferred_element_type=F32)


def _dot_tn(a, b):
    return lax.dot_general(a, b, (((0,), (0,)), ((), ())), preferred_element_type=F32)


def _split2(a):
    hi = a.astype(BF16)
    lo = (a - hi.astype(F32)).astype(BF16)
    return hi, lo


def _dot3(a, b):
    ah, al = _split2(a)
    bh, bl = _split2(b)
    return _dot(ah, bh) + _dot(ah, bl) + _dot(al, bh)


def _modnorm(x, gain, scale, shift):
    ms = jnp.mean(x * x, axis=-1, keepdims=True)
    return x * lax.rsqrt(ms + RMS_EPS) * gain * (1.0 + scale) + shift


def _log_sigmoid(z):
    return jnp.minimum(z, 0.0) - jnp.log1p(jnp.exp(-jnp.abs(z)))


def _params(*sem):
    return pltpu.CompilerParams(dimension_semantics=sem, vmem_limit_bytes=VMEM_LIMIT)


def _mod_spec(rows, d, tiles_per_group):
    return pl.BlockSpec((None, rows, d), lambda i: (i // tiles_per_group, 0, 0))


def _adaln_kernel(c_ref, w_ref, b_ref, o_ref):
    c = c_ref[...]
    o_ref[...] = _dot3(c * jax.nn.sigmoid(c), w_ref[...]) + b_ref[...]


def adaln(c, ada_w, ada_b, tn=1536):
    L, D, D6 = ada_w.shape
    R = c.shape[0]
    return pl.pallas_call(
        _adaln_kernel,
        grid=(L, D6 // tn),
        in_specs=[pl.BlockSpec((R, D), lambda l, j: (0, 0)),
                  pl.BlockSpec((None, D, tn), lambda l, j: (l, 0, j)),
                  pl.BlockSpec((None, 1, tn), lambda l, j: (l, 0, j))],
        out_specs=pl.BlockSpec((None, R, tn), lambda l, j: (l, 0, j)),
        out_shape=jax.ShapeDtypeStruct((L, R, D6), F32),
        compiler_params=_params("arbitrary", "arbitrary"),
        name="adaln",
    )(c, ada_w, ada_b.reshape(L, 1, D6))


def _segcumsum_kernel(x_ref, o_ref, *, seg):
    C, T = x_ref.shape
    lane = lax.broadcasted_iota(I32, (C, LANES), 1)
    w = min(seg, LANES)
    carry = None
    for blk in range(T // LANES):
        x = x_ref[:, blk * LANES:(blk + 1) * LANES]
        k = 1
        while k < w:
            x = x + jnp.where((lane & (w - 1)) >= k, pltpu.roll(x, k, 1), 0.0)
            k *= 2
        if seg > LANES:
            if blk % (seg // LANES) != 0:
                x = x + carry
            carry = x[:, LANES - 1:LANES]
        o_ref[:, blk * LANES:(blk + 1) * LANES] = x


def seg_cumsum(x, seg):
    R, C, T = x.shape
    assert T % LANES == 0 and (seg & (seg - 1)) == 0
    assert T % seg == 0 and (seg <= LANES or seg % LANES == 0)
    return pl.pallas_call(
        functools.partial(_segcumsum_kernel, seg=seg),
        grid=(R,),
        in_specs=[pl.BlockSpec((None, C, T), lambda r: (r, 0, 0))],
        out_specs=pl.BlockSpec((None, C, T), lambda r: (r, 0, 0)),
        out_shape=jax.ShapeDtypeStruct((R, C, T), F32),
        compiler_params=_params("arbitrary"),
        name="seg_cumsum",
    )(x)


def _pair_headnorm(blk, gain2, hd):
    lane = lax.broadcasted_iota(I32, blk.shape, 1)
    lo = lane < hd
    sq = blk * blk
    s_lo = jnp.sum(jnp.where(lo, sq, 0.0), axis=-1, keepdims=True)
    s_hi = jnp.sum(jnp.where(lo, 0.0, sq), axis=-1, keepdims=True)
    inv = lax.rsqrt(jnp.where(lo, s_lo, s_hi) * (1.0 / hd) + RMS_EPS)
    return blk * inv * gain2


def _fox_proj_kernel(x_ref, sc_ref, sh_ref, gain_ref, w_ref, wfh_ref, wfl_ref, bf_ref, qg_ref, kg_ref,
                     q_ref, k_ref, v_ref, kb_ref, vb_ref, sg_ref, lf_ref, *, n_heads):
    D = x_ref.shape[1]
    hd = D // n_heads
    h = _modnorm(x_ref[...], gain_ref[...], sc_ref[...], sh_ref[...])
    hh, hl = _split2(h)
    qk_scale = hd ** -0.5
    q = _dot(hh, w_ref[:, 0:D])
    for c in range(D // LANES):
        sl = slice(c * LANES, (c + 1) * LANES)
        q_ref[:, sl] = (_pair_headnorm(q[:, sl], qg_ref[...], hd) * qk_scale).astype(BF16)
    k = _dot(hh, w_ref[:, D:2 * D])
    for c in range(D // LANES):
        sl = slice(c * LANES, (c + 1) * LANES)
        kn = _pair_headnorm(k[:, sl], kg_ref[...], hd)
        k_ref[:, sl] = kn
        kb_ref[:, sl] = kn.astype(BF16)
    v = _dot(hh, w_ref[:, 2 * D:3 * D])
    v_ref[...] = v
    vb_ref[...] = v.astype(BF16)
    sg_ref[...] = jax.nn.sigmoid(_dot(hh, w_ref[:, 3 * D:4 * D]))
    f = _dot(hh, wfh_ref[...]) + _dot(hh, wfl_ref[...]) + _dot(hl, wfh_ref[...])
    lf_ref[...] = _log_sigmoid(f[:, 0:n_heads] + bf_ref[...])


def fox_project(x, scale, shift, gain, w_in, b_f, q_gain, k_gain, *, tm, tiles_per_group):
    N, D = x.shape
    H = b_f.shape[0]
    hd = D // H
    assert 2 * hd == LANES
    R = scale.shape[1]
    w_main = jnp.concatenate([w_in[:, :3 * D], w_in[:, 3 * D + H:]], axis=1).astype(BF16)
    w_f = jnp.pad(w_in[:, 3 * D:3 * D + H], ((0, 0), (0, LANES - H)))
    wfh = w_f.astype(BF16)
    wfl = (w_f - wfh.astype(F32)).astype(BF16)
    const = lambda i: (0, 0)
    row = lambda i: (i, 0)
    outs = pl.pallas_call(
        functools.partial(_fox_proj_kernel, n_heads=H),
        grid=(N // tm,),
        in_specs=[pl.BlockSpec((tm, D), row),
                  _mod_spec(R, D, tiles_per_group), _mod_spec(R, D, tiles_per_group),
                  pl.BlockSpec((1, D), const),
                  pl.BlockSpec((D, 4 * D), const),
                  pl.BlockSpec((D, LANES), const), pl.BlockSpec((D, LANES), const),
                  pl.BlockSpec((1, H), const),
                  pl.BlockSpec((1, LANES), const), pl.BlockSpec((1, LANES), const)],
        out_specs=[pl.BlockSpec((tm, D), row)] * 6 + [pl.BlockSpec((tm, H), row)],
        out_shape=[jax.ShapeDtypeStruct((N, D), BF16),
                   jax.ShapeDtypeStruct((N, D), F32), jax.ShapeDtypeStruct((N, D), F32),
                   jax.ShapeDtypeStruct((N, D), BF16), jax.ShapeDtypeStruct((N, D), BF16),
                   jax.ShapeDtypeStruct((N, D), F32),
                   jax.ShapeDtypeStruct((N, H), F32)],
        compiler_params=_params("arbitrary"),
        name="fox_project",
    )(x, scale, shift, gain.reshape(1, D), w_main, wfh, wfl, b_f.reshape(1, H),
      jnp.tile(q_gain, 2).reshape(1, LANES), jnp.tile(k_gain, 2).reshape(1, LANES))
    return outs


def _fox_attn_kernel(q_ref, k_ref, v_ref, fc_ref, fr_ref, o_ref, *, hd, n_sub, hps):
    TQ = q_ref.shape[0]
    SQ = TQ // n_sub
    qi = pl.program_id(2)
    lane = lax.broadcasted_iota(I32, (SQ, LANES), 1)
    row = lax.broadcasted_iota(I32, (SQ, TQ), 0)
    col = lax.broadcasted_iota(I32, (SQ, TQ), 1)
    chains = [(hh, sb) for hh in range(hps) for sb in range(n_sub)]
    qms, fqs = [], []
    for hh, sb in chains:
        pb, ph = hh // 2, hh % 2
        q = q_ref[sb * SQ:(sb + 1) * SQ, pb * LANES:(pb + 1) * LANES]
        qms.append(jnp.where((lane >= ph * hd) & (lane < (ph + 1) * hd), q, jnp.zeros_like(q)))
        fqs.append(fc_ref[sb * SQ:(sb + 1) * SQ, hh:hh + 1])

    def block_scores(j):
        start = pl.multiple_of(j * TQ, TQ)
        kjs = [k_ref[pl.ds(start, TQ), pb * LANES:(pb + 1) * LANES] for pb in range(hps // 2)]
        return tuple(_dot_nt(qms[c], kjs[hh // 2]) + (fqs[c] - fr_ref[hh:hh + 1, pl.ds(start, TQ)])
                     for c, (hh, sb) in enumerate(chains))

    def update(j, state, scores, masked):
        start = pl.multiple_of(j * TQ, TQ)
        vjs = [v_ref[pl.ds(start, TQ), pb * LANES:(pb + 1) * LANES] for pb in range(hps // 2)]
        ss = [jnp.where(col <= row + sb * SQ, scores[c], NEG_INF) if masked else scores[c]
              for c, (hh, sb) in enumerate(chains)]
        m_new = [jnp.maximum(state[c][0], jnp.max(ss[c], axis=-1, keepdims=True)) for c in range(len(chains))]
        ps = [jnp.exp(ss[c] - m_new[c]) for c in range(len(chains))]
        alphas = [jnp.exp(state[c][0] - m_new[c]) for c in range(len(chains))]
        pvs = [_dot(ps[c].astype(BF16), vjs[hh // 2]) for c, (hh, sb) in enumerate(chains)]
        return tuple((m_new[c],
                      alphas[c] * state[c][1] + jnp.sum(ps[c], axis=-1, keepdims=True),
                      alphas[c] * state[c][2] + pvs[c]) for c in range(len(chains)))

    def step(j, carry):
        state, scores = carry
        nxt = block_scores(j + 1)
        return update(j, state, scores, False), nxt

    init = tuple((jnp.full((SQ, 1), NEG_INF, F32), jnp.zeros((SQ, 1), F32), jnp.zeros((SQ, LANES), F32))
                 for _ in chains)
    state, scores = lax.fori_loop(0, qi, step, (init, block_scores(0)))
    final = update(qi, state, scores, True)
    for pb in range(hps // 2):
        for sb in range(n_sub):
            even = final[(2 * pb) * n_sub + sb]
            odd = final[(2 * pb + 1) * n_sub + sb]
            o_ref[sb * SQ:(sb + 1) * SQ, pb * LANES:(pb + 1) * LANES] = jnp.where(
                lane < hd, even[2] / even[1], odd[2] / odd[1])


def fox_attend_prompt(q, kb, vb, F, *, n_heads, tq=256, hps=2, n_sub=2):
    B, S, D = q.shape
    hd = D // n_heads
    assert 2 * hd == LANES and hps % 2 == 0 and n_heads % hps == 0
    HG = n_heads // hps
    W = hps * hd
    fcol = F.reshape(B, S, HG, hps).transpose(0, 2, 1, 3)
    frow = fcol.transpose(0, 1, 3, 2)
    return pl.pallas_call(
        functools.partial(_fox_attn_kernel, hd=hd, n_sub=n_sub, hps=hps),
        grid=(B, HG, S // tq),
        in_specs=[pl.BlockSpec((None, tq, W), lambda b, h, i: (b, i, h)),
                  pl.BlockSpec((None, S, W), lambda b, h, i: (b, 0, h)),
                  pl.BlockSpec((None, S, W), lambda b, h, i: (b, 0, h)),
                  pl.BlockSpec((None, None, tq, hps), lambda b, h, i: (b, h, i, 0)),
                  pl.BlockSpec((None, None, hps, S), lambda b, h, i: (b, h, 0, 0))],
        out_specs=pl.BlockSpec((None, tq, W), lambda b, h, i: (b, i, h)),
        out_shape=jax.ShapeDtypeStruct((B, S, D), F32),
        compiler_params=_params("arbitrary", "arbitrary", "arbitrary"),
        name="fox_attend_prompt",
    )(q, kb, vb, fcol, frow)


def _fox_decode_kernel(pt_ref, qbd_ref, *refs, n_heads, pp):
    del pt_ref
    kt_refs, vt_refs, cf_refs = refs[:pp], refs[pp:2 * pp], refs[2 * pp:3 * pp]
    kn_ref, vn_ref, cn_ref, o_ref, m_scr, l_scr, acc_scr, carry_scr = refs[3 * pp:]
    H = n_heads
    step = pl.program_id(1)
    n_steps = pl.num_programs(1)
    R, D = qbd_ref.shape
    T = R // H
    P = cf_refs[0].shape[1]

    @pl.when(step == 0)
    def _():
        m_scr[...] = jnp.full(m_scr.shape, NEG_INF, F32)
        l_scr[...] = jnp.zeros(l_scr.shape, F32)
        acc_scr[...] = jnp.zeros(acc_scr.shape, F32)
        carry_scr[...] = jnp.zeros(carry_scr.shape, F32)

    qbd = qbd_ref[...]

    def merge(m_i, l_i, acc_i):
        m = m_scr[...]
        m_new = jnp.maximum(m, m_i)
        a_old = jnp.exp(m - m_new)
        a_new = jnp.exp(m_i - m_new)
        l_scr[...] = a_old * l_scr[...] + a_new * l_i
        acc_scr[...] = a_old * acc_scr[...] + a_new * acc_i
        m_scr[...] = m_new

    def scores(s, fk):
        return s - jnp.concatenate([fk] * T, axis=0)

    carry = carry_scr[...]
    ss = []
    for i in range(pp):
        cf = cf_refs[i][...]
        ss.append(scores(_dot(qbd, kt_refs[i][...].astype(BF16)), cf + carry))
        carry = carry + cf[:, P - 1:P]
    carry_scr[...] = carry
    m_s = jnp.max(ss[0], axis=-1, keepdims=True)
    for s in ss[1:]:
        m_s = jnp.maximum(m_s, jnp.max(s, axis=-1, keepdims=True))
    l_s = jnp.zeros_like(m_s)
    acc_s = jnp.zeros(acc_scr.shape, F32)
    for i, s in enumerate(ss):
        pr = jnp.exp(s - m_s)
        l_s = l_s + jnp.sum(pr, axis=-1, keepdims=True)
        acc_s = acc_s + _dot_nt(pr.astype(BF16), vt_refs[i][...].astype(BF16))
    merge(m_s, l_s, acc_s)

    @pl.when(step == n_steps - 1)
    def _():
        s2 = scores(_dot_nt(qbd, kn_ref[...]), cn_ref[...] + carry_scr[...])
        key = lax.broadcasted_iota(I32, s2.shape, 1)
        t = lax.broadcasted_iota(I32, s2.shape, 0) // H
        s2 = jnp.where(key <= t, s2, NEG_INF)
        m2 = jnp.max(s2, axis=-1, keepdims=True)
        p2 = jnp.exp(s2 - m2)
        merge(m2, jnp.sum(p2, axis=-1, keepdims=True), _dot(p2.astype(BF16), vn_ref[...]))
        out = acc_scr[...] / l_scr[...]
        rh = lax.broadcasted_iota(I32, (R, D), 0) % H
        ch = lax.broadcasted_iota(I32, (R, D), 1) // (D // H)
        out = jnp.where(rh == ch, out, 0.0)
        o_ref[...] = jnp.sum(out.reshape(T, H, D), axis=1)


def fox_attend_sample(q, kb_new, vb_new, lf_new, cache_kt, cache_vt, cache_cf, page_table, j, *, n_heads):
    Bd, T, D = q.shape
    H = n_heads
    hd = D // H
    n_pages = page_table.shape[1]
    n_fox, n_pool, _, P = cache_cf.shape
    R = T * H
    pp = 4 if n_pages % 4 == 0 else 1
    assert T <= P
    q4 = q.reshape(Bd, T, H, hd)
    eye = jnp.eye(H, dtype=q.dtype)
    qbd = (q4[:, :, :, None, :] * eye[None, None, :, :, None]).reshape(Bd, R, D)
    pad = ((0, 0), (0, P - T), (0, 0))
    kn = jnp.pad(kb_new, pad)
    vn = jnp.pad(vb_new, pad)
    cn = seg_cumsum(jnp.pad(lf_new.transpose(0, 2, 1), ((0, 0), (0, 0), (0, P - T))), P)
    pt = page_table.reshape(-1).astype(I32)

    def page_spec(shape, i):
        return pl.BlockSpec((None, None) + shape, lambda b, s, pt: (j, pt[b * n_pages + s * pp + i], 0, 0))

    per_seq = lambda shape: pl.BlockSpec((None,) + shape, lambda b, s, pt: (b, 0, 0))
    grid_spec = pltpu.PrefetchScalarGridSpec(
        num_scalar_prefetch=1,
        grid=(Bd, n_pages // pp),
        in_specs=([per_seq((R, D))]
                  + [page_spec((D, P), i) for i in range(pp)]
                  + [page_spec((D, P), i) for i in range(pp)]
                  + [page_spec((H, P), i) for i in range(pp)]
                  + [per_seq((P, D)), per_seq((P, D)), per_seq((H, P))]),
        out_specs=per_seq((T, D)),
        scratch_shapes=[pltpu.VMEM((R, 1), F32), pltpu.VMEM((R, 1), F32),
                        pltpu.VMEM((R, D), F32), pltpu.VMEM((H, 1), F32)],
    )
    return pl.pallas_call(
        functools.partial(_fox_decode_kernel, n_heads=H, pp=pp),
        grid_spec=grid_spec,
        out_shape=jax.ShapeDtypeStruct((Bd, T, D), F32),
        compiler_params=_params("arbitrary", "arbitrary"),
        name="fox_attend_sample",
    )(pt, qbd, *([cache_kt] * pp), *([cache_vt] * pp), *([cache_cf] * pp), kn, vn, cn)


def prepare_cache(cache_k, cache_v, cache_logf):
    n_fox, n_pool, P, H, hd = cache_k.shape
    kt = cache_k.transpose(0, 1, 3, 4, 2).reshape(n_fox, n_pool, H * hd, P)
    vt = cache_v.transpose(0, 1, 3, 4, 2).reshape(n_fox, n_pool, H * hd, P)
    rows = n_fox * n_pool * H
    group = 2048 if rows % 2048 == 0 else H
    cf = seg_cumsum(cache_logf.transpose(0, 1, 3, 2).reshape(rows // group, group, P), P)
    return kt, vt, cf.reshape(n_fox, n_pool, H, P)


def _outproj_kernel(a_ref, gate_ref, w_ref, x_ref, g_ref, hn_ref, o_ref, *, head_norm):
    a = a_ref[...]
    D = a.shape[1]
    if head_norm:
        blocks = []
        for c in range(D // LANES):
            sl = slice(c * LANES, (c + 1) * LANES)
            blk = a[:, sl]
            ms = jnp.mean(blk * blk, axis=-1, keepdims=True)
            blocks.append(blk * lax.rsqrt(ms + RMS_EPS) * hn_ref[:, sl])
        a = jnp.concatenate(blocks, axis=-1)
    y = (a * gate_ref[...]).astype(BF16)
    o_ref[...] = x_ref[...] + g_ref[...] * _dot(y, w_ref[...])


def out_project(a, gate, w_out, x, g, head_gain, *, tm, tiles_per_group, head_norm):
    N, D = x.shape
    R = g.shape[1]
    row = lambda i: (i, 0)
    const = lambda i: (0, 0)
    return pl.pallas_call(
        functools.partial(_outproj_kernel, head_norm=head_norm),
        grid=(N // tm,),
        in_specs=[pl.BlockSpec((tm, D), row), pl.BlockSpec((tm, D), row),
                  pl.BlockSpec((D, D), const), pl.BlockSpec((tm, D), row),
                  _mod_spec(R, D, tiles_per_group), pl.BlockSpec((1, D), const)],
        out_specs=pl.BlockSpec((tm, D), row),
        out_shape=jax.ShapeDtypeStruct((N, D), F32),
        compiler_params=_params("arbitrary"),
        name="out_project",
    )(a, gate, w_out.astype(BF16), x, g, head_gain.reshape(1, D))


def _ml_proj_kernel(x_ref, sc_ref, sh_ref, gain_ref, w_ref, wgh_ref, wgl_ref, bi_ref, bf_ref,
                    q_ref, k_ref, v_ref, so_ref, il_ref, fl_ref, *, n_heads, qk_width):
    D = x_ref.shape[1]
    QW = qk_width
    h = _modnorm(x_ref[...], gain_ref[...], sc_ref[...], sh_ref[...])
    hh, hl = _split2(h)
    k_scale = (QW // n_heads) ** -0.5
    q_ref[...] = _dot(hh, w_ref[:, 0:QW]).astype(BF16)
    k_ref[...] = (_dot(hh, w_ref[:, QW:2 * QW]) * k_scale).astype(BF16)
    v_ref[...] = _dot(hh, w_ref[:, 2 * QW:2 * QW + D])
    so_ref[...] = jax.nn.sigmoid(_dot(hh, w_ref[:, 2 * QW + D:2 * QW + 2 * D]))
    gts = _dot(hh, wgh_ref[...]) + _dot(hh, wgl_ref[...]) + _dot(hl, wgh_ref[...])
    il_ref[...] = gts[:, 0:n_heads] + bi_ref[...]
    fl_ref[...] = _log_sigmoid(gts[:, n_heads:2 * n_heads] + bf_ref[...])


def ml_project(x, scale, shift, gain, w_in, b_i, b_f, *, tm, tiles_per_group, v_width):
    N, D = x.shape
    H = b_i.shape[0]
    QW = (w_in.shape[1] - 2 * v_width - 2 * H) // 2
    R = scale.shape[1]
    g0 = 2 * QW + v_width
    w_main = jnp.concatenate([w_in[:, :g0], w_in[:, g0 + 2 * H:]], axis=1).astype(BF16)
    w_g = jnp.pad(w_in[:, g0:g0 + 2 * H], ((0, 0), (0, LANES - 2 * H)))
    wgh = w_g.astype(BF16)
    wgl = (w_g - wgh.astype(F32)).astype(BF16)
    const = lambda i: (0, 0)
    row = lambda i: (i, 0)
    WM = w_main.shape[1]
    return pl.pallas_call(
        functools.partial(_ml_proj_kernel, n_heads=H, qk_width=QW),
        grid=(N // tm,),
        in_specs=[pl.BlockSpec((tm, D), row),
                  _mod_spec(R, D, tiles_per_group), _mod_spec(R, D, tiles_per_group),
                  pl.BlockSpec((1, D), const),
                  pl.BlockSpec((D, WM), const),
                  pl.BlockSpec((D, LANES), const), pl.BlockSpec((D, LANES), const),
                  pl.BlockSpec((1, H), const), pl.BlockSpec((1, H), const)],
        out_specs=[pl.BlockSpec((tm, QW), row), pl.BlockSpec((tm, QW), row),
                   pl.BlockSpec((tm, v_width), row), pl.BlockSpec((tm, v_width), row),
                   pl.BlockSpec((tm, H), row), pl.BlockSpec((tm, H), row)],
        out_shape=[jax.ShapeDtypeStruct((N, QW), BF16), jax.ShapeDtypeStruct((N, QW), BF16),
                   jax.ShapeDtypeStruct((N, v_width), F32), jax.ShapeDtypeStruct((N, v_width), F32),
                   jax.ShapeDtypeStruct((N, H), F32), jax.ShapeDtypeStruct((N, H), F32)],
        compiler_params=_params("arbitrary"),
        name="ml_project",
    )(x, scale, shift, gain.reshape(1, D), w_main, wgh, wgl, b_i.reshape(1, H), b_f.reshape(1, H))


def _mlstm_kernel(q_ref, k_ref, v_ref, ic_ref, ir_ref, bc_ref, br_ref, c0_ref, n0_ref, m0_ref,
                  h_ref, c_ref, n_ref, m_ref, *, n_heads, dk):
    nbb, L = q_ref.shape[0], q_ref.shape[1]
    ci = pl.program_id(1)

    @pl.when(ci == 0)
    def _():
        c_ref[...] = c0_ref[...]
        n_ref[...] = n0_ref[...]
        m_ref[...] = m0_ref[...]

    tri = lax.broadcasted_iota(I32, (L, L), 1) <= lax.broadcasted_iota(I32, (L, L), 0)
    lane = lax.broadcasted_iota(I32, (L, LANES), 1)
    lane1 = lax.broadcasted_iota(I32, (1, LANES), 1)
    rsub = lax.broadcasted_iota(I32, (LANES, LANES), 0)
    dv = LANES
    m_in = [m_ref[bb] for bb in range(nbb)]
    for bb, pr in [(bb, pr) for bb in range(nbb) for pr in range(n_heads // 2)]:
        q, k = q_ref[bb], k_ref[bb]
        ic, ir, bc, br = ic_ref[bb], ir_ref[bb], bc_ref[bb], br_ref[bb]
        m_all = m_in[bb]
        sl = slice(pr * LANES, (pr + 1) * LANES)
        qp = q[:, sl]
        kp = k[:, sl]
        kpf = kp.astype(F32)
        qpf = qp.astype(F32)
        c_pair = c_ref[bb, pr]
        n_pair = n_ref[bb, pr]
        cb = c_pair.astype(BF16)
        new_c, new_n = [], []
        for hh in range(2):
            h = 2 * pr + hh
            sel = (lane >= hh * dk) & (lane < (hh + 1) * dk)
            qm = jnp.where(sel, qp, jnp.zeros_like(qp))
            b_l = bc[:, h:h + 1]
            b_s = br[h:h + 1, :]
            ig_s = ir[h:h + 1, :]
            ig_l = ic[:, h:h + 1]
            m = m_all[:, h:h + 1]
            dm = jnp.where(tri, b_l - b_s + ig_s, NEG_INF)
            inter = b_l + m
            m_t = jnp.maximum(inter, jnp.max(dm, axis=-1, keepdims=True))
            w_inter = jnp.exp(inter - m_t)
            a = jnp.exp(dm - m_t) * _dot_nt(qm, kp)
            vh = v_ref[bb, :, h * dv:(h + 1) * dv]
            num = w_inter * _dot(qm, cb) + _dot(a.astype(BF16), vh.astype(BF16))
            qn = jnp.sum(jnp.where(sel, qpf, 0.0) * n_pair, axis=-1, keepdims=True)
            den = w_inter * qn + jnp.sum(a, axis=-1, keepdims=True)
            h_ref[bb, :, h * dv:(h + 1) * dv] = num / jnp.maximum(jnp.abs(den), jnp.exp(-m_t))
            b_last = b_l[L - 1:L, :]
            g_row = b_last - b_s + ig_s
            m_new = jnp.maximum(b_last + m, jnp.max(g_row, axis=-1, keepdims=True))
            wk = jnp.exp(b_last - b_l + ig_l - m_new)
            decay = jnp.exp(b_last + m - m_new)
            upd = _dot_tn(kp, (wk * vh).astype(BF16))
            new_c.append(decay * c_pair + upd)
            new_n.append(decay * n_pair + jnp.sum(wk * kpf, axis=0, keepdims=True))
            m_ref[bb, :, h:h + 1] = m_new
        c_ref[bb, pr] = jnp.where(rsub < dk, new_c[0], new_c[1])
        n_ref[bb, pr] = jnp.where(lane1 < dk, new_n[0], new_n[1])


def mlstm_chunked(q, k, v, i_log, f_log, C0, n0, m0, chunk):
    B, T, _ = q.shape
    H = i_log.shape[-1]
    dk = C0.shape[2]
    dv = C0.shape[3]
    assert 2 * dk == LANES and dv == LANES
    nc = T // chunk
    L = chunk
    Tp = -(-T // LANES) * LANES
    fT = jnp.pad(f_log.transpose(0, 2, 1), ((0, 0), (0, 0), (0, Tp - T)))
    bT = seg_cumsum(fT, chunk)[:, :, :T]
    br = bT.reshape(B, H, nc, L).transpose(0, 2, 1, 3)
    bc = br.transpose(0, 1, 3, 2)
    ic = i_log.reshape(B, nc, L, H)
    ir = ic.transpose(0, 1, 3, 2)
    HP = H // 2
    c0 = C0.reshape(B, HP, 2 * dk, dv)
    n0p = n0.reshape(B, HP, 1, 2 * dk)
    m0p = m0.reshape(B, 1, H)
    QW = H * dk
    VW = H * dv
    nbb = 2 if B % 2 == 0 else 1
    col_spec = pl.BlockSpec((nbb, None, L, H), lambda b, c: (b, c, 0, 0))
    row_spec = pl.BlockSpec((nbb, None, H, L), lambda b, c: (b, c, 0, 0))
    st = lambda b, c: (b, 0, 0, 0)
    h, Cn, nn, mn = pl.pallas_call(
        functools.partial(_mlstm_kernel, n_heads=H, dk=dk),
        grid=(B // nbb, nc),
        in_specs=[pl.BlockSpec((nbb, L, QW), lambda b, c: (b, c, 0)),
                  pl.BlockSpec((nbb, L, QW), lambda b, c: (b, c, 0)),
                  pl.BlockSpec((nbb, L, VW), lambda b, c: (b, c, 0)),
                  col_spec, row_spec, col_spec, row_spec,
                  pl.BlockSpec((nbb, HP, 2 * dk, dv), st),
                  pl.BlockSpec((nbb, HP, 1, 2 * dk), st),
                  pl.BlockSpec((nbb, 1, H), lambda b, c: (b, 0, 0))],
        out_specs=[pl.BlockSpec((nbb, L, VW), lambda b, c: (b, c, 0)),
                   pl.BlockSpec((nbb, HP, 2 * dk, dv), st),
                   pl.BlockSpec((nbb, HP, 1, 2 * dk), st),
                   pl.BlockSpec((nbb, 1, H), lambda b, c: (b, 0, 0))],
        out_shape=[jax.ShapeDtypeStruct((B, T, VW), F32),
                   jax.ShapeDtypeStruct((B, HP, 2 * dk, dv), F32),
                   jax.ShapeDtypeStruct((B, HP, 1, 2 * dk), F32),
                   jax.ShapeDtypeStruct((B, 1, H), F32)],
        compiler_params=_params("arbitrary", "arbitrary"),
        name="mlstm_chunked",
    )(q, k, v, ic, ir, bc, br, c0, n0p, m0p)
    return h, Cn.reshape(B, H, dk, dv), nn.reshape(B, H, dk), mn.reshape(B, H)


def _staircase(n):
    return [(a, n // (a + 1)) for a in range(n)]


def _extract_topk(ss, payloads, n):
    ss = list(ss)
    R = ss[0].shape[0]
    ridx = lax.broadcasted_iota(I32, ss[0].shape, 0).astype(F32)
    vals = [[] for _ in ss]
    pays = [[] for _ in ss]
    for it in range(n):
        for c in range(len(ss)):
            s = ss[c]
            m = jnp.max(s, axis=0, keepdims=True)
            am = jnp.min(jnp.where(s == m, ridx, float(R)), axis=0, keepdims=True)
            hit = ridx == am
            if payloads[c] is None:
                pays[c].append(am)
            else:
                pays[c].append(jnp.max(jnp.where(hit, payloads[c], -1.0), axis=0, keepdims=True))
            vals[c].append(m)
            if it + 1 < n:
                ss[c] = jnp.where(hit, NEG_INF, s)
    return vals, pays


def _peer_topk_kernel(x_ref, sc_ref, sh_ref, gain_ref, wq_ref, k1_ref, k2_ref,
                      hb_ref, e_ref, g_ref, q_scr, v1_scr, i1_scr, v2_scr, i2_scr, *, n_heads, n_keys):
    TM = x_ref.shape[0]
    K = PEER_TOPK
    half = k1_ref.shape[2]
    hb = _modnorm(x_ref[...], gain_ref[...], sc_ref[...], sh_ref[...]).astype(BF16)
    hb_ref[...] = hb
    q_scr[...] = _dot(hb, wq_ref[...]).astype(BF16)
    sub8 = lax.broadcasted_iota(I32, (8, LANES), 0)

    def head_body(hd, _):
        c0 = pl.multiple_of(hd * 2 * half, 2 * half)
        c1 = pl.multiple_of(hd * 2 * half + half, half)
        s1 = _dot_nt(k1_ref[hd], q_scr[:, pl.ds(c0, half)])
        s2 = _dot_nt(k2_ref[hd], q_scr[:, pl.ds(c1, half)])
        NB = TM // LANES
        blocks = [slice(blk * LANES, (blk + 1) * LANES) for blk in range(NB)]
        vals, idxs = _extract_topk([s[:, ls] for ls in blocks for s in (s1, s2)], [None] * (2 * NB), K)
        for blk in range(NB):
            for side, (v_scr, i_scr) in enumerate(((v1_scr, i1_scr), (v2_scr, i2_scr))):
                for r in range(K):
                    v_scr[blk, r:r + 1, :] = vals[2 * blk + side][r]
                    i_scr[blk, r:r + 1, :] = idxs[2 * blk + side][r]
        cands, pays = [], []
        for blk in range(NB):
            cand, pay = [], []
            for a, cnt in _staircase(K):
                if cnt == 1:
                    break
                for b0 in range(0, cnt, 8):
                    nb = min(8, cnt - b0)
                    cv = v1_scr[blk, a:a + 1, :] + v2_scr[blk, b0:b0 + 8, :]
                    pv = i1_scr[blk, a:a + 1, :] * n_keys + i2_scr[blk, b0:b0 + 8, :]
                    if nb < 8:
                        cv = jnp.where(sub8 < nb, cv, NEG_INF)
                    cand.append(cv)
                    pay.append(pv)
            a1 = K // 2
            cand.append(v1_scr[blk, a1:K, :] + v2_scr[blk, 0:1, :])
            pay.append(i1_scr[blk, a1:K, :] * n_keys + i2_scr[blk, 0:1, :])
            cands.append(jnp.concatenate(cand, axis=0))
            pays.append(jnp.concatenate(pay, axis=0))
        tops, topes = _extract_topk(cands, pays, K)
        out_rows = pl.ds(pl.multiple_of(hd * K, K), K)
        for blk in range(NB):
            top_s, top_e = tops[blk], topes[blk]
            ex = [jnp.exp(t - top_s[0]) for t in top_s]
            den = ex[0]
            for t in ex[1:]:
                den = den + t
            inv = 1.0 / den
            for r in range(K):
                i1_scr[blk, r:r + 1, :] = top_e[r]
                v1_scr[blk, r:r + 1, :] = ex[r] * inv
            e_ref[out_rows, blocks[blk]] = i1_scr[blk].astype(I32)
            g_ref[out_rows, blocks[blk]] = v1_scr[blk]
        return 0

    lax.fori_loop(0, n_heads, head_body, 0)


def peer_retrieve(x, scale, shift, gain, w_q, sub_k1, sub_k2, *, tm, tiles_per_group):
    N, D = x.shape
    HP, NK, half = sub_k1.shape
    assert NK == LANES and half == LANES and PEER_TOPK % 8 == 0
    R = scale.shape[1]
    QW = w_q.shape[1]
    J = HP * PEER_TOPK
    const = lambda i: (0, 0)
    const3 = lambda i: (0, 0, 0)
    kspec = pl.BlockSpec((HP, NK, half), const3)
    return pl.pallas_call(
        functools.partial(_peer_topk_kernel, n_heads=HP, n_keys=NK),
        grid=(N // tm,),
        in_specs=[pl.BlockSpec((tm, D), lambda i: (i, 0)),
                  _mod_spec(R, D, tiles_per_group), _mod_spec(R, D, tiles_per_group),
                  pl.BlockSpec((1, D), const),
                  pl.BlockSpec((D, QW), const),
                  kspec, kspec],
        out_specs=[pl.BlockSpec((tm, D), lambda i: (i, 0)),
                   pl.BlockSpec((J, tm), lambda i: (0, i)),
                   pl.BlockSpec((J, tm), lambda i: (0, i))],
        out_shape=[jax.ShapeDtypeStruct((N, D), BF16),
                   jax.ShapeDtypeStruct((J, N), I32),
                   jax.ShapeDtypeStruct((J, N), F32)],
        scratch_shapes=[pltpu.VMEM((tm, QW), BF16),
                        ] + [pltpu.VMEM((tm // LANES, PEER_TOPK, LANES), F32)] * 4,
        compiler_params=_params("arbitrary"),
        name="peer_retrieve",
    )(x, scale, shift, gain.reshape(1, D), w_q.astype(BF16), sub_k1.astype(BF16), sub_k2.astype(BF16))


def _peer_mix_kernel(hb_ref, i1_ref, i2_ref, g_ref, ut_ref, v_ref, x_ref, g2_ref, o_ref,
                     a_scr, w_scr, hbuf_scr, y_scr, *, n_keys, tg):
    TM = hb_ref.shape[0]
    TE = ut_ref.shape[1]
    J = i1_ref.shape[1]
    NE = pl.num_programs(1) // 2
    j = pl.program_id(1)
    NP2 = n_keys // 2
    W2 = 2 * n_keys

    @pl.when(j == 0)
    def _():
        a_scr[...] = jnp.zeros(a_scr.shape, F32)

    @pl.when(j < NE)
    def _():
        act = _dot(hb_ref[...], ut_ref[...])
        i1 = i1_ref[...]
        i2 = i2_ref[...]
        acc = a_scr[...]
        for c in range(TE // n_keys):
            r = j * (TE // n_keys) + c
            picked = jnp.take_along_axis(act[:, c * n_keys:(c + 1) * n_keys], i2, axis=1)
            acc = jnp.where(i1 == r, picked, acc)
        a_scr[...] = acc

    @pl.when(j == NE)
    def _():
        a = a_scr[...]
        gelu = 0.5 * a * (1.0 + lax.erf(a * math.sqrt(0.5)))
        w_scr[...] = g_ref[...] * gelu
        y_scr[...] = jnp.zeros(y_scr.shape, F32)
        rows = lax.broadcasted_iota(I32, (NP2, J), 0)
        cols = lax.broadcasted_iota(I32, (W2, J), 0)

        def group_body(gi, _):
            t0 = pl.multiple_of(gi * tg, tg)
            grids = []
            for u in range(tg):
                i1 = i1_ref[pl.ds(t0 + u, 1), :]
                i2 = i2_ref[pl.ds(t0 + u, 1), :]
                w = w_scr[pl.ds(t0 + u, 1), :]
                p1 = jnp.where(rows == (i1 >> 1), w, 0.0).astype(BF16)
                p2 = jnp.where(cols == ((i1 & 1) * n_keys + i2), 1.0, 0.0).astype(BF16)
                grids.append(_dot_nt(p1, p2))
            hbuf_scr[:, pl.ds(t0, tg), :] = pltpu.einshape("tpc->ptc", jnp.stack(grids, axis=0)).astype(BF16)
            return 0

        lax.fori_loop(0, TM // tg, group_body, 0)

    @pl.when(j >= NE)
    def _():
        jj = j - NE
        y = y_scr[...]
        for c in range(TE // W2):
            y = y + _dot(hbuf_scr[jj * (TE // W2) + c], v_ref[c * W2:(c + 1) * W2, :])
        y_scr[...] = y

    @pl.when(j == 2 * NE - 1)
    def _():
        o_ref[...] = x_ref[...] + g2_ref[...] * y_scr[...]


def peer_mix(hb, i1, i2, g, ut_tab, v_tab, x, g2, *, tm, te, tiles_per_group, n_keys):
    N, D = x.shape
    E = v_tab.shape[0]
    J = i1.shape[1]
    NE = E // te
    R = g2.shape[1]
    tg = 16
    assert te % (2 * n_keys) == 0 and n_keys == LANES and tm % tg == 0
    return pl.pallas_call(
        functools.partial(_peer_mix_kernel, n_keys=n_keys, tg=tg),
        grid=(N // tm, 2 * NE),
        in_specs=[pl.BlockSpec((tm, D), lambda i, j: (i, 0)),
                  pl.BlockSpec((tm, J), lambda i, j: (i, 0)),
                  pl.BlockSpec((tm, J), lambda i, j: (i, 0)),
                  pl.BlockSpec((tm, J), lambda i, j: (i, 0)),
                  pl.BlockSpec((D, te), lambda i, j: (0, jnp.minimum(j, NE - 1))),
                  pl.BlockSpec((te, D), lambda i, j: (jnp.maximum(j - NE, 0), 0)),
                  pl.BlockSpec((tm, D), lambda i, j: (i, 0)),
                  pl.BlockSpec((None, R, D), lambda i, j: (i // tiles_per_group, 0, 0))],
        out_specs=pl.BlockSpec((tm, D), lambda i, j: (i, 0)),
        out_shape=jax.ShapeDtypeStruct((N, D), F32),
        scratch_shapes=[pltpu.VMEM((tm, J), F32), pltpu.VMEM((tm, J), F32),
                        pltpu.VMEM((n_keys // 2, tm, 2 * n_keys), BF16),
                        pltpu.VMEM((tm, D), F32)],
        compiler_params=_params("arbitrary", "arbitrary"),
        name="peer_mix",
    )(hb, i1, i2, g, ut_tab, v_tab, x, g2)


def _trunk(x3, mods, past, params, tables, *, tm):
    (norm_mix, norm_ffn, fox_w_in, fox_b_f, fox_q_norm, fox_k_norm, fox_w_out,
     ml_w_in, ml_b_i, ml_b_f, ml_h_norm, ml_w_out, peer_w_q, peer_sub_k1, peer_sub_k2) = params
    ut_bf, v_bf = tables
    B, T, D = x3.shape
    N = B * T
    depth = norm_mix.shape[0]
    n_fox_heads = fox_b_f.shape[1]
    n_keys = peer_sub_k1.shape[2]
    x = x3.reshape(N, D)
    if T % tm == 0:
        tpg, rows = T // tm, 1
        expand = lambda a: a.reshape(B, 1, D)
    else:
        assert N == tm
        tpg, rows = 1, N
        expand = lambda a: jnp.repeat(a, T, axis=0).reshape(1, N, D)
    new_k, new_v, new_lf, new_C, new_n, new_m = [], [], [], [], [], []
    for layer in range(depth):
        sh1, sc1, g1, sh2, sc2, g2 = [expand(a) for a in jnp.split(mods[layer], 6, axis=-1)]
        j = layer // 2
        if layer % 2 == 0:
            q, k, v, kb, vb, sg, lf = fox_project(
                x, sc1, sh1, norm_mix[layer], fox_w_in[j], fox_b_f[j], fox_q_norm[j], fox_k_norm[j],
                tm=tm, tiles_per_group=tpg)
            if past is None:
                F = seg_cumsum(lf.reshape(B, T, n_fox_heads).transpose(0, 2, 1), T).transpose(0, 2, 1)
                o = fox_attend_prompt(q.reshape(B, T, D), kb.reshape(B, T, D), vb.reshape(B, T, D), F,
                                      n_heads=n_fox_heads)
            else:
                cache_kt, cache_vt, cache_cf, _, _, _, page_table = past
                o = fox_attend_sample(q.reshape(B, T, D), kb.reshape(B, T, D), vb.reshape(B, T, D),
                                      lf.reshape(B, T, n_fox_heads), cache_kt, cache_vt, cache_cf,
                                      page_table, j, n_heads=n_fox_heads)
            x = out_project(o.reshape(N, D), sg, fox_w_out[j], x, g1, jnp.ones((D,), F32),
                            tm=tm, tiles_per_group=tpg, head_norm=False)
            new_k.append(k.reshape(B, T, n_fox_heads, -1))
            new_v.append(v.reshape(B, T, n_fox_heads, -1))
            new_lf.append(lf.reshape(B, T, n_fox_heads))
        else:
            n_ml_heads = ml_b_i.shape[1]
            v_width = ml_w_out.shape[1]
            q, k, v, so, il, fl = ml_project(x, sc1, sh1, norm_mix[layer], ml_w_in[j], ml_b_i[j], ml_b_f[j],
                                             tm=tm, tiles_per_group=tpg, v_width=v_width)
            QW = q.shape[1]
            dk = QW // n_ml_heads
            dv = v_width // n_ml_heads
            q3, k3, v3 = q.reshape(B, T, QW), k.reshape(B, T, QW), v.reshape(B, T, v_width)
            il3, fl3 = il.reshape(B, T, n_ml_heads), fl.reshape(B, T, n_ml_heads)
            if past is None:
                C0 = jnp.zeros((B, n_ml_heads, dk, dv), F32)
                n0 = jnp.zeros((B, n_ml_heads, dk), F32)
                m0 = jnp.zeros((B, n_ml_heads), F32)
                hh, C, n_, m_ = mlstm_chunked(q3, k3, v3, il3, fl3, C0, n0, m0, 64)
            else:
                _, _, _, state_C, state_n, state_m, _ = past
                Tp = 16
                pad3 = ((0, 0), (0, Tp - T), (0, 0))
                hh, C, n_, m_ = mlstm_chunked(
                    jnp.pad(q3, pad3), jnp.pad(k3, pad3), jnp.pad(v3, pad3),
                    jnp.pad(il3, pad3, constant_values=-1e30), jnp.pad(fl3, pad3),
                    state_C[j], state_n[j], state_m[j], Tp)
                hh = hh[:, :T]
            x = out_project(hh.reshape(N, v_width), so, ml_w_out[j], x, g1, ml_h_norm[j].reshape(-1),
                            tm=tm, tiles_per_group=tpg, head_norm=True)
            new_C.append(C)
            new_n.append(n_)
            new_m.append(m_)
        hb, e, g = peer_retrieve(x, sc2, sh2, norm_ffn[layer], peer_w_q[layer], peer_sub_k1[layer],
                                 peer_sub_k2[layer], tm=tm, tiles_per_group=tpg)
        eT = e.T
        mix_tm = 2 * tm if tpg % 2 == 0 else tm
        x = peer_mix(hb, eT // n_keys, eT % n_keys, g.T, ut_bf[layer], v_bf[layer], x, g2,
                     tm=mix_tm, te=2048, tiles_per_group=tpg * tm // mix_tm, n_keys=n_keys)
    return (x.reshape(B, T, D), jnp.stack(new_k), jnp.stack(new_v), jnp.stack(new_lf),
            jnp.stack(new_C), jnp.stack(new_n), jnp.stack(new_m))


def kernel(x_prompt, x_sample, cache_k, cache_v, cache_logf, state_C, state_n, state_m, page_table,
           c_prompt, c_sample, ada_w, ada_b, norm_mix, norm_ffn, fox_w_in, fox_b_f, fox_q_norm, fox_k_norm,
           fox_w_out, ml_w_in, ml_b_i, ml_b_f, ml_h_norm, ml_w_out, peer_w_q, peer_sub_k1, peer_sub_k2,
           peer_u, peer_v):
    params = (norm_mix, norm_ffn, fox_w_in, fox_b_f, fox_q_norm, fox_k_norm, fox_w_out,
              ml_w_in, ml_b_i, ml_b_f, ml_h_norm, ml_w_out, peer_w_q, peer_sub_k1, peer_sub_k2)
    tables = (peer_u.astype(BF16).transpose(0, 2, 1), peer_v.astype(BF16))
    Bp = c_prompt.shape[0]
    mods = adaln(jnp.concatenate([c_prompt, c_sample], axis=0), ada_w, ada_b)
    y_p, k_p, v_p, lf_p, C_p, n_p, m_p = _trunk(x_prompt, mods[:, :Bp], None, params, tables, tm=256)
    past = prepare_cache(cache_k, cache_v, cache_logf) + (state_C, state_n, state_m, page_table)
    y_s, k_s, v_s, lf_s, C_s, n_s, m_s = _trunk(x_sample, mods[:, Bp:], past, params, tables,
                                               tm=x_sample.shape[0] * x_sample.shape[1])
    return (y_p, y_s, k_p, v_p, lf_p, C_p, n_p, m_p, k_s, v_s, lf_s, C_s, n_s, m_s)
```

```python
import functools
import math

import jax
import jax.numpy as jnp
from jax import lax
from jax.experimental import pallas as pl
from jax.experimental.pallas import tpu as pltpu

F32 = jnp.float32
BF16 = jnp.bfloat16
I32 = jnp.int32

RMS_EPS = 1e-6
LANES = 128
MXU_DIM = 256
VMEM_LIMIT = 52 * 1024 * 1024
PEER_TOPK = 16
PAGE_SIZE = 128
NEG_INF = float("-inf")


def _dot(a, b):
    return jnp.dot(a, b, preferred_element_type=F32)


def _dot_nt(a, b):
    return lax.dot_general(a, b, (((1,), (1,)), ((), ())), preferred_element_type=F32)


def _dot_tn(a, b):
    return lax.dot_general(a, b, (((0,), (0,)), ((), ())), preferred_element_type=F32)


def _split2(a):
    hi = a.astype(BF16)
    lo = (a - hi.astype(F32)).astype(BF16)
    return hi, lo


def _dot3(a, b):
    ah, al = _split2(a)
    bh, bl = _split2(b)
    return _dot(ah, bh) + _dot(ah, bl) + _dot(al, bh)


def _modnorm(x, gain, scale, shift):
    ms = jnp.mean(x * x, axis=-1, keepdims=True)
    return x * lax.rsqrt(ms + RMS_EPS) * gain * (1.0 + scale) + shift


def _log_sigmoid(z):
    return jnp.minimum(z, 0.0) - jnp.log1p(jnp.exp(-jnp.abs(z)))


def _params(*sem):
    return pltpu.CompilerParams(dimension_semantics=sem, vmem_limit_bytes=VMEM_LIMIT)


def _mod_spec(rows, d, tiles_per_group):
    return pl.BlockSpec((None, rows, d), lambda i: (i // tiles_per_group, 0, 0))


def _adaln_kernel(c_ref, w_ref, b_ref, o_ref):
    c = c_ref[...]
    o_ref[...] = _dot3(c * jax.nn.sigmoid(c), w_ref[...]) + b_ref[...]


def adaln(c, ada_w, ada_b, tn=1536):
    L, D, D6 = ada_w.shape
    R = c.shape[0]
    return pl.pallas_call(
        _adaln_kernel,
        grid=(L, D6 // tn),
        in_specs=[pl.BlockSpec((R, D), lambda l, j: (0, 0)),
                  pl.BlockSpec((None, D, tn), lambda l, j: (l, 0, j)),
                  pl.BlockSpec((None, 1, tn), lambda l, j: (l, 0, j))],
        out_specs=pl.BlockSpec((None, R, tn), lambda l, j: (l, 0, j)),
        out_shape=jax.ShapeDtypeStruct((L, R, D6), F32),
        compiler_params=_params("arbitrary", "arbitrary"),
        name="adaln",
    )(c, ada_w, ada_b.reshape(L, 1, D6))


def _segcumsum_kernel(x_ref, o_ref, *, seg):
    C, T = x_ref.shape
    lane = lax.broadcasted_iota(I32, (C, LANES), 1)
    w = min(seg, LANES)
    carry = None
    for blk in range(T // LANES):
        x = x_ref[:, blk * LANES:(blk + 1) * LANES]
        k = 1
        while k < w:
            x = x + jnp.where((lane & (w - 1)) >= k, pltpu.roll(x, k, 1), 0.0)
            k *= 2
        if seg > LANES:
            if blk % (seg // LANES) != 0:
                x = x + carry
            carry = x[:, LANES - 1:LANES]
        o_ref[:, blk * LANES:(blk + 1) * LANES] = x


def seg_cumsum(x, seg):
    R, C, T = x.shape
    assert T % LANES == 0 and (seg & (seg - 1)) == 0
    assert T % seg == 0 and (seg <= LANES or seg % LANES == 0)
    return pl.pallas_call(
        functools.partial(_segcumsum_kernel, seg=seg),
        grid=(R,),
        in_specs=[pl.BlockSpec((None, C, T), lambda r: (r, 0, 0))],
        out_specs=pl.BlockSpec((None, C, T), lambda r: (r, 0, 0)),
        out_shape=jax.ShapeDtypeStruct((R, C, T), F32),
        compiler_params=_params("arbitrary"),
        name="seg_cumsum",
    )(x)


def _pair_headnorm(blk, gain2, hd):
    lane = lax.broadcasted_iota(I32, blk.shape, 1)
    lo = lane < hd
    sq = blk * blk
    s_lo = jnp.sum(jnp.where(lo, sq, 0.0), axis=-1, keepdims=True)
    s_hi = jnp.sum(jnp.where(lo, 0.0, sq), axis=-1, keepdims=True)
    inv = lax.rsqrt(jnp.where(lo, s_lo, s_hi) * (1.0 / hd) + RMS_EPS)
    return blk * inv * gain2


def _fox_proj_kernel(x_ref, sc_ref, sh_ref, gain_ref, w_ref, wfh_ref, wfl_ref, bf_ref, qg_ref, kg_ref,
                     q_ref, k_ref, v_ref, kb_ref, vb_ref, sg_ref, lf_ref, *, n_heads):
    D = x_ref.shape[1]
    hd = D // n_heads
    h = _modnorm(x_ref[...], gain_ref[...], sc_ref[...], sh_ref[...])
    hh, hl = _split2(h)
    qk_scale = hd ** -0.5
    q = _dot(hh, w_ref[:, 0:D])
    for c in range(D // LANES):
        sl = slice(c * LANES, (c + 1) * LANES)
        q_ref[:, sl] = (_pair_headnorm(q[:, sl], qg_ref[...], hd) * qk_scale).astype(BF16)
    k = _dot(hh, w_ref[:, D:2 * D])
    for c in range(D // LANES):
        sl = slice(c * LANES, (c + 1) * LANES)
        kn = _pair_headnorm(k[:, sl], kg_ref[...], hd)
        k_ref[:, sl] = kn
        kb_ref[:, sl] = kn.astype(BF16)
    v = _dot(hh, w_ref[:, 2 * D:3 * D])
    v_ref[...] = v
    vb_ref[...] = v.astype(BF16)
    sg_ref[...] = jax.nn.sigmoid(_dot(hh, w_ref[:, 3 * D:4 * D]))
    f = _dot(hh, wfh_ref[...]) + _dot(hh, wfl_ref[...]) + _dot(hl, wfh_ref[...])
    lf_ref[...] = _log_sigmoid(f[:, 0:n_heads] + bf_ref[...])


def fox_project(x, scale, shift, gain, w_in, b_f, q_gain, k_gain, *, tm, tiles_per_group):
    N, D = x.shape
    H = b_f.shape[0]
    hd = D // H
    assert 2 * hd == LANES
    R = scale.shape[1]
    w_main = jnp.concatenate([w_in[:, :3 * D], w_in[:, 3 * D + H:]], axis=1).astype(BF16)
    w_f = jnp.pad(w_in[:, 3 * D:3 * D + H], ((0, 0), (0, LANES - H)))
    wfh = w_f.astype(BF16)
    wfl = (w_f - wfh.astype(F32)).astype(BF16)
    const = lambda i: (0, 0)
    row = lambda i: (i, 0)
    outs = pl.pallas_call(
        functools.partial(_fox_proj_kernel, n_heads=H),
        grid=(N // tm,),
        in_specs=[pl.BlockSpec((tm, D), row),
                  _mod_spec(R, D, tiles_per_group), _mod_spec(R, D, tiles_per_group),
                  pl.BlockSpec((1, D), const),
                  pl.BlockSpec((D, 4 * D), const),
                  pl.BlockSpec((D, LANES), const), pl.BlockSpec((D, LANES), const),
                  pl.BlockSpec((1, H), const),
                  pl.BlockSpec((1, LANES), const), pl.BlockSpec((1, LANES), const)],
        out_specs=[pl.BlockSpec((tm, D), row)] * 6 + [pl.BlockSpec((tm, H), row)],
        out_shape=[jax.ShapeDtypeStruct((N, D), BF16),
                   jax.ShapeDtypeStruct((N, D), F32), jax.ShapeDtypeStruct((N, D), F32),
                   jax.ShapeDtypeStruct((N, D), BF16), jax.ShapeDtypeStruct((N, D), BF16),
                   jax.ShapeDtypeStruct((N, D), F32),
                   jax.ShapeDtypeStruct((N, H), F32)],
        compiler_params=_params("arbitrary"),
        name="fox_project",
    )(x, scale, shift, gain.reshape(1, D), w_main, wfh, wfl, b_f.reshape(1, H),
      jnp.tile(q_gain, 2).reshape(1, LANES), jnp.tile(k_gain, 2).reshape(1, LANES))
    return outs


def _fox_attn_kernel(q_ref, k_ref, v_ref, fc_ref, fr_ref, o_ref, *, hd, n_sub, hps):
    TQ = q_ref.shape[0]
    SQ = TQ // n_sub
    qi = pl.program_id(2)
    lane = lax.broadcasted_iota(I32, (SQ, LANES), 1)
    row = lax.broadcasted_iota(I32, (SQ, TQ), 0)
    col = lax.broadcasted_iota(I32, (SQ, TQ), 1)
    chains = [(hh, sb) for hh in range(hps) for sb in range(n_sub)]
    nch = len(chains)
    qms, fqs = [], []
    for hh, sb in chains:
        pb, ph = hh // 2, hh % 2
        q = q_ref[sb * SQ:(sb + 1) * SQ, pb * LANES:(pb + 1) * LANES]
        qms.append(jnp.where((lane >= ph * hd) & (lane < (ph + 1) * hd), q, jnp.zeros_like(q)))
        fqs.append(fc_ref[sb * SQ:(sb + 1) * SQ, hh:hh + 1])

    def block_scores(j):
        start = pl.multiple_of(j * TQ, TQ)
        kjs = [k_ref[pl.ds(start, TQ), pb * LANES:(pb + 1) * LANES] for pb in range(hps // 2)]
        return tuple(_dot_nt(qms[c], kjs[hh // 2]) + (fqs[c] - fr_ref[hh:hh + 1, pl.ds(start, TQ)])
                     for c, (hh, sb) in enumerate(chains))

    def update(j, state, scores, masked):
        start = pl.multiple_of(j * TQ, TQ)
        vjs = [v_ref[pl.ds(start, TQ), pb * LANES:(pb + 1) * LANES] for pb in range(hps // 2)]
        ss = [jnp.where(col <= row + sb * SQ, scores[c], NEG_INF) if masked else scores[c]
              for c, (hh, sb) in enumerate(chains)]
        m_new = [jnp.maximum(state[c][0], jnp.max(ss[c], axis=-1, keepdims=True)) for c in range(nch)]
        ps = [jnp.exp(ss[c] - m_new[c]) for c in range(nch)]
        alphas = [jnp.exp(state[c][0] - m_new[c]) for c in range(nch)]
        pvs = [_dot(ps[c].astype(BF16), vjs[hh // 2]) for c, (hh, sb) in enumerate(chains)]
        return tuple((m_new[c],
                      alphas[c] * state[c][1] + jnp.sum(ps[c], axis=-1, keepdims=True),
                      alphas[c] * state[c][2] + pvs[c]) for c in range(nch))

    def step(j, carry):
        state, scores = carry
        nxt = block_scores(j + 1)
        return update(j, state, scores, False), nxt

    init = tuple((jnp.full((SQ, 1), NEG_INF, F32), jnp.zeros((SQ, 1), F32), jnp.zeros((SQ, LANES), F32))
                 for _ in chains)
    state, scores = lax.fori_loop(0, qi, step, (init, block_scores(0)))
    final = update(qi, state, scores, True)
    for pb in range(hps // 2):
        for sb in range(n_sub):
            even = final[(2 * pb) * n_sub + sb]
            odd = final[(2 * pb + 1) * n_sub + sb]
            o_ref[sb * SQ:(sb + 1) * SQ, pb * LANES:(pb + 1) * LANES] = jnp.where(
                lane < hd, even[2] / even[1], odd[2] / odd[1])


def fox_attend_prompt(q, kb, vb, F, *, n_heads, tq=256, hps=2, n_sub=2):
    B, S, D = q.shape
    hd = D // n_heads
    assert 2 * hd == LANES and hps % 2 == 0 and n_heads % hps == 0
    HG = n_heads // hps
    W = hps * hd
    fcol = F.reshape(B, S, HG, hps).transpose(0, 2, 1, 3)
    frow = fcol.transpose(0, 1, 3, 2)
    return pl.pallas_call(
        functools.partial(_fox_attn_kernel, hd=hd, n_sub=n_sub, hps=hps),
        grid=(B, HG, S // tq),
        in_specs=[pl.BlockSpec((None, tq, W), lambda b, h, i: (b, i, h)),
                  pl.BlockSpec((None, S, W), lambda b, h, i: (b, 0, h)),
                  pl.BlockSpec((None, S, W), lambda b, h, i: (b, 0, h)),
                  pl.BlockSpec((None, None, tq, hps), lambda b, h, i: (b, h, i, 0)),
                  pl.BlockSpec((None, None, hps, S), lambda b, h, i: (b, h, 0, 0))],
        out_specs=pl.BlockSpec((None, tq, W), lambda b, h, i: (b, i, h)),
        out_shape=jax.ShapeDtypeStruct((B, S, D), F32),
        compiler_params=_params("arbitrary", "arbitrary", "arbitrary"),
        name="fox_attend_prompt",
    )(q, kb, vb, fcol, frow)


def _fox_decode_kernel(pt_ref, qbd_ref, *refs, n_heads, pp):
    del pt_ref
    kt_refs, vt_refs, cf_refs = refs[:pp], refs[pp:2 * pp], refs[2 * pp:3 * pp]
    kn_ref, vn_ref, cn_ref, o_ref, m_scr, l_scr, acc_scr, carry_scr = refs[3 * pp:]
    H = n_heads
    step = pl.program_id(1)
    n_steps = pl.num_programs(1)
    R, D = qbd_ref.shape
    T = R // H
    P = cf_refs[0].shape[1]

    @pl.when(step == 0)
    def _():
        m_scr[...] = jnp.full(m_scr.shape, NEG_INF, F32)
        l_scr[...] = jnp.zeros(l_scr.shape, F32)
        acc_scr[...] = jnp.zeros(acc_scr.shape, F32)
        carry_scr[...] = jnp.zeros(carry_scr.shape, F32)

    qbd = qbd_ref[...]

    def merge(m_i, l_i, acc_i):
        m = m_scr[...]
        m_new = jnp.maximum(m, m_i)
        a_old = jnp.exp(m - m_new)
        a_new = jnp.exp(m_i - m_new)
        l_scr[...] = a_old * l_scr[...] + a_new * l_i
        acc_scr[...] = a_old * acc_scr[...] + a_new * acc_i
        m_scr[...] = m_new

    def scores(s, fk):
        return s - jnp.concatenate([fk] * T, axis=0)

    carry = carry_scr[...]
    ss = []
    for i in range(pp):
        cf = cf_refs[i][...]
        ss.append(scores(_dot(qbd, kt_refs[i][...].astype(BF16)), cf + carry))
        carry = carry + cf[:, P - 1:P]
    carry_scr[...] = carry
    m_s = jnp.max(ss[0], axis=-1, keepdims=True)
    for s in ss[1:]:
        m_s = jnp.maximum(m_s, jnp.max(s, axis=-1, keepdims=True))
    l_s = jnp.zeros_like(m_s)
    acc_s = jnp.zeros(acc_scr.shape, F32)
    for i, s in enumerate(ss):
        pr = jnp.exp(s - m_s)
        l_s = l_s + jnp.sum(pr, axis=-1, keepdims=True)
        acc_s = acc_s + _dot_nt(pr.astype(BF16), vt_refs[i][...].astype(BF16))
    merge(m_s, l_s, acc_s)

    @pl.when(step == n_steps - 1)
    def _():
        s2 = scores(_dot_nt(qbd, kn_ref[...]), cn_ref[...] + carry_scr[...])
        key = lax.broadcasted_iota(I32, s2.shape, 1)
        t = lax.broadcasted_iota(I32, s2.shape, 0) // H
        s2 = jnp.where(key <= t, s2, NEG_INF)
        m2 = jnp.max(s2, axis=-1, keepdims=True)
        p2 = jnp.exp(s2 - m2)
        merge(m2, jnp.sum(p2, axis=-1, keepdims=True), _dot(p2.astype(BF16), vn_ref[...]))
        out = acc_scr[...] / l_scr[...]
        rh = lax.broadcasted_iota(I32, (R, D), 0) % H
        ch = lax.broadcasted_iota(I32, (R, D), 1) // (D // H)
        out = jnp.where(rh == ch, out, 0.0)
        o_ref[...] = jnp.sum(out.reshape(T, H, D), axis=1)


def fox_attend_sample(q, kb_new, vb_new, lf_new, cache_kt, cache_vt, cache_cf, page_table, j, *, n_heads):
    Bd, T, D = q.shape
    H = n_heads
    hd = D // H
    n_pages = page_table.shape[1]
    n_fox, n_pool, _, P = cache_cf.shape
    R = T * H
    pp = next(c for c in (8, 4, 2, 1) if n_pages % c == 0)
    assert T <= P
    q4 = q.reshape(Bd, T, H, hd)
    eye = jnp.eye(H, dtype=q.dtype)
    qbd = (q4[:, :, :, None, :] * eye[None, None, :, :, None]).reshape(Bd, R, D)
    pad = ((0, 0), (0, P - T), (0, 0))
    kn = jnp.pad(kb_new, pad)
    vn = jnp.pad(vb_new, pad)
    cn = seg_cumsum(jnp.pad(lf_new.transpose(0, 2, 1), ((0, 0), (0, 0), (0, P - T))), P)
    pt = page_table.reshape(-1).astype(I32)

    def page_spec(shape, i):
        return pl.BlockSpec((None, None) + shape, lambda b, s, pt: (j, pt[b * n_pages + s * pp + i], 0, 0))

    per_seq = lambda shape: pl.BlockSpec((None,) + shape, lambda b, s, pt: (b, 0, 0))
    grid_spec = pltpu.PrefetchScalarGridSpec(
        num_scalar_prefetch=1,
        grid=(Bd, n_pages // pp),
        in_specs=([per_seq((R, D))]
                  + [page_spec((D, P), i) for i in range(pp)]
                  + [page_spec((D, P), i) for i in range(pp)]
                  + [page_spec((H, P), i) for i in range(pp)]
                  + [per_seq((P, D)), per_seq((P, D)), per_seq((H, P))]),
        out_specs=per_seq((T, D)),
        scratch_shapes=[pltpu.VMEM((R, 1), F32), pltpu.VMEM((R, 1), F32),
                        pltpu.VMEM((R, D), F32), pltpu.VMEM((H, 1), F32)],
    )
    return pl.pallas_call(
        functools.partial(_fox_decode_kernel, n_heads=H, pp=pp),
        grid_spec=grid_spec,
        out_shape=jax.ShapeDtypeStruct((Bd, T, D), F32),
        compiler_params=_params("arbitrary", "arbitrary"),
        name="fox_attend_sample",
    )(pt, qbd, *([cache_kt] * pp), *([cache_vt] * pp), *([cache_cf] * pp), kn, vn, cn)


def prepare_cache(cache_k, cache_v, cache_logf):
    n_fox, n_pool, P, H, hd = cache_k.shape
    kt = cache_k.transpose(0, 1, 3, 4, 2).reshape(n_fox, n_pool, H * hd, P)
    vt = cache_v.transpose(0, 1, 3, 4, 2).reshape(n_fox, n_pool, H * hd, P)
    rows = n_fox * n_pool * H
    group = 2048 if rows % 2048 == 0 else H
    cf = seg_cumsum(cache_logf.transpose(0, 1, 3, 2).reshape(rows // group, group, P), P)
    return kt, vt, cf.reshape(n_fox, n_pool, H, P)


def _outproj_kernel(a_ref, gate_ref, w_ref, x_ref, g_ref, hn_ref, o_ref, *, head_norm):
    a = a_ref[...]
    D = a.shape[1]
    if head_norm:
        blocks = []
        for c in range(D // LANES):
            sl = slice(c * LANES, (c + 1) * LANES)
            blk = a[:, sl]
            ms = jnp.mean(blk * blk, axis=-1, keepdims=True)
            blocks.append(blk * lax.rsqrt(ms + RMS_EPS) * hn_ref[:, sl])
        a = jnp.concatenate(blocks, axis=-1)
    y = (a * gate_ref[...]).astype(BF16)
    o_ref[...] = x_ref[...] + g_ref[...] * _dot(y, w_ref[...])


def out_project(a, gate, w_out, x, g, head_gain, *, tm, tiles_per_group, head_norm):
    N, D = x.shape
    R = g.shape[1]
    row = lambda i: (i, 0)
    const = lambda i: (0, 0)
    return pl.pallas_call(
        functools.partial(_outproj_kernel, head_norm=head_norm),
        grid=(N // tm,),
        in_specs=[pl.BlockSpec((tm, D), row), pl.BlockSpec((tm, D), row),
                  pl.BlockSpec((D, D), const), pl.BlockSpec((tm, D), row),
                  _mod_spec(R, D, tiles_per_group), pl.BlockSpec((1, D), const)],
        out_specs=pl.BlockSpec((tm, D), row),
        out_shape=jax.ShapeDtypeStruct((N, D), F32),
        compiler_params=_params("arbitrary"),
        name="out_project",
    )(a, gate, w_out.astype(BF16), x, g, head_gain.reshape(1, D))


def _ml_proj_kernel(x_ref, sc_ref, sh_ref, gain_ref, w_ref, wgh_ref, wgl_ref, bi_ref, bf_ref,
                    q_ref, k_ref, v_ref, so_ref, il_ref, fl_ref, *, n_heads, qk_width):
    D = x_ref.shape[1]
    QW = qk_width
    h = _modnorm(x_ref[...], gain_ref[...], sc_ref[...], sh_ref[...])
    hh, hl = _split2(h)
    k_scale = (QW // n_heads) ** -0.5
    q_ref[...] = _dot(hh, w_ref[:, 0:QW]).astype(BF16)
    k_ref[...] = (_dot(hh, w_ref[:, QW:2 * QW]) * k_scale).astype(BF16)
    v_ref[...] = _dot(hh, w_ref[:, 2 * QW:2 * QW + D])
    so_ref[...] = jax.nn.sigmoid(_dot(hh, w_ref[:, 2 * QW + D:2 * QW + 2 * D]))
    gts = _dot(hh, wgh_ref[...]) + _dot(hh, wgl_ref[...]) + _dot(hl, wgh_ref[...])
    il_ref[...] = gts[:, 0:n_heads] + bi_ref[...]
    fl_ref[...] = _log_sigmoid(gts[:, n_heads:2 * n_heads] + bf_ref[...])


def ml_project(x, scale, shift, gain, w_in, b_i, b_f, *, tm, tiles_per_group, v_width):
    N, D = x.shape
    H = b_i.shape[0]
    QW = (w_in.shape[1] - 2 * v_width - 2 * H) // 2
    R = scale.shape[1]
    g0 = 2 * QW + v_width
    w_main = jnp.concatenate([w_in[:, :g0], w_in[:, g0 + 2 * H:]], axis=1).astype(BF16)
    w_g = jnp.pad(w_in[:, g0:g0 + 2 * H], ((0, 0), (0, LANES - 2 * H)))
    wgh = w_g.astype(BF16)
    wgl = (w_g - wgh.astype(F32)).astype(BF16)
    const = lambda i: (0, 0)
    row = lambda i: (i, 0)
    WM = w_main.shape[1]
    return pl.pallas_call(
        functools.partial(_ml_proj_kernel, n_heads=H, qk_width=QW),
        grid=(N // tm,),
        in_specs=[pl.BlockSpec((tm, D), row),
                  _mod_spec(R, D, tiles_per_group), _mod_spec(R, D, tiles_per_group),
                  pl.BlockSpec((1, D), const),
                  pl.BlockSpec((D, WM), const),
                  pl.BlockSpec((D, LANES), const), pl.BlockSpec((D, LANES), const),
                  pl.BlockSpec((1, H), const), pl.BlockSpec((1, H), const)],
        out_specs=[pl.BlockSpec((tm, QW), row), pl.BlockSpec((tm, QW), row),
                   pl.BlockSpec((tm, v_width), row), pl.BlockSpec((tm, v_width), row),
                   pl.BlockSpec((tm, H), row), pl.BlockSpec((tm, H), row)],
        out_shape=[jax.ShapeDtypeStruct((N, QW), BF16), jax.ShapeDtypeStruct((N, QW), BF16),
                   jax.ShapeDtypeStruct((N, v_width), F32), jax.ShapeDtypeStruct((N, v_width), F32),
                   jax.ShapeDtypeStruct((N, H), F32), jax.ShapeDtypeStruct((N, H), F32)],
        compiler_params=_params("arbitrary"),
        name="ml_project",
    )(x, scale, shift, gain.reshape(1, D), w_main, wgh, wgl, b_i.reshape(1, H), b_f.reshape(1, H))


def _mlstm_kernel(q_ref, k_ref, v_ref, ic_ref, ir_ref, bc_ref, br_ref, c0_ref, n0_ref, m0_ref,
                  h_ref, c_ref, n_ref, m_ref, *, n_heads, dk):
    nbb, L = q_ref.shape[0], q_ref.shape[1]
    ci = pl.program_id(1)

    @pl.when(ci == 0)
    def _():
        c_ref[...] = c0_ref[...]
        n_ref[...] = n0_ref[...]
        m_ref[...] = m0_ref[...]

    tri = lax.broadcasted_iota(I32, (L, L), 1) <= lax.broadcasted_iota(I32, (L, L), 0)
    lane = lax.broadcasted_iota(I32, (L, LANES), 1)
    lane1 = lax.broadcasted_iota(I32, (1, LANES), 1)
    rsub = lax.broadcasted_iota(I32, (LANES, LANES), 0)
    dv = LANES
    m_in = [m_ref[bb] for bb in range(nbb)]
    units = [(bb, h) for bb in range(nbb) for h in range(n_heads)]
    pairs = [(bb, pr) for bb in range(nbb) for pr in range(n_heads // 2)]
    qp = {(bb, pr): q_ref[bb, :, pr * LANES:(pr + 1) * LANES] for bb, pr in pairs}
    kp = {(bb, pr): k_ref[bb, :, pr * LANES:(pr + 1) * LANES] for bb, pr in pairs}
    c_in = {u: c_ref[u[0], u[1]] for u in pairs}
    n_in = {u: n_ref[u[0], u[1]] for u in pairs}
    sel = [(lane >= hh * dk) & (lane < (hh + 1) * dk) for hh in range(2)]
    qm = {(bb, h): jnp.where(sel[h % 2], qp[(bb, h // 2)], jnp.zeros_like(qp[(bb, h // 2)])) for bb, h in units}
    qk = {(bb, h): _dot_nt(qm[(bb, h)], kp[(bb, h // 2)]) for bb, h in units}
    qc = {(bb, h): _dot(qm[(bb, h)], c_in[(bb, h // 2)].astype(BF16)) for bb, h in units}
    st = {}
    for bb, h in units:
        b_l = bc_ref[bb][:, h:h + 1]
        b_s = br_ref[bb][h:h + 1, :]
        ig_s = ir_ref[bb][h:h + 1, :]
        ig_l = ic_ref[bb][:, h:h + 1]
        m = m_in[bb][:, h:h + 1]
        dm = jnp.where(tri, b_l - b_s + ig_s, NEG_INF)
        inter = b_l + m
        m_t = jnp.maximum(inter, jnp.max(dm, axis=-1, keepdims=True))
        w_inter = jnp.exp(inter - m_t)
        a = jnp.exp(dm - m_t) * qk[(bb, h)]
        b_last = b_l[L - 1:L, :]
        m_new = jnp.maximum(b_last + m, jnp.max(b_last - b_s + ig_s, axis=-1, keepdims=True))
        wk = jnp.exp(b_last - b_l + ig_l - m_new)
        decay = jnp.exp(b_last + m - m_new)
        vh = v_ref[bb, :, h * dv:(h + 1) * dv]
        st[(bb, h)] = (a, m_t, w_inter, m_new, wk, decay, vh)
    av = {u: _dot(st[u][0].astype(BF16), st[u][6].astype(BF16)) for u in units}
    upd = {(bb, h): _dot_tn(kp[(bb, h // 2)], (st[(bb, h)][4] * st[(bb, h)][6]).astype(BF16)) for bb, h in units}
    for bb, pr in pairs:
        c_pair, n_pair = c_in[(bb, pr)], n_in[(bb, pr)]
        qpf = qp[(bb, pr)].astype(F32)
        kpf = kp[(bb, pr)].astype(F32)
        new_c, new_n = [], []
        for hh in range(2):
            h = 2 * pr + hh
            a, m_t, w_inter, m_new, wk, decay, vh = st[(bb, h)]
            num = w_inter * qc[(bb, h)] + av[(bb, h)]
            qn = jnp.sum(jnp.where(sel[hh], qpf, 0.0) * n_pair, axis=-1, keepdims=True)
            den = w_inter * qn + jnp.sum(a, axis=-1, keepdims=True)
            h_ref[bb, :, h * dv:(h + 1) * dv] = num / jnp.maximum(jnp.abs(den), jnp.exp(-m_t))
            new_c.append(decay * c_pair + upd[(bb, h)])
            new_n.append(decay * n_pair + jnp.sum(wk * kpf, axis=0, keepdims=True))
            m_ref[bb, :, h:h + 1] = m_new
        c_ref[bb, pr] = jnp.where(rsub < dk, new_c[0], new_c[1])
        n_ref[bb, pr] = jnp.where(lane1 < dk, new_n[0], new_n[1])


def mlstm_chunked(q, k, v, i_log, f_log, C0, n0, m0, chunk):
    B, T, _ = q.shape
    H = i_log.shape[-1]
    dk = C0.shape[2]
    dv = C0.shape[3]
    assert 2 * dk == LANES and dv == LANES
    nc = T // chunk
    L = chunk
    Tp = -(-T // LANES) * LANES
    fT = jnp.pad(f_log.transpose(0, 2, 1), ((0, 0), (0, 0), (0, Tp - T)))
    bT = seg_cumsum(fT, chunk)[:, :, :T]
    br = bT.reshape(B, H, nc, L).transpose(0, 2, 1, 3)
    bc = br.transpose(0, 1, 3, 2)
    ic = i_log.reshape(B, nc, L, H)
    ir = ic.transpose(0, 1, 3, 2)
    HP = H // 2
    c0 = C0.reshape(B, HP, 2 * dk, dv)
    n0p = n0.reshape(B, HP, 1, 2 * dk)
    m0p = m0.reshape(B, 1, H)
    QW = H * dk
    VW = H * dv
    nbb = 2 if B % 2 == 0 else 1
    col_spec = pl.BlockSpec((nbb, None, L, H), lambda b, c: (b, c, 0, 0))
    row_spec = pl.BlockSpec((nbb, None, H, L), lambda b, c: (b, c, 0, 0))
    st = lambda b, c: (b, 0, 0, 0)
    h, Cn, nn, mn = pl.pallas_call(
        functools.partial(_mlstm_kernel, n_heads=H, dk=dk),
        grid=(B // nbb, nc),
        in_specs=[pl.BlockSpec((nbb, L, QW), lambda b, c: (b, c, 0)),
                  pl.BlockSpec((nbb, L, QW), lambda b, c: (b, c, 0)),
                  pl.BlockSpec((nbb, L, VW), lambda b, c: (b, c, 0)),
                  col_spec, row_spec, col_spec, row_spec,
                  pl.BlockSpec((nbb, HP, 2 * dk, dv), st),
                  pl.BlockSpec((nbb, HP, 1, 2 * dk), st),
                  pl.BlockSpec((nbb, 1, H), lambda b, c: (b, 0, 0))],
        out_specs=[pl.BlockSpec((nbb, L, VW), lambda b, c: (b, c, 0)),
                   pl.BlockSpec((nbb, HP, 2 * dk, dv), st),
                   pl.BlockSpec((nbb, HP, 1, 2 * dk), st),
                   pl.BlockSpec((nbb, 1, H), lambda b, c: (b, 0, 0))],
        out_shape=[jax.ShapeDtypeStruct((B, T, VW), F32),
                   jax.ShapeDtypeStruct((B, HP, 2 * dk, dv), F32),
                   jax.ShapeDtypeStruct((B, HP, 1, 2 * dk), F32),
                   jax.ShapeDtypeStruct((B, 1, H), F32)],
        compiler_params=_params("arbitrary", "arbitrary"),
        name="mlstm_chunked",
    )(q, k, v, ic, ir, bc, br, c0, n0p, m0p)
    return h, Cn.reshape(B, H, dk, dv), nn.reshape(B, H, dk), mn.reshape(B, H)


def _staircase(n):
    return [(a, n // (a + 1)) for a in range(n)]


def _extract_topk(ss, payloads, n):
    ss = list(ss)
    R = ss[0].shape[0]
    ridx = lax.broadcasted_iota(I32, ss[0].shape, 0).astype(F32)
    vals = [[] for _ in ss]
    pays = [[] for _ in ss]
    for it in range(n):
        for c in range(len(ss)):
            s = ss[c]
            m = jnp.max(s, axis=0, keepdims=True)
            am = jnp.min(jnp.where(s == m, ridx, float(R)), axis=0, keepdims=True)
            hit = ridx == am
            if payloads[c] is None:
                pays[c].append(am)
            else:
                pays[c].append(jnp.max(jnp.where(hit, payloads[c], -1.0), axis=0, keepdims=True))
            vals[c].append(m)
            if it + 1 < n:
                ss[c] = jnp.where(hit, NEG_INF, s)
    return vals, pays


def _peer_topk_kernel(x_ref, sc_ref, sh_ref, gain_ref, wq_ref, k1_ref, k2_ref,
                      hb_ref, e_ref, g_ref, q_scr, v1_scr, i1_scr, v2_scr, i2_scr, *, n_heads, n_keys):
    TM = x_ref.shape[0]
    K = PEER_TOPK
    half = k1_ref.shape[2]
    hb = _modnorm(x_ref[...], gain_ref[...], sc_ref[...], sh_ref[...]).astype(BF16)
    hb_ref[...] = hb
    q_scr[...] = _dot(hb, wq_ref[...]).astype(BF16)
    sub8 = lax.broadcasted_iota(I32, (8, LANES), 0)

    def head_body(hd, _):
        c0 = pl.multiple_of(hd * 2 * half, 2 * half)
        c1 = pl.multiple_of(hd * 2 * half + half, half)
        s1 = _dot_nt(k1_ref[hd], q_scr[:, pl.ds(c0, half)])
        s2 = _dot_nt(k2_ref[hd], q_scr[:, pl.ds(c1, half)])
        NB = TM // LANES
        blocks = [slice(blk * LANES, (blk + 1) * LANES) for blk in range(NB)]
        vals, idxs = _extract_topk([s[:, ls] for ls in blocks for s in (s1, s2)], [None] * (2 * NB), K)
        for blk in range(NB):
            for side, (v_scr, i_scr) in enumerate(((v1_scr, i1_scr), (v2_scr, i2_scr))):
                for r in range(K):
                    v_scr[blk, r:r + 1, :] = vals[2 * blk + side][r]
                    i_scr[blk, r:r + 1, :] = idxs[2 * blk + side][r]
        cands, pays = [], []
        for blk in range(NB):
            cand, pay = [], []
            for a, cnt in _staircase(K):
                if cnt == 1:
                    break
                for b0 in range(0, cnt, 8):
                    nb = min(8, cnt - b0)
                    cv = v1_scr[blk, a:a + 1, :] + v2_scr[blk, b0:b0 + 8, :]
                    pv = i1_scr[blk, a:a + 1, :] * n_keys + i2_scr[blk, b0:b0 + 8, :]
                    if nb < 8:
                        cv = jnp.where(sub8 < nb, cv, NEG_INF)
                    cand.append(cv)
                    pay.append(pv)
            a1 = K // 2
            cand.append(v1_scr[blk, a1:K, :] + v2_scr[blk, 0:1, :])
            pay.append(i1_scr[blk, a1:K, :] * n_keys + i2_scr[blk, 0:1, :])
            cands.append(jnp.concatenate(cand, axis=0))
            pays.append(jnp.concatenate(pay, axis=0))
        tops, topes = _extract_topk(cands, pays, K)
        out_rows = pl.ds(pl.multiple_of(hd * K, K), K)
        for blk in range(NB):
            top_s, top_e = tops[blk], topes[blk]
            ex = [jnp.exp(t - top_s[0]) for t in top_s]
            den = ex[0]
            for t in ex[1:]:
                den = den + t
            inv = 1.0 / den
            for r in range(K):
                i1_scr[blk, r:r + 1, :] = top_e[r]
                v1_scr[blk, r:r + 1, :] = ex[r] * inv
            e_ref[out_rows, blocks[blk]] = i1_scr[blk].astype(I32)
            g_ref[out_rows, blocks[blk]] = v1_scr[blk]
        return 0

    lax.fori_loop(0, n_heads, head_body, 0)


def peer_retrieve(x, scale, shift, gain, w_q, sub_k1, sub_k2, *, tm, tiles_per_group):
    N, D = x.shape
    HP, NK, half = sub_k1.shape
    assert NK == LANES and half == LANES and PEER_TOPK % 8 == 0
    R = scale.shape[1]
    QW = w_q.shape[1]
    J = HP * PEER_TOPK
    const = lambda i: (0, 0)
    const3 = lambda i: (0, 0, 0)
    kspec = pl.BlockSpec((HP, NK, half), const3)
    return pl.pallas_call(
        functools.partial(_peer_topk_kernel, n_heads=HP, n_keys=NK),
        grid=(N // tm,),
        in_specs=[pl.BlockSpec((tm, D), lambda i: (i, 0)),
                  _mod_spec(R, D, tiles_per_group), _mod_spec(R, D, tiles_per_group),
                  pl.BlockSpec((1, D), const),
                  pl.BlockSpec((D, QW), const),
                  kspec, kspec],
        out_specs=[pl.BlockSpec((tm, D), lambda i: (i, 0)),
                   pl.BlockSpec((J, tm), lambda i: (0, i)),
                   pl.BlockSpec((J, tm), lambda i: (0, i))],
        out_shape=[jax.ShapeDtypeStruct((N, D), BF16),
                   jax.ShapeDtypeStruct((J, N), I32),
                   jax.ShapeDtypeStruct((J, N), F32)],
        scratch_shapes=[pltpu.VMEM((tm, QW), BF16),
                        ] + [pltpu.VMEM((tm // LANES, PEER_TOPK, LANES), F32)] * 4,
        compiler_params=_params("arbitrary"),
        name="peer_retrieve",
    )(x, scale, shift, gain.reshape(1, D), w_q.astype(BF16), sub_k1.astype(BF16), sub_k2.astype(BF16))


def _peer_mix_kernel(hb_ref, i1_ref, i2_ref, g_ref, ut_ref, v_ref, x_ref, g2_ref, o_ref,
                     a_scr, w_scr, hbuf_scr, y_scr, *, n_keys, tg):
    TM = hb_ref.shape[0]
    TE = ut_ref.shape[1]
    J = i1_ref.shape[1]
    NE = pl.num_programs(1) // 2
    j = pl.program_id(1)
    NP2 = n_keys // 2
    W2 = 2 * n_keys

    @pl.when(j == 0)
    def _():
        a_scr[...] = jnp.zeros(a_scr.shape, F32)

    @pl.when(j < NE)
    def _():
        act = _dot(hb_ref[...], ut_ref[...])
        i1 = i1_ref[...]
        i2 = i2_ref[...]
        acc = a_scr[...]
        for c in range(TE // n_keys):
            r = j * (TE // n_keys) + c
            picked = jnp.take_along_axis(act[:, c * n_keys:(c + 1) * n_keys], i2, axis=1)
            acc = jnp.where(i1 == r, picked, acc)
        a_scr[...] = acc

    @pl.when(j == NE)
    def _():
        a = a_scr[...]
        gelu = 0.5 * a * (1.0 + lax.erf(a * math.sqrt(0.5)))
        w_scr[...] = g_ref[...] * gelu
        y_scr[...] = jnp.zeros(y_scr.shape, F32)
        rows = lax.broadcasted_iota(I32, (NP2, J), 0)
        cols = lax.broadcasted_iota(I32, (W2, J), 0)

        def one_group(t0):
            grids = []
            for u in range(tg):
                i1 = i1_ref[pl.ds(t0 + u, 1), :]
                i2 = i2_ref[pl.ds(t0 + u, 1), :]
                w = w_scr[pl.ds(t0 + u, 1), :]
                p1 = jnp.where(rows == (i1 >> 1), w, 0.0).astype(BF16)
                p2 = jnp.where(cols == ((i1 & 1) * n_keys + i2), 1.0, 0.0).astype(BF16)
                grids.append(_dot_nt(p1, p2).astype(BF16))
            hbuf_scr[:, pl.ds(t0, tg), :] = pltpu.einshape("tpc->ptc", jnp.stack(grids, axis=0))

        def group_body(gi, _):
            for u in range(group_unroll):
                one_group(pl.multiple_of((gi * group_unroll + u) * tg, tg))
            return 0

        group_unroll = 4 if TM % (4 * tg) == 0 else 1
        lax.fori_loop(0, TM // (tg * group_unroll), group_body, 0)

    @pl.when(j >= NE)
    def _():
        jj = j - NE
        y = y_scr[...]
        for c in range(TE // W2):
            y = y + _dot(hbuf_scr[jj * (TE // W2) + c], v_ref[c * W2:(c + 1) * W2, :])
        y_scr[...] = y

    @pl.when(j == 2 * NE - 1)
    def _():
        o_ref[...] = x_ref[...] + g2_ref[...] * y_scr[...]


def peer_mix(hb, i1, i2, g, ut_tab, v_tab, x, g2, *, tm, te, tiles_per_group, n_keys):
    N, D = x.shape
    E = v_tab.shape[0]
    J = i1.shape[1]
    NE = E // te
    R = g2.shape[1]
    tg = 16
    assert te % (2 * n_keys) == 0 and n_keys == LANES and tm % tg == 0
    return pl.pallas_call(
        functools.partial(_peer_mix_kernel, n_keys=n_keys, tg=tg),
        grid=(N // tm, 2 * NE),
        in_specs=[pl.BlockSpec((tm, D), lambda i, j: (i, 0)),
                  pl.BlockSpec((tm, J), lambda i, j: (i, 0)),
                  pl.BlockSpec((tm, J), lambda i, j: (i, 0)),
                  pl.BlockSpec((tm, J), lambda i, j: (i, 0)),
                  pl.BlockSpec((D, te), lambda i, j: (0, jnp.minimum(j, NE - 1))),
                  pl.BlockSpec((te, D), lambda i, j: (jnp.maximum(j - NE, 0), 0)),
                  pl.BlockSpec((tm, D), lambda i, j: (i, 0)),
                  pl.BlockSpec((None, R, D), lambda i, j: (i // tiles_per_group, 0, 0))],
        out_specs=pl.BlockSpec((tm, D), lambda i, j: (i, 0)),
        out_shape=jax.ShapeDtypeStruct((N, D), F32),
        scratch_shapes=[pltpu.VMEM((tm, J), F32), pltpu.VMEM((tm, J), F32),
                        pltpu.VMEM((n_keys // 2, tm, 2 * n_keys), BF16),
                        pltpu.VMEM((tm, D), F32)],
        compiler_params=_params("arbitrary", "arbitrary"),
        name="peer_mix",
    )(hb, i1, i2, g, ut_tab, v_tab, x, g2)


def _trunk(x3, mods, past, params, tables, *, tm):
    (norm_mix, norm_ffn, fox_w_in, fox_b_f, fox_q_norm, fox_k_norm, fox_w_out,
     ml_w_in, ml_b_i, ml_b_f, ml_h_norm, ml_w_out, peer_w_q, peer_sub_k1, peer_sub_k2) = params
    ut_bf, v_bf = tables
    B, T, D = x3.shape
    N = B * T
    depth = norm_mix.shape[0]
    n_fox_heads = fox_b_f.shape[1]
    n_keys = peer_sub_k1.shape[2]
    x = x3.reshape(N, D)
    if T % tm == 0:
        tpg, rows = T // tm, 1
        expand = lambda a: a.reshape(B, 1, D)
    else:
        assert N == tm
        tpg, rows = 1, N
        expand = lambda a: jnp.repeat(a, T, axis=0).reshape(1, N, D)
    new_k, new_v, new_lf, new_C, new_n, new_m = [], [], [], [], [], []
    for layer in range(depth):
        sh1, sc1, g1, sh2, sc2, g2 = [expand(a) for a in jnp.split(mods[layer], 6, axis=-1)]
        j = layer // 2
        if layer % 2 == 0:
            q, k, v, kb, vb, sg, lf = fox_project(
                x, sc1, sh1, norm_mix[layer], fox_w_in[j], fox_b_f[j], fox_q_norm[j], fox_k_norm[j],
                tm=tm, tiles_per_group=tpg)
            if past is None:
                F = seg_cumsum(lf.reshape(B, T, n_fox_heads).transpose(0, 2, 1), T).transpose(0, 2, 1)
                o = fox_attend_prompt(q.reshape(B, T, D), kb.reshape(B, T, D), vb.reshape(B, T, D), F,
                                      n_heads=n_fox_heads)
            else:
                cache_kt, cache_vt, cache_cf, _, _, _, page_table = past
                o = fox_attend_sample(q.reshape(B, T, D), kb.reshape(B, T, D), vb.reshape(B, T, D),
                                      lf.reshape(B, T, n_fox_heads), cache_kt, cache_vt, cache_cf,
                                      page_table, j, n_heads=n_fox_heads)
            x = out_project(o.reshape(N, D), sg, fox_w_out[j], x, g1, jnp.ones((D,), F32),
                            tm=tm, tiles_per_group=tpg, head_norm=False)
            new_k.append(k.reshape(B, T, n_fox_heads, -1))
            new_v.append(v.reshape(B, T, n_fox_heads, -1))
            new_lf.append(lf.reshape(B, T, n_fox_heads))
        else:
            n_ml_heads = ml_b_i.shape[1]
            v_width = ml_w_out.shape[1]
            q, k, v, so, il, fl = ml_project(x, sc1, sh1, norm_mix[layer], ml_w_in[j], ml_b_i[j], ml_b_f[j],
                                             tm=tm, tiles_per_group=tpg, v_width=v_width)
            QW = q.shape[1]
            dk = QW // n_ml_heads
            dv = v_width // n_ml_heads
            q3, k3, v3 = q.reshape(B, T, QW), k.reshape(B, T, QW), v.reshape(B, T, v_width)
            il3, fl3 = il.reshape(B, T, n_ml_heads), fl.reshape(B, T, n_ml_heads)
            if past is None:
                C0 = jnp.zeros((B, n_ml_heads, dk, dv), F32)
                n0 = jnp.zeros((B, n_ml_heads, dk), F32)
                m0 = jnp.zeros((B, n_ml_heads), F32)
                hh, C, n_, m_ = mlstm_chunked(q3, k3, v3, il3, fl3, C0, n0, m0, 64)
            else:
                _, _, _, state_C, state_n, state_m, _ = past
                Tp = 16
                pad3 = ((0, 0), (0, Tp - T), (0, 0))
                hh, C, n_, m_ = mlstm_chunked(
                    jnp.pad(q3, pad3), jnp.pad(k3, pad3), jnp.pad(v3, pad3),
                    jnp.pad(il3, pad3, constant_values=-1e30), jnp.pad(fl3, pad3),
                    state_C[j], state_n[j], state_m[j], Tp)
                hh = hh[:, :T]
            x = out_project(hh.reshape(N, v_width), so, ml_w_out[j], x, g1, ml_h_norm[j].reshape(-1),
                            tm=tm, tiles_per_group=tpg, head_norm=True)
            new_C.append(C)
            new_n.append(n_)
            new_m.append(m_)
        hb, e, g = peer_retrieve(x, sc2, sh2, norm_ffn[layer], peer_w_q[layer], peer_sub_k1[layer],
                                 peer_sub_k2[layer], tm=tm, tiles_per_group=tpg)
        eT = e.T
        mix_tm = 2 * tm if tpg % 2 == 0 else tm
        x = peer_mix(hb, eT // n_keys, eT % n_keys, g.T, ut_bf[layer], v_bf[layer], x, g2,
                     tm=mix_tm, te=2048, tiles_per_group=tpg * tm // mix_tm, n_keys=n_keys)
    return (x.reshape(B, T, D), jnp.stack(new_k), jnp.stack(new_v), jnp.stack(new_lf),
            jnp.stack(new_C), jnp.stack(new_n), jnp.stack(new_m))


def kernel(x_prompt, x_sample, cache_k, cache_v, cache_logf, state_C, state_n, state_m, page_table,
           c_prompt, c_sample, ada_w, ada_b, norm_mix, norm_ffn, fox_w_in, fox_b_f, fox_q_norm, fox_k_norm,
           fox_w_out, ml_w_in, ml_b_i, ml_b_f, ml_h_norm, ml_w_out, peer_w_q, peer_sub_k1, peer_sub_k2,
           peer_u, peer_v):
    params = (norm_mix, norm_ffn, fox_w_in, fox_b_f, fox_q_norm, fox_k_norm, fox_w_out,
              ml_w_in, ml_b_i, ml_b_f, ml_h_norm, ml_w_out, peer_w_q, peer_sub_k1, peer_sub_k2)
    tables = (peer_u.astype(BF16).transpose(0, 2, 1), peer_v.astype(BF16))
    Bp = c_prompt.shape[0]
    mods = adaln(jnp.concatenate([c_prompt, c_sample], axis=0), ada_w, ada_b)
    y_p, k_p, v_p, lf_p, C_p, n_p, m_p = _trunk(x_prompt, mods[:, :Bp], None, params, tables, tm=256)
    past = prepare_cache(cache_k, cache_v, cache_logf) + (state_C, state_n, state_m, page_table)
    y_s, k_s, v_s, lf_s, C_s, n_s, m_s = _trunk(x_sample, mods[:, Bp:], past, params, tables,
                                               tm=x_sample.shape[0] * x_sample.shape[1])
    return (y_p, y_s, k_p, v_p, lf_p, C_p, n_p, m_p, k_s, v_s, lf_s, C_s, n_s, m_s)
```

```python
import functools
import math

import jax
import jax.numpy as jnp
from jax import lax
from jax.experimental import pallas as pl
from jax.experimental.pallas import tpu as pltpu

F32 = jnp.float32
BF16 = jnp.bfloat16
I32 = jnp.int32

RMS_EPS = 1e-6
LANES = 128
MXU_DIM = 256
VMEM_LIMIT = 52 * 1024 * 1024
PEER_TOPK = 16
PAGE_SIZE = 128
NEG_INF = float("-inf")


def _dot(a, b):
    return jnp.dot(a, b, preferred_element_type=F32)


def _dot_nt(a, b):
    return lax.dot_general(a, b, (((1,), (1,)), ((), ())), preferred_element_type=F32)


def _dot_tn(a, b):
    return lax.dot_general(a, b, (((0,), (0,)), ((), ())), preferred_element_type=F32)


def _split2(a):
    hi = a.astype(BF16)
    lo = (a - hi.astype(F32)).astype(BF16)
    return hi, lo


def _dot3(a, b):
    ah, al = _split2(a)
    bh, bl = _split2(b)
    return _dot(ah, bh) + _dot(ah, bl) + _dot(al, bh)


def _modnorm(x, gain, scale, shift):
    ms = jnp.mean(x * x, axis=-1, keepdims=True)
    return x * lax.rsqrt(ms + RMS_EPS) * gain * (1.0 + scale) + shift


def _log_sigmoid(z):
    return jnp.minimum(z, 0.0) - jnp.log1p(jnp.exp(-jnp.abs(z)))


def _params(*sem):
    return pltpu.CompilerParams(dimension_semantics=sem, vmem_limit_bytes=VMEM_LIMIT)


def _mod_spec(rows, d, tiles_per_group):
    return pl.BlockSpec((None, rows, d), lambda i: (i // tiles_per_group, 0, 0))


def _adaln_kernel(c_ref, w_ref, b_ref, o_ref):
    c = c_ref[...]
    o_ref[...] = _dot3(c * jax.nn.sigmoid(c), w_ref[...]) + b_ref[...]


def adaln(c, ada_w, ada_b, tn=1536):
    L, D, D6 = ada_w.shape
    R = c.shape[0]
    return pl.pallas_call(
        _adaln_kernel,
        grid=(L, D6 // tn),
        in_specs=[pl.BlockSpec((R, D), lambda l, j: (0, 0)),
                  pl.BlockSpec((None, D, tn), lambda l, j: (l, 0, j)),
                  pl.BlockSpec((None, 1, tn), lambda l, j: (l, 0, j))],
        out_specs=pl.BlockSpec((None, R, tn), lambda l, j: (l, 0, j)),
        out_shape=jax.ShapeDtypeStruct((L, R, D6), F32),
        compiler_params=_params("arbitrary", "arbitrary"),
        name="adaln",
    )(c, ada_w, ada_b.reshape(L, 1, D6))


def _segcumsum_kernel(x_ref, o_ref, *, seg):
    C, T = x_ref.shape
    lane = lax.broadcasted_iota(I32, (C, LANES), 1)
    w = min(seg, LANES)
    carry = None
    for blk in range(T // LANES):
        x = x_ref[:, blk * LANES:(blk + 1) * LANES]
        k = 1
        while k < w:
            x = x + jnp.where((lane & (w - 1)) >= k, pltpu.roll(x, k, 1), 0.0)
            k *= 2
        if seg > LANES:
            if blk % (seg // LANES) != 0:
                x = x + carry
            carry = x[:, LANES - 1:LANES]
        o_ref[:, blk * LANES:(blk + 1) * LANES] = x


def seg_cumsum(x, seg):
    R, C, T = x.shape
    assert T % LANES == 0 and (seg & (seg - 1)) == 0
    assert T % seg == 0 and (seg <= LANES or seg % LANES == 0)
    return pl.pallas_call(
        functools.partial(_segcumsum_kernel, seg=seg),
        grid=(R,),
        in_specs=[pl.BlockSpec((None, C, T), lambda r: (r, 0, 0))],
        out_specs=pl.BlockSpec((None, C, T), lambda r: (r, 0, 0)),
        out_shape=jax.ShapeDtypeStruct((R, C, T), F32),
        compiler_params=_params("arbitrary"),
        name="seg_cumsum",
    )(x)


def _pair_headnorm(blk, gain2, hd):
    lane = lax.broadcasted_iota(I32, blk.shape, 1)
    lo = lane < hd
    sq = blk * blk
    s_lo = jnp.sum(jnp.where(lo, sq, 0.0), axis=-1, keepdims=True)
    s_hi = jnp.sum(jnp.where(lo, 0.0, sq), axis=-1, keepdims=True)
    inv = lax.rsqrt(jnp.where(lo, s_lo, s_hi) * (1.0 / hd) + RMS_EPS)
    return blk * inv * gain2


def _fox_proj_kernel(x_ref, sc_ref, sh_ref, gain_ref, w_ref, wfh_ref, wfl_ref, bf_ref, qg_ref, kg_ref,
                     q_ref, k_ref, v_ref, kb_ref, vb_ref, sg_ref, lf_ref, *, n_heads):
    D = x_ref.shape[1]
    hd = D // n_heads
    h = _modnorm(x_ref[...], gain_ref[...], sc_ref[...], sh_ref[...])
    hh, hl = _split2(h)
    qk_scale = hd ** -0.5
    q = _dot(hh, w_ref[:, 0:D])
    for c in range(D // LANES):
        sl = slice(c * LANES, (c + 1) * LANES)
        q_ref[:, sl] = (_pair_headnorm(q[:, sl], qg_ref[...], hd) * qk_scale).astype(BF16)
    k = _dot(hh, w_ref[:, D:2 * D])
    for c in range(D // LANES):
        sl = slice(c * LANES, (c + 1) * LANES)
        kn = _pair_headnorm(k[:, sl], kg_ref[...], hd)
        k_ref[:, sl] = kn
        kb_ref[:, sl] = kn.astype(BF16)
    v = _dot(hh, w_ref[:, 2 * D:3 * D])
    v_ref[...] = v
    vb_ref[...] = v.astype(BF16)
    sg_ref[...] = jax.nn.sigmoid(_dot(hh, w_ref[:, 3 * D:4 * D]))
    f = _dot(hh, wfh_ref[...]) + _dot(hh, wfl_ref[...]) + _dot(hl, wfh_ref[...])
    lf_ref[...] = _log_sigmoid(f[:, 0:n_heads] + bf_ref[...])


def fox_project(x, scale, shift, gain, w_in, b_f, q_gain, k_gain, *, tm, tiles_per_group):
    N, D = x.shape
    H = b_f.shape[0]
    hd = D // H
    assert 2 * hd == LANES
    R = scale.shape[1]
    w_main = jnp.concatenate([w_in[:, :3 * D], w_in[:, 3 * D + H:]], axis=1).astype(BF16)
    w_f = jnp.pad(w_in[:, 3 * D:3 * D + H], ((0, 0), (0, LANES - H)))
    wfh = w_f.astype(BF16)
    wfl = (w_f - wfh.astype(F32)).astype(BF16)
    const = lambda i: (0, 0)
    row = lambda i: (i, 0)
    outs = pl.pallas_call(
        functools.partial(_fox_proj_kernel, n_heads=H),
        grid=(N // tm,),
        in_specs=[pl.BlockSpec((tm, D), row),
                  _mod_spec(R, D, tiles_per_group), _mod_spec(R, D, tiles_per_group),
                  pl.BlockSpec((1, D), const),
                  pl.BlockSpec((D, 4 * D), const),
                  pl.BlockSpec((D, LANES), const), pl.BlockSpec((D, LANES), const),
                  pl.BlockSpec((1, H), const),
                  pl.BlockSpec((1, LANES), const), pl.BlockSpec((1, LANES), const)],
        out_specs=[pl.BlockSpec((tm, D), row)] * 6 + [pl.BlockSpec((tm, H), row)],
        out_shape=[jax.ShapeDtypeStruct((N, D), BF16),
                   jax.ShapeDtypeStruct((N, D), F32), jax.ShapeDtypeStruct((N, D), F32),
                   jax.ShapeDtypeStruct((N, D), BF16), jax.ShapeDtypeStruct((N, D), BF16),
                   jax.ShapeDtypeStruct((N, D), F32),
                   jax.ShapeDtypeStruct((N, H), F32)],
        compiler_params=_params("arbitrary"),
        name="fox_project",
    )(x, scale, shift, gain.reshape(1, D), w_main, wfh, wfl, b_f.reshape(1, H),
      jnp.tile(q_gain, 2).reshape(1, LANES), jnp.tile(k_gain, 2).reshape(1, LANES))
    return outs


def _fox_attn_kernel(q_ref, k_ref, v_ref, fc_ref, fr_ref, o_ref, *, hd, n_sub, hps):
    TQ = q_ref.shape[0]
    SQ = TQ // n_sub
    qi = pl.program_id(2)
    lane = lax.broadcasted_iota(I32, (SQ, LANES), 1)
    row = lax.broadcasted_iota(I32, (SQ, TQ), 0)
    col = lax.broadcasted_iota(I32, (SQ, TQ), 1)
    chains = [(hh, sb) for hh in range(hps) for sb in range(n_sub)]
    nch = len(chains)
    qms, fqs = [], []
    for hh, sb in chains:
        pb, ph = hh // 2, hh % 2
        q = q_ref[sb * SQ:(sb + 1) * SQ, pb * LANES:(pb + 1) * LANES]
        qms.append(jnp.where((lane >= ph * hd) & (lane < (ph + 1) * hd), q, jnp.zeros_like(q)))
        fqs.append(fc_ref[sb * SQ:(sb + 1) * SQ, hh:hh + 1])

    def block_scores(j):
        start = pl.multiple_of(j * TQ, TQ)
        kjs = [k_ref[pl.ds(start, TQ), pb * LANES:(pb + 1) * LANES] for pb in range(hps // 2)]
        return tuple(_dot_nt(qms[c], kjs[hh // 2]) + (fqs[c] - fr_ref[hh:hh + 1, pl.ds(start, TQ)])
                     for c, (hh, sb) in enumerate(chains))

    def update(j, state, scores, masked):
        start = pl.multiple_of(j * TQ, TQ)
        vjs = [v_ref[pl.ds(start, TQ), pb * LANES:(pb + 1) * LANES] for pb in range(hps // 2)]
        ss = [jnp.where(col <= row + sb * SQ, scores[c], NEG_INF) if masked else scores[c]
              for c, (hh, sb) in enumerate(chains)]
        m_new = [jnp.maximum(state[c][0], jnp.max(ss[c], axis=-1, keepdims=True)) for c in range(nch)]
        ps = [jnp.exp(ss[c] - m_new[c]) for c in range(nch)]
        alphas = [jnp.exp(state[c][0] - m_new[c]) for c in range(nch)]
        pvs = [_dot(ps[c].astype(BF16), vjs[hh // 2]) for c, (hh, sb) in enumerate(chains)]
        return tuple((m_new[c],
                      alphas[c] * state[c][1] + jnp.sum(ps[c], axis=-1, keepdims=True),
                      alphas[c] * state[c][2] + pvs[c]) for c in range(nch))

    def step(j, carry):
        state, scores = carry
        nxt = block_scores(j + 1)
        return update(j, state, scores, False), nxt

    init = tuple((jnp.full((SQ, 1), NEG_INF, F32), jnp.zeros((SQ, 1), F32), jnp.zeros((SQ, LANES), F32))
                 for _ in chains)
    state, scores = lax.fori_loop(0, qi, step, (init, block_scores(0)))
    final = update(qi, state, scores, True)
    for pb in range(hps // 2):
        for sb in range(n_sub):
            even = final[(2 * pb) * n_sub + sb]
            odd = final[(2 * pb + 1) * n_sub + sb]
            o_ref[sb * SQ:(sb + 1) * SQ, pb * LANES:(pb + 1) * LANES] = jnp.where(
                lane < hd, even[2] / even[1], odd[2] / odd[1])


def fox_attend_prompt(q, kb, vb, F, *, n_heads, tq=256, hps=2, n_sub=2):
    B, S, D = q.shape
    hd = D // n_heads
    assert 2 * hd == LANES and hps % 2 == 0 and n_heads % hps == 0
    HG = n_heads // hps
    W = hps * hd
    fcol = F.reshape(B, S, HG, hps).transpose(0, 2, 1, 3)
    frow = fcol.transpose(0, 1, 3, 2)
    return pl.pallas_call(
        functools.partial(_fox_attn_kernel, hd=hd, n_sub=n_sub, hps=hps),
        grid=(B, HG, S // tq),
        in_specs=[pl.BlockSpec((None, tq, W), lambda b, h, i: (b, i, h)),
                  pl.BlockSpec((None, S, W), lambda b, h, i: (b, 0, h)),
                  pl.BlockSpec((None, S, W), lambda b, h, i: (b, 0, h)),
                  pl.BlockSpec((None, None, tq, hps), lambda b, h, i: (b, h, i, 0)),
                  pl.BlockSpec((None, None, hps, S), lambda b, h, i: (b, h, 0, 0))],
        out_specs=pl.BlockSpec((None, tq, W), lambda b, h, i: (b, i, h)),
        out_shape=jax.ShapeDtypeStruct((B, S, D), F32),
        compiler_params=_params("arbitrary", "arbitrary", "arbitrary"),
        name="fox_attend_prompt",
    )(q, kb, vb, fcol, frow)


def _fox_decode_kernel(pt_ref, qbd_ref, *refs, n_heads, pp):
    del pt_ref
    kt_refs, vt_refs, cf_refs = refs[:pp], refs[pp:2 * pp], refs[2 * pp:3 * pp]
    kn_ref, vn_ref, cn_ref, o_ref, m_scr, l_scr, acc_scr, carry_scr = refs[3 * pp:]
    H = n_heads
    step = pl.program_id(1)
    n_steps = pl.num_programs(1)
    R, D = qbd_ref.shape
    T = R // H
    P = cf_refs[0].shape[1]

    @pl.when(step == 0)
    def _():
        m_scr[...] = jnp.full(m_scr.shape, NEG_INF, F32)
        l_scr[...] = jnp.zeros(l_scr.shape, F32)
        acc_scr[...] = jnp.zeros(acc_scr.shape, F32)
        carry_scr[...] = jnp.zeros(carry_scr.shape, F32)

    qbd = qbd_ref[...]

    def merge(m_i, l_i, acc_i):
        m = m_scr[...]
        m_new = jnp.maximum(m, m_i)
        a_old = jnp.exp(m - m_new)
        a_new = jnp.exp(m_i - m_new)
        l_scr[...] = a_old * l_scr[...] + a_new * l_i
        acc_scr[...] = a_old * acc_scr[...] + a_new * acc_i
        m_scr[...] = m_new

    def scores(s, fk):
        return s - jnp.concatenate([fk] * T, axis=0)

    carry = carry_scr[...]
    ss = []
    for i in range(pp):
        cf = cf_refs[i][...]
        ss.append(scores(_dot(qbd, kt_refs[i][...].astype(BF16)), cf + carry))
        carry = carry + cf[:, P - 1:P]
    carry_scr[...] = carry
    m_s = jnp.max(ss[0], axis=-1, keepdims=True)
    for s in ss[1:]:
        m_s = jnp.maximum(m_s, jnp.max(s, axis=-1, keepdims=True))
    l_s = jnp.zeros_like(m_s)
    acc_s = jnp.zeros(acc_scr.shape, F32)
    for i, s in enumerate(ss):
        pr = jnp.exp(s - m_s)
        l_s = l_s + jnp.sum(pr, axis=-1, keepdims=True)
        acc_s = acc_s + _dot_nt(pr.astype(BF16), vt_refs[i][...].astype(BF16))
    merge(m_s, l_s, acc_s)

    @pl.when(step == n_steps - 1)
    def _():
        s2 = scores(_dot_nt(qbd, kn_ref[...]), cn_ref[...] + carry_scr[...])
        key = lax.broadcasted_iota(I32, s2.shape, 1)
        t = lax.broadcasted_iota(I32, s2.shape, 0) // H
        s2 = jnp.where(key <= t, s2, NEG_INF)
        m2 = jnp.max(s2, axis=-1, keepdims=True)
        p2 = jnp.exp(s2 - m2)
        merge(m2, jnp.sum(p2, axis=-1, keepdims=True), _dot(p2.astype(BF16), vn_ref[...]))
        out = acc_scr[...] / l_scr[...]
        rh = lax.broadcasted_iota(I32, (R, D), 0) % H
        ch = lax.broadcasted_iota(I32, (R, D), 1) // (D // H)
        out = jnp.where(rh == ch, out, 0.0)
        o_ref[...] = jnp.sum(out.reshape(T, H, D), axis=1)


def fox_attend_sample(q, kb_new, vb_new, lf_new, cache_kt, cache_vt, cache_cf, page_table, j, *, n_heads):
    Bd, T, D = q.shape
    H = n_heads
    hd = D // H
    n_pages = page_table.shape[1]
    n_fox, n_pool, _, P = cache_cf.shape
    R = T * H
    pp = next(c for c in (16, 8, 4, 2, 1) if n_pages % c == 0)
    assert T <= P
    q4 = q.reshape(Bd, T, H, hd)
    eye = jnp.eye(H, dtype=q.dtype)
    qbd = (q4[:, :, :, None, :] * eye[None, None, :, :, None]).reshape(Bd, R, D)
    pad = ((0, 0), (0, P - T), (0, 0))
    kn = jnp.pad(kb_new, pad)
    vn = jnp.pad(vb_new, pad)
    cn = seg_cumsum(jnp.pad(lf_new.transpose(0, 2, 1), ((0, 0), (0, 0), (0, P - T))), P)
    pt = page_table.reshape(-1).astype(I32)

    def page_spec(shape, i):
        return pl.BlockSpec((None, None) + shape, lambda b, s, pt: (j, pt[b * n_pages + s * pp + i], 0, 0))

    per_seq = lambda shape: pl.BlockSpec((None,) + shape, lambda b, s, pt: (b, 0, 0))
    grid_spec = pltpu.PrefetchScalarGridSpec(
        num_scalar_prefetch=1,
        grid=(Bd, n_pages // pp),
        in_specs=([per_seq((R, D))]
                  + [page_spec((D, P), i) for i in range(pp)]
                  + [page_spec((D, P), i) for i in range(pp)]
                  + [page_spec((H, P), i) for i in range(pp)]
                  + [per_seq((P, D)), per_seq((P, D)), per_seq((H, P))]),
        out_specs=per_seq((T, D)),
        scratch_shapes=[pltpu.VMEM((R, 1), F32), pltpu.VMEM((R, 1), F32),
                        pltpu.VMEM((R, D), F32), pltpu.VMEM((H, 1), F32)],
    )
    return pl.pallas_call(
        functools.partial(_fox_decode_kernel, n_heads=H, pp=pp),
        grid_spec=grid_spec,
        out_shape=jax.ShapeDtypeStruct((Bd, T, D), F32),
        compiler_params=_params("arbitrary", "arbitrary"),
        name="fox_attend_sample",
    )(pt, qbd, *([cache_kt] * pp), *([cache_vt] * pp), *([cache_cf] * pp), kn, vn, cn)


def prepare_cache(cache_k, cache_v, cache_logf):
    n_fox, n_pool, P, H, hd = cache_k.shape
    kt = cache_k.transpose(0, 1, 3, 4, 2).reshape(n_fox, n_pool, H * hd, P)
    vt = cache_v.transpose(0, 1, 3, 4, 2).reshape(n_fox, n_pool, H * hd, P)
    rows = n_fox * n_pool * H
    group = 2048 if rows % 2048 == 0 else H
    cf = seg_cumsum(cache_logf.transpose(0, 1, 3, 2).reshape(rows // group, group, P), P)
    return kt, vt, cf.reshape(n_fox, n_pool, H, P)


def _outproj_kernel(a_ref, gate_ref, w_ref, x_ref, g_ref, hn_ref, o_ref, *, head_norm):
    a = a_ref[...]
    D = a.shape[1]
    if head_norm:
        blocks = []
        for c in range(D // LANES):
            sl = slice(c * LANES, (c + 1) * LANES)
            blk = a[:, sl]
            ms = jnp.mean(blk * blk, axis=-1, keepdims=True)
            blocks.append(blk * lax.rsqrt(ms + RMS_EPS) * hn_ref[:, sl])
        a = jnp.concatenate(blocks, axis=-1)
    y = (a * gate_ref[...]).astype(BF16)
    o_ref[...] = x_ref[...] + g_ref[...] * _dot(y, w_ref[...])


def out_project(a, gate, w_out, x, g, head_gain, *, tm, tiles_per_group, head_norm):
    N, D = x.shape
    R = g.shape[1]
    row = lambda i: (i, 0)
    const = lambda i: (0, 0)
    return pl.pallas_call(
        functools.partial(_outproj_kernel, head_norm=head_norm),
        grid=(N // tm,),
        in_specs=[pl.BlockSpec((tm, D), row), pl.BlockSpec((tm, D), row),
                  pl.BlockSpec((D, D), const), pl.BlockSpec((tm, D), row),
                  _mod_spec(R, D, tiles_per_group), pl.BlockSpec((1, D), const)],
        out_specs=pl.BlockSpec((tm, D), row),
        out_shape=jax.ShapeDtypeStruct((N, D), F32),
        compiler_params=_params("arbitrary"),
        name="out_project",
    )(a, gate, w_out.astype(BF16), x, g, head_gain.reshape(1, D))


def _ml_proj_kernel(x_ref, sc_ref, sh_ref, gain_ref, w_ref, wgh_ref, wgl_ref, bi_ref, bf_ref,
                    q_ref, k_ref, v_ref, so_ref, il_ref, fl_ref, *, n_heads, qk_width):
    D = x_ref.shape[1]
    QW = qk_width
    h = _modnorm(x_ref[...], gain_ref[...], sc_ref[...], sh_ref[...])
    hh, hl = _split2(h)
    k_scale = (QW // n_heads) ** -0.5
    q_ref[...] = _dot(hh, w_ref[:, 0:QW]).astype(BF16)
    k_ref[...] = (_dot(hh, w_ref[:, QW:2 * QW]) * k_scale).astype(BF16)
    v_ref[...] = _dot(hh, w_ref[:, 2 * QW:2 * QW + D])
    so_ref[...] = jax.nn.sigmoid(_dot(hh, w_ref[:, 2 * QW + D:2 * QW + 2 * D]))
    gts = _dot(hh, wgh_ref[...]) + _dot(hh, wgl_ref[...]) + _dot(hl, wgh_ref[...])
    il_ref[...] = gts[:, 0:n_heads] + bi_ref[...]
    fl_ref[...] = _log_sigmoid(gts[:, n_heads:2 * n_heads] + bf_ref[...])


def ml_project(x, scale, shift, gain, w_in, b_i, b_f, *, tm, tiles_per_group, v_width):
    N, D = x.shape
    H = b_i.shape[0]
    QW = (w_in.shape[1] - 2 * v_width - 2 * H) // 2
    R = scale.shape[1]
    g0 = 2 * QW + v_width
    w_main = jnp.concatenate([w_in[:, :g0], w_in[:, g0 + 2 * H:]], axis=1).astype(BF16)
    w_g = jnp.pad(w_in[:, g0:g0 + 2 * H], ((0, 0), (0, LANES - 2 * H)))
    wgh = w_g.astype(BF16)
    wgl = (w_g - wgh.astype(F32)).astype(BF16)
    const = lambda i: (0, 0)
    row = lambda i: (i, 0)
    WM = w_main.shape[1]
    return pl.pallas_call(
        functools.partial(_ml_proj_kernel, n_heads=H, qk_width=QW),
        grid=(N // tm,),
        in_specs=[pl.BlockSpec((tm, D), row),
                  _mod_spec(R, D, tiles_per_group), _mod_spec(R, D, tiles_per_group),
                  pl.BlockSpec((1, D), const),
                  pl.BlockSpec((D, WM), const),
                  pl.BlockSpec((D, LANES), const), pl.BlockSpec((D, LANES), const),
                  pl.BlockSpec((1, H), const), pl.BlockSpec((1, H), const)],
        out_specs=[pl.BlockSpec((tm, QW), row), pl.BlockSpec((tm, QW), row),
                   pl.BlockSpec((tm, v_width), row), pl.BlockSpec((tm, v_width), row),
                   pl.BlockSpec((tm, H), row), pl.BlockSpec((tm, H), row)],
        out_shape=[jax.ShapeDtypeStruct((N, QW), BF16), jax.ShapeDtypeStruct((N, QW), BF16),
                   jax.ShapeDtypeStruct((N, v_width), F32), jax.ShapeDtypeStruct((N, v_width), F32),
                   jax.ShapeDtypeStruct((N, H), F32), jax.ShapeDtypeStruct((N, H), F32)],
        compiler_params=_params("arbitrary"),
        name="ml_project",
    )(x, scale, shift, gain.reshape(1, D), w_main, wgh, wgl, b_i.reshape(1, H), b_f.reshape(1, H))


def _mlstm_kernel(q_ref, k_ref, v_ref, ic_ref, ir_ref, bc_ref, br_ref, c0_ref, n0_ref, m0_ref,
                  h_ref, c_ref, n_ref, m_ref, *, n_heads, dk):
    nbb, L = q_ref.shape[0], q_ref.shape[1]
    ci = pl.program_id(1)

    @pl.when(ci == 0)
    def _():
        c_ref[...] = c0_ref[...]
        n_ref[...] = n0_ref[...]
        m_ref[...] = m0_ref[...]

    tri = lax.broadcasted_iota(I32, (L, L), 1) <= lax.broadcasted_iota(I32, (L, L), 0)
    lane = lax.broadcasted_iota(I32, (L, LANES), 1)
    lane1 = lax.broadcasted_iota(I32, (1, LANES), 1)
    rsub = lax.broadcasted_iota(I32, (LANES, LANES), 0)
    dv = LANES
    m_in = [m_ref[bb] for bb in range(nbb)]
    units = [(bb, h) for bb in range(nbb) for h in range(n_heads)]
    pairs = [(bb, pr) for bb in range(nbb) for pr in range(n_heads // 2)]
    qp = {(bb, pr): q_ref[bb, :, pr * LANES:(pr + 1) * LANES] for bb, pr in pairs}
    kp = {(bb, pr): k_ref[bb, :, pr * LANES:(pr + 1) * LANES] for bb, pr in pairs}
    c_in = {u: c_ref[u[0], u[1]] for u in pairs}
    n_in = {u: n_ref[u[0], u[1]] for u in pairs}
    sel = [(lane >= hh * dk) & (lane < (hh + 1) * dk) for hh in range(2)]
    qm = {(bb, h): jnp.where(sel[h % 2], qp[(bb, h // 2)], jnp.zeros_like(qp[(bb, h // 2)])) for bb, h in units}
    qk = {(bb, h): _dot_nt(qm[(bb, h)], kp[(bb, h // 2)]) for bb, h in units}
    qc = {(bb, h): _dot(qm[(bb, h)], c_in[(bb, h // 2)].astype(BF16)) for bb, h in units}
    st = {}
    for bb, h in units:
        b_l = bc_ref[bb][:, h:h + 1]
        b_s = br_ref[bb][h:h + 1, :]
        ig_s = ir_ref[bb][h:h + 1, :]
        ig_l = ic_ref[bb][:, h:h + 1]
        m = m_in[bb][:, h:h + 1]
        dm = jnp.where(tri, b_l - b_s + ig_s, NEG_INF)
        inter = b_l + m
        m_t = jnp.maximum(inter, jnp.max(dm, axis=-1, keepdims=True))
        w_inter = jnp.exp(inter - m_t)
        a = jnp.exp(dm - m_t) * qk[(bb, h)]
        b_last = b_l[L - 1:L, :]
        m_new = jnp.maximum(b_last + m, jnp.max(b_last - b_s + ig_s, axis=-1, keepdims=True))
        wk = jnp.exp(b_last - b_l + ig_l - m_new)
        decay = jnp.exp(b_last + m - m_new)
        vh = v_ref[bb, :, h * dv:(h + 1) * dv]
        st[(bb, h)] = (a, m_t, w_inter, m_new, wk, decay, vh)
    av = {u: _dot(st[u][0].astype(BF16), st[u][6].astype(BF16)) for u in units}
    upd = {(bb, h): _dot_tn(kp[(bb, h // 2)], (st[(bb, h)][4] * st[(bb, h)][6]).astype(BF16)) for bb, h in units}
    for bb, pr in pairs:
        c_pair, n_pair = c_in[(bb, pr)], n_in[(bb, pr)]
        qpf = qp[(bb, pr)].astype(F32)
        kpf = kp[(bb, pr)].astype(F32)
        new_c, new_n = [], []
        for hh in range(2):
            h = 2 * pr + hh
            a, m_t, w_inter, m_new, wk, decay, vh = st[(bb, h)]
            num = w_inter * qc[(bb, h)] + av[(bb, h)]
            qn = jnp.sum(jnp.where(sel[hh], qpf, 0.0) * n_pair, axis=-1, keepdims=True)
            den = w_inter * qn + jnp.sum(a, axis=-1, keepdims=True)
            h_ref[bb, :, h * dv:(h + 1) * dv] = num / jnp.maximum(jnp.abs(den), jnp.exp(-m_t))
            new_c.append(decay * c_pair + upd[(bb, h)])
            new_n.append(decay * n_pair + jnp.sum(wk * kpf, axis=0, keepdims=True))
            m_ref[bb, :, h:h + 1] = m_new
        c_ref[bb, pr] = jnp.where(rsub < dk, new_c[0], new_c[1])
        n_ref[bb, pr] = jnp.where(lane1 < dk, new_n[0], new_n[1])


def mlstm_chunked(q, k, v, i_log, f_log, C0, n0, m0, chunk):
    B, T, _ = q.shape
    H = i_log.shape[-1]
    dk = C0.shape[2]
    dv = C0.shape[3]
    assert 2 * dk == LANES and dv == LANES
    nc = T // chunk
    L = chunk
    Tp = -(-T // LANES) * LANES
    fT = jnp.pad(f_log.transpose(0, 2, 1), ((0, 0), (0, 0), (0, Tp - T)))
    bT = seg_cumsum(fT, chunk)[:, :, :T]
    br = bT.reshape(B, H, nc, L).transpose(0, 2, 1, 3)
    bc = br.transpose(0, 1, 3, 2)
    ic = i_log.reshape(B, nc, L, H)
    ir = ic.transpose(0, 1, 3, 2)
    HP = H // 2
    c0 = C0.reshape(B, HP, 2 * dk, dv)
    n0p = n0.reshape(B, HP, 1, 2 * dk)
    m0p = m0.reshape(B, 1, H)
    QW = H * dk
    VW = H * dv
    nbb = 2 if B % 2 == 0 else 1
    col_spec = pl.BlockSpec((nbb, None, L, H), lambda b, c: (b, c, 0, 0))
    row_spec = pl.BlockSpec((nbb, None, H, L), lambda b, c: (b, c, 0, 0))
    st = lambda b, c: (b, 0, 0, 0)
    h, Cn, nn, mn = pl.pallas_call(
        functools.partial(_mlstm_kernel, n_heads=H, dk=dk),
        grid=(B // nbb, nc),
        in_specs=[pl.BlockSpec((nbb, L, QW), lambda b, c: (b, c, 0)),
                  pl.BlockSpec((nbb, L, QW), lambda b, c: (b, c, 0)),
                  pl.BlockSpec((nbb, L, VW), lambda b, c: (b, c, 0)),
                  col_spec, row_spec, col_spec, row_spec,
                  pl.BlockSpec((nbb, HP, 2 * dk, dv), st),
                  pl.BlockSpec((nbb, HP, 1, 2 * dk), st),
                  pl.BlockSpec((nbb, 1, H), lambda b, c: (b, 0, 0))],
        out_specs=[pl.BlockSpec((nbb, L, VW), lambda b, c: (b, c, 0)),
                   pl.BlockSpec((nbb, HP, 2 * dk, dv), st),
                   pl.BlockSpec((nbb, HP, 1, 2 * dk), st),
                   pl.BlockSpec((nbb, 1, H), lambda b, c: (b, 0, 0))],
        out_shape=[jax.ShapeDtypeStruct((B, T, VW), F32),
                   jax.ShapeDtypeStruct((B, HP, 2 * dk, dv), F32),
                   jax.ShapeDtypeStruct((B, HP, 1, 2 * dk), F32),
                   jax.ShapeDtypeStruct((B, 1, H), F32)],
        compiler_params=_params("arbitrary", "arbitrary"),
        name="mlstm_chunked",
    )(q, k, v, ic, ir, bc, br, c0, n0p, m0p)
    return h, Cn.reshape(B, H, dk, dv), nn.reshape(B, H, dk), mn.reshape(B, H)


def _staircase(n):
    return [(a, n // (a + 1)) for a in range(n)]


def _extract_topk(ss, payloads, n):
    ss = list(ss)
    R = ss[0].shape[0]
    ridx = lax.broadcasted_iota(I32, ss[0].shape, 0).astype(F32)
    vals = [[] for _ in ss]
    pays = [[] for _ in ss]
    for it in range(n):
        for c in range(len(ss)):
            s = ss[c]
            m = jnp.max(s, axis=0, keepdims=True)
            am = jnp.min(jnp.where(s == m, ridx, float(R)), axis=0, keepdims=True)
            hit = ridx == am
            if payloads[c] is None:
                pays[c].append(am)
            else:
                pays[c].append(jnp.max(jnp.where(hit, payloads[c], -1.0), axis=0, keepdims=True))
            vals[c].append(m)
            if it + 1 < n:
                ss[c] = jnp.where(hit, NEG_INF, s)
    return vals, pays


def _peer_topk_kernel(x_ref, sc_ref, sh_ref, gain_ref, wq_ref, k1_ref, k2_ref,
                      hb_ref, e_ref, g_ref, q_scr, v1_scr, i1_scr, v2_scr, i2_scr, *, n_heads, n_keys):
    TM = x_ref.shape[0]
    K = PEER_TOPK
    half = k1_ref.shape[2]
    hb = _modnorm(x_ref[...], gain_ref[...], sc_ref[...], sh_ref[...]).astype(BF16)
    hb_ref[...] = hb
    q_scr[...] = _dot(hb, wq_ref[...]).astype(BF16)
    sub8 = lax.broadcasted_iota(I32, (8, LANES), 0)

    def head_body(hd, _):
        c0 = pl.multiple_of(hd * 2 * half, 2 * half)
        c1 = pl.multiple_of(hd * 2 * half + half, half)
        s1 = _dot_nt(k1_ref[hd], q_scr[:, pl.ds(c0, half)])
        s2 = _dot_nt(k2_ref[hd], q_scr[:, pl.ds(c1, half)])
        NB = TM // LANES
        blocks = [slice(blk * LANES, (blk + 1) * LANES) for blk in range(NB)]
        vals, idxs = _extract_topk([s[:, ls] for ls in blocks for s in (s1, s2)], [None] * (2 * NB), K)
        for blk in range(NB):
            for side, (v_scr, i_scr) in enumerate(((v1_scr, i1_scr), (v2_scr, i2_scr))):
                for r in range(K):
                    v_scr[blk, r:r + 1, :] = vals[2 * blk + side][r]
                    i_scr[blk, r:r + 1, :] = idxs[2 * blk + side][r]
        cands, pays = [], []
        for blk in range(NB):
            cand, pay = [], []
            for a, cnt in _staircase(K):
                if cnt == 1:
                    break
                for b0 in range(0, cnt, 8):
                    nb = min(8, cnt - b0)
                    cv = v1_scr[blk, a:a + 1, :] + v2_scr[blk, b0:b0 + 8, :]
                    pv = i1_scr[blk, a:a + 1, :] * n_keys + i2_scr[blk, b0:b0 + 8, :]
                    if nb < 8:
                        cv = jnp.where(sub8 < nb, cv, NEG_INF)
                    cand.append(cv)
                    pay.append(pv)
            a1 = K // 2
            cand.append(v1_scr[blk, a1:K, :] + v2_scr[blk, 0:1, :])
            pay.append(i1_scr[blk, a1:K, :] * n_keys + i2_scr[blk, 0:1, :])
            cands.append(jnp.concatenate(cand, axis=0))
            pays.append(jnp.concatenate(pay, axis=0))
        tops, topes = _extract_topk(cands, pays, K)
        out_rows = pl.ds(pl.multiple_of(hd * K, K), K)
        for blk in range(NB):
            top_s, top_e = tops[blk], topes[blk]
            ex = [jnp.exp(t - top_s[0]) for t in top_s]
            den = ex[0]
            for t in ex[1:]:
                den = den + t
            inv = 1.0 / den
            for r in range(K):
                i1_scr[blk, r:r + 1, :] = top_e[r]
                v1_scr[blk, r:r + 1, :] = ex[r] * inv
            e_ref[out_rows, blocks[blk]] = i1_scr[blk].astype(I32)
            g_ref[out_rows, blocks[blk]] = v1_scr[blk]
        return 0

    lax.fori_loop(0, n_heads, head_body, 0)


def peer_retrieve(x, scale, shift, gain, w_q, sub_k1, sub_k2, *, tm, tiles_per_group):
    N, D = x.shape
    HP, NK, half = sub_k1.shape
    assert NK == LANES and half == LANES and PEER_TOPK % 8 == 0
    R = scale.shape[1]
    QW = w_q.shape[1]
    J = HP * PEER_TOPK
    const = lambda i: (0, 0)
    const3 = lambda i: (0, 0, 0)
    kspec = pl.BlockSpec((HP, NK, half), const3)
    return pl.pallas_call(
        functools.partial(_peer_topk_kernel, n_heads=HP, n_keys=NK),
        grid=(N // tm,),
        in_specs=[pl.BlockSpec((tm, D), lambda i: (i, 0)),
                  _mod_spec(R, D, tiles_per_group), _mod_spec(R, D, tiles_per_group),
                  pl.BlockSpec((1, D), const),
                  pl.BlockSpec((D, QW), const),
                  kspec, kspec],
        out_specs=[pl.BlockSpec((tm, D), lambda i: (i, 0)),
                   pl.BlockSpec((J, tm), lambda i: (0, i)),
                   pl.BlockSpec((J, tm), lambda i: (0, i))],
        out_shape=[jax.ShapeDtypeStruct((N, D), BF16),
                   jax.ShapeDtypeStruct((J, N), I32),
                   jax.ShapeDtypeStruct((J, N), F32)],
        scratch_shapes=[pltpu.VMEM((tm, QW), BF16),
                        ] + [pltpu.VMEM((tm // LANES, PEER_TOPK, LANES), F32)] * 4,
        compiler_params=_params("arbitrary"),
        name="peer_retrieve",
    )(x, scale, shift, gain.reshape(1, D), w_q.astype(BF16), sub_k1.astype(BF16), sub_k2.astype(BF16))


def _peer_mix_kernel(hb_ref, i1n_ref, i2n_ref, gn_ref, i1c_ref, i2c_ref, ut_ref, v_ref, x_ref, g2_ref, o_ref,
                     a_scr, w_scr, hbuf_scr, y_scr, *, n_keys, tg):
    TM = hb_ref.shape[0]
    TE = ut_ref.shape[1]
    J = i1n_ref.shape[1]
    NE = pl.num_programs(1) // 2
    NT = pl.num_programs(0) - 1
    t = pl.program_id(0)
    j = pl.program_id(1)
    NP2 = n_keys // 2
    W2 = 2 * n_keys
    per_step = TM // NE
    rows = lax.broadcasted_iota(I32, (NP2, J), 0)
    cols = lax.broadcasted_iota(I32, (W2, J), 0)

    n_groups = per_step // tg

    def group_grids(g):
        t0 = pl.multiple_of(j * per_step + g * tg, tg)
        grids = []
        for u in range(tg):
            i1 = i1c_ref[pl.ds(t0 + u, 1), :]
            i2 = i2c_ref[pl.ds(t0 + u, 1), :]
            w = w_scr[pl.ds(t0 + u, 1), :]
            p1 = jnp.where(rows == (i1 >> 1), w, 0.0).astype(BF16)
            p2 = jnp.where(cols == ((i1 & 1) * n_keys + i2), 1.0, 0.0).astype(BF16)
            grids.append(_dot_nt(p1, p2).astype(BF16))
        return jnp.stack(grids, axis=0)

    def group_store(g, stacked):
        t0 = pl.multiple_of(j * per_step + g * tg, tg)
        hbuf_scr[:, pl.ds(t0, tg), :] = pltpu.einshape("tpc->ptc", stacked)

    def build_slice():
        for g in range(n_groups):
            group_store(g, group_grids(g))

    def activations(with_build=False):
        CW = MXU_DIM
        n_chunks = TE // CW
        hb = hb_ref[...]
        i1 = i1n_ref[...]
        i2 = i2n_ref[...]
        acc = a_scr[...]
        events = {}
        if with_build:
            for e in range(2 * n_groups):
                events.setdefault(min(e * n_chunks // (2 * n_groups), n_chunks - 1), []).append(e)
        pending = {}

        def pick(acc, k, act):
            for c in range(CW // n_keys):
                r = j * (TE // n_keys) + k * (CW // n_keys) + c
                acc = jnp.where(i1 == r, jnp.take_along_axis(act[:, c * n_keys:(c + 1) * n_keys], i2, axis=1), acc)
            return acc

        prev = None
        for k in range(n_chunks):
            act = _dot(hb, ut_ref[:, k * CW:(k + 1) * CW])
            for e in events.get(k, []):
                if e % 2 == 0:
                    pending[e // 2] = group_grids(e // 2)
                else:
                    group_store(e // 2, pending.pop(e // 2))
            if prev is not None:
                acc = pick(acc, k - 1, prev)
            prev = act
        a_scr[...] = pick(acc, n_chunks - 1, prev)

    @pl.when((j == 0) & (t < NT))
    def _():
        a_scr[...] = jnp.zeros(a_scr.shape, F32)

    @pl.when((j < NE) & (t > 0) & (t < NT))
    def _():
        activations(with_build=True)

    @pl.when((j < NE) & (t == 0))
    def _():
        activations()

    @pl.when((j < NE) & (t == NT))
    def _():
        build_slice()

    @pl.when((j == NE - 1) & (t < NT))
    def _():
        a = a_scr[...]
        w_scr[...] = gn_ref[...] * (0.5 * a * (1.0 + lax.erf(a * math.sqrt(0.5))))

    @pl.when((j == NE) & (t > 0))
    def _():
        y_scr[...] = jnp.zeros(y_scr.shape, F32)

    @pl.when((j >= NE) & (t > 0))
    def _():
        jj = j - NE
        y = y_scr[...]
        for c in range(TE // W2):
            y = y + _dot(hbuf_scr[jj * (TE // W2) + c], v_ref[c * W2:(c + 1) * W2, :])
        y_scr[...] = y

    @pl.when((j == 2 * NE - 1) & (t > 0))
    def _():
        o_ref[...] = x_ref[...] + g2_ref[...] * y_scr[...]


def peer_mix(hb, i1, i2, g, ut_tab, v_tab, x, g2, *, tm, te, tiles_per_group, n_keys):
    N, D = x.shape
    E = v_tab.shape[0]
    J = i1.shape[1]
    NE = E // te
    NT = N // tm
    R = g2.shape[1]
    tg = 16
    assert te % (2 * n_keys) == 0 and n_keys == LANES and tm % (NE * tg) == 0
    nxt = lambda t, j: (jnp.minimum(t, NT - 1), 0)
    cur = lambda t, j: (jnp.maximum(t - 1, 0), 0)
    return pl.pallas_call(
        functools.partial(_peer_mix_kernel, n_keys=n_keys, tg=tg),
        grid=(NT + 1, 2 * NE),
        in_specs=[pl.BlockSpec((tm, D), nxt),
                  pl.BlockSpec((tm, J), nxt), pl.BlockSpec((tm, J), nxt), pl.BlockSpec((tm, J), nxt),
                  pl.BlockSpec((tm, J), cur), pl.BlockSpec((tm, J), cur),
                  pl.BlockSpec((D, te), lambda t, j: (0, jnp.where(t < NT, jnp.minimum(j, NE - 1), NE - 1))),
                  pl.BlockSpec((te, D), lambda t, j: (jnp.where(t > 0, jnp.maximum(j - NE, 0), 0), 0)),
                  pl.BlockSpec((tm, D), cur),
                  pl.BlockSpec((None, R, D), lambda t, j: (jnp.maximum(t - 1, 0) // tiles_per_group, 0, 0))],
        out_specs=pl.BlockSpec((tm, D), cur),
        out_shape=jax.ShapeDtypeStruct((N, D), F32),
        scratch_shapes=[pltpu.VMEM((tm, J), F32), pltpu.VMEM((tm, J), F32),
                        pltpu.VMEM((n_keys // 2, tm, 2 * n_keys), BF16),
                        pltpu.VMEM((tm, D), F32)],
        compiler_params=_params("arbitrary", "arbitrary"),
        name="peer_mix",
    )(hb, i1, i2, g, i1, i2, ut_tab, v_tab, x, g2)


def _trunk(x3, mods, past, params, tables, *, tm):
    (norm_mix, norm_ffn, fox_w_in, fox_b_f, fox_q_norm, fox_k_norm, fox_w_out,
     ml_w_in, ml_b_i, ml_b_f, ml_h_norm, ml_w_out, peer_w_q, peer_sub_k1, peer_sub_k2) = params
    ut_bf, v_bf = tables
    B, T, D = x3.shape
    N = B * T
    depth = norm_mix.shape[0]
    n_fox_heads = fox_b_f.shape[1]
    n_keys = peer_sub_k1.shape[2]
    x = x3.reshape(N, D)
    if T % tm == 0:
        tpg, rows = T // tm, 1
        expand = lambda a: a.reshape(B, 1, D)
    else:
        assert N == tm
        tpg, rows = 1, N
        expand = lambda a: jnp.repeat(a, T, axis=0).reshape(1, N, D)
    new_k, new_v, new_lf, new_C, new_n, new_m = [], [], [], [], [], []
    for layer in range(depth):
        sh1, sc1, g1, sh2, sc2, g2 = [expand(a) for a in jnp.split(mods[layer], 6, axis=-1)]
        j = layer // 2
        if layer % 2 == 0:
            q, k, v, kb, vb, sg, lf = fox_project(
                x, sc1, sh1, norm_mix[layer], fox_w_in[j], fox_b_f[j], fox_q_norm[j], fox_k_norm[j],
                tm=tm, tiles_per_group=tpg)
            if past is None:
                F = seg_cumsum(lf.reshape(B, T, n_fox_heads).transpose(0, 2, 1), T).transpose(0, 2, 1)
                o = fox_attend_prompt(q.reshape(B, T, D), kb.reshape(B, T, D), vb.reshape(B, T, D), F,
                                      n_heads=n_fox_heads)
            else:
                cache_kt, cache_vt, cache_cf, _, _, _, page_table = past
                o = fox_attend_sample(q.reshape(B, T, D), kb.reshape(B, T, D), vb.reshape(B, T, D),
                                      lf.reshape(B, T, n_fox_heads), cache_kt, cache_vt, cache_cf,
                                      page_table, j, n_heads=n_fox_heads)
            x = out_project(o.reshape(N, D), sg, fox_w_out[j], x, g1, jnp.ones((D,), F32),
                            tm=tm, tiles_per_group=tpg, head_norm=False)
            new_k.append(k.reshape(B, T, n_fox_heads, -1))
            new_v.append(v.reshape(B, T, n_fox_heads, -1))
            new_lf.append(lf.reshape(B, T, n_fox_heads))
        else:
            n_ml_heads = ml_b_i.shape[1]
            v_width = ml_w_out.shape[1]
            q, k, v, so, il, fl = ml_project(x, sc1, sh1, norm_mix[layer], ml_w_in[j], ml_b_i[j], ml_b_f[j],
                                             tm=tm, tiles_per_group=tpg, v_width=v_width)
            QW = q.shape[1]
            dk = QW // n_ml_heads
            dv = v_width // n_ml_heads
            q3, k3, v3 = q.reshape(B, T, QW), k.reshape(B, T, QW), v.reshape(B, T, v_width)
            il3, fl3 = il.reshape(B, T, n_ml_heads), fl.reshape(B, T, n_ml_heads)
            if past is None:
                C0 = jnp.zeros((B, n_ml_heads, dk, dv), F32)
                n0 = jnp.zeros((B, n_ml_heads, dk), F32)
                m0 = jnp.zeros((B, n_ml_heads), F32)
                hh, C, n_, m_ = mlstm_chunked(q3, k3, v3, il3, fl3, C0, n0, m0, 64)
            else:
                _, _, _, state_C, state_n, state_m, _ = past
                Tp = 16
                pad3 = ((0, 0), (0, Tp - T), (0, 0))
                hh, C, n_, m_ = mlstm_chunked(
                    jnp.pad(q3, pad3), jnp.pad(k3, pad3), jnp.pad(v3, pad3),
                    jnp.pad(il3, pad3, constant_values=-1e30), jnp.pad(fl3, pad3),
                    state_C[j], state_n[j], state_m[j], Tp)
                hh = hh[:, :T]
            x = out_project(hh.reshape(N, v_width), so, ml_w_out[j], x, g1, ml_h_norm[j].reshape(-1),
                            tm=tm, tiles_per_group=tpg, head_norm=True)
            new_C.append(C)
            new_n.append(n_)
            new_m.append(m_)
        hb, e, g = peer_retrieve(x, sc2, sh2, norm_ffn[layer], peer_w_q[layer], peer_sub_k1[layer],
                                 peer_sub_k2[layer], tm=tm, tiles_per_group=tpg)
        eT = e.T
        mix_tm = 2 * tm if tpg % 2 == 0 else tm
        x = peer_mix(hb, eT // n_keys, eT % n_keys, g.T, ut_bf[layer], v_bf[layer], x, g2,
                     tm=mix_tm, te=2048, tiles_per_group=tpg * tm // mix_tm, n_keys=n_keys)
    return (x.reshape(B, T, D), jnp.stack(new_k), jnp.stack(new_v), jnp.stack(new_lf),
            jnp.stack(new_C), jnp.stack(new_n), jnp.stack(new_m))


def kernel(x_prompt, x_sample, cache_k, cache_v, cache_logf, state_C, state_n, state_m, page_table,
           c_prompt, c_sample, ada_w, ada_b, norm_mix, norm_ffn, fox_w_in, fox_b_f, fox_q_norm, fox_k_norm,
           fox_w_out, ml_w_in, ml_b_i, ml_b_f, ml_h_norm, ml_w_out, peer_w_q, peer_sub_k1, peer_sub_k2,
           peer_u, peer_v):
    params = (norm_mix, norm_ffn, fox_w_in, fox_b_f, fox_q_norm, fox_k_norm, fox_w_out,
              ml_w_in, ml_b_i, ml_b_f, ml_h_norm, ml_w_out, peer_w_q, peer_sub_k1, peer_sub_k2)
    tables = (peer_u.astype(BF16).transpose(0, 2, 1), peer_v.astype(BF16))
    Bp = c_prompt.shape[0]
    mods = adaln(jnp.concatenate([c_prompt, c_sample], axis=0), ada_w, ada_b)
    y_p, k_p, v_p, lf_p, C_p, n_p, m_p = _trunk(x_prompt, mods[:, :Bp], None, params, tables, tm=256)
    past = prepare_cache(cache_k, cache_v, cache_logf) + (state_C, state_n, state_m, page_table)
    y_s, k_s, v_s, lf_s, C_s, n_s, m_s = _trunk(x_sample, mods[:, Bp:], past, params, tables,
                                               tm=x_sample.shape[0] * x_sample.shape[1])
    return (y_p, y_s, k_p, v_p, lf_p, C_p, n_p, m_p, k_s, v_s, lf_s, C_s, n_s, m_s)
```

```python
import functools
import math

import jax
import jax.numpy as jnp
from jax import lax
from jax.experimental import pallas as pl
from jax.experimental.pallas import tpu as pltpu

F32 = jnp.float32
BF16 = jnp.bfloat16
I32 = jnp.int32

RMS_EPS = 1e-6
LANES = 128
MXU_DIM = 256
VMEM_LIMIT = 52 * 1024 * 1024
PEER_TOPK = 16
PAGE_SIZE = 128
NEG_INF = float("-inf")


def _dot(a, b):
    return jnp.dot(a, b, preferred_element_type=F32)


def _dot_nt(a, b):
    return lax.dot_general(a, b, (((1,), (1,)), ((), ())), preferred_element_type=F32)


def _dot_tn(a, b):
    return lax.dot_general(a, b, (((0,), (0,)), ((), ())), preferred_element_type=F32)


def _split2(a):
    hi = a.astype(BF16)
    lo = (a - hi.astype(F32)).astype(BF16)
    return hi, lo


def _dot3(a, b):
    ah, al = _split2(a)
    bh, bl = _split2(b)
    return _dot(ah, bh) + _dot(ah, bl) + _dot(al, bh)


def _modnorm(x, gain, scale, shift):
    ms = jnp.mean(x * x, axis=-1, keepdims=True)
    return x * lax.rsqrt(ms + RMS_EPS) * gain * (1.0 + scale) + shift


def _log_sigmoid(z):
    return jnp.minimum(z, 0.0) - jnp.log1p(jnp.exp(-jnp.abs(z)))


def _params(*sem):
    return pltpu.CompilerParams(dimension_semantics=sem, vmem_limit_bytes=VMEM_LIMIT)


def _mod_spec(rows, d, tiles_per_group):
    return pl.BlockSpec((None, rows, d), lambda i: (i // tiles_per_group, 0, 0))


def _adaln_kernel(c_ref, w_ref, b_ref, o_ref):
    c = c_ref[...]
    o_ref[...] = _dot3(c * jax.nn.sigmoid(c), w_ref[...]) + b_ref[...]


def adaln(c, ada_w, ada_b, tn=1536):
    L, D, D6 = ada_w.shape
    R = c.shape[0]
    return pl.pallas_call(
        _adaln_kernel,
        grid=(L, D6 // tn),
        in_specs=[pl.BlockSpec((R, D), lambda l, j: (0, 0)),
                  pl.BlockSpec((None, D, tn), lambda l, j: (l, 0, j)),
                  pl.BlockSpec((None, 1, tn), lambda l, j: (l, 0, j))],
        out_specs=pl.BlockSpec((None, R, tn), lambda l, j: (l, 0, j)),
        out_shape=jax.ShapeDtypeStruct((L, R, D6), F32),
        compiler_params=_params("arbitrary", "arbitrary"),
        name="adaln",
    )(c, ada_w, ada_b.reshape(L, 1, D6))


def _segcumsum_kernel(x_ref, o_ref, *, seg):
    C, T = x_ref.shape
    lane = lax.broadcasted_iota(I32, (C, LANES), 1)
    w = min(seg, LANES)
    carry = None
    for blk in range(T // LANES):
        x = x_ref[:, blk * LANES:(blk + 1) * LANES]
        k = 1
        while k < w:
            x = x + jnp.where((lane & (w - 1)) >= k, pltpu.roll(x, k, 1), 0.0)
            k *= 2
        if seg > LANES:
            if blk % (seg // LANES) != 0:
                x = x + carry
            carry = x[:, LANES - 1:LANES]
        o_ref[:, blk * LANES:(blk + 1) * LANES] = x


def seg_cumsum(x, seg):
    R, C, T = x.shape
    assert T % LANES == 0 and (seg & (seg - 1)) == 0
    assert T % seg == 0 and (seg <= LANES or seg % LANES == 0)
    return pl.pallas_call(
        functools.partial(_segcumsum_kernel, seg=seg),
        grid=(R,),
        in_specs=[pl.BlockSpec((None, C, T), lambda r: (r, 0, 0))],
        out_specs=pl.BlockSpec((None, C, T), lambda r: (r, 0, 0)),
        out_shape=jax.ShapeDtypeStruct((R, C, T), F32),
        compiler_params=_params("arbitrary"),
        name="seg_cumsum",
    )(x)


def _pair_headnorm(blk, gain2, hd):
    lane = lax.broadcasted_iota(I32, blk.shape, 1)
    lo = lane < hd
    sq = blk * blk
    s_lo = jnp.sum(jnp.where(lo, sq, 0.0), axis=-1, keepdims=True)
    s_hi = jnp.sum(jnp.where(lo, 0.0, sq), axis=-1, keepdims=True)
    inv = lax.rsqrt(jnp.where(lo, s_lo, s_hi) * (1.0 / hd) + RMS_EPS)
    return blk * inv * gain2


def _fox_proj_kernel(x_ref, sc_ref, sh_ref, gain_ref, w_ref, wfh_ref, wfl_ref, bf_ref, qg_ref, kg_ref,
                     q_ref, k_ref, v_ref, kb_ref, vb_ref, sg_ref, lf_ref, *, n_heads):
    D = x_ref.shape[1]
    hd = D // n_heads
    h = _modnorm(x_ref[...], gain_ref[...], sc_ref[...], sh_ref[...])
    hh, hl = _split2(h)
    qk_scale = hd ** -0.5
    q = _dot(hh, w_ref[:, 0:D])
    for c in range(D // LANES):
        sl = slice(c * LANES, (c + 1) * LANES)
        q_ref[:, sl] = (_pair_headnorm(q[:, sl], qg_ref[...], hd) * qk_scale).astype(BF16)
    k = _dot(hh, w_ref[:, D:2 * D])
    for c in range(D // LANES):
        sl = slice(c * LANES, (c + 1) * LANES)
        kn = _pair_headnorm(k[:, sl], kg_ref[...], hd)
        k_ref[:, sl] = kn
        kb_ref[:, sl] = kn.astype(BF16)
    v = _dot(hh, w_ref[:, 2 * D:3 * D])
    v_ref[...] = v
    vb_ref[...] = v.astype(BF16)
    sg_ref[...] = jax.nn.sigmoid(_dot(hh, w_ref[:, 3 * D:4 * D]))
    f = _dot(hh, wfh_ref[...]) + _dot(hh, wfl_ref[...]) + _dot(hl, wfh_ref[...])
    lf_ref[...] = _log_sigmoid(f[:, 0:n_heads] + bf_ref[...])


def fox_project(x, scale, shift, gain, w_in, b_f, q_gain, k_gain, *, tm, tiles_per_group):
    N, D = x.shape
    H = b_f.shape[0]
    hd = D // H
    assert 2 * hd == LANES
    R = scale.shape[1]
    w_main = jnp.concatenate([w_in[:, :3 * D], w_in[:, 3 * D + H:]], axis=1).astype(BF16)
    w_f = jnp.pad(w_in[:, 3 * D:3 * D + H], ((0, 0), (0, LANES - H)))
    wfh = w_f.astype(BF16)
    wfl = (w_f - wfh.astype(F32)).astype(BF16)
    const = lambda i: (0, 0)
    row = lambda i: (i, 0)
    outs = pl.pallas_call(
        functools.partial(_fox_proj_kernel, n_heads=H),
        grid=(N // tm,),
        in_specs=[pl.BlockSpec((tm, D), row),
                  _mod_spec(R, D, tiles_per_group), _mod_spec(R, D, tiles_per_group),
                  pl.BlockSpec((1, D), const),
                  pl.BlockSpec((D, 4 * D), const),
                  pl.BlockSpec((D, LANES), const), pl.BlockSpec((D, LANES), const),
                  pl.BlockSpec((1, H), const),
                  pl.BlockSpec((1, LANES), const), pl.BlockSpec((1, LANES), const)],
        out_specs=[pl.BlockSpec((tm, D), row)] * 6 + [pl.BlockSpec((tm, H), row)],
        out_shape=[jax.ShapeDtypeStruct((N, D), BF16),
                   jax.ShapeDtypeStruct((N, D), F32), jax.ShapeDtypeStruct((N, D), F32),
                   jax.ShapeDtypeStruct((N, D), BF16), jax.ShapeDtypeStruct((N, D), BF16),
                   jax.ShapeDtypeStruct((N, D), F32),
                   jax.ShapeDtypeStruct((N, H), F32)],
        compiler_params=_params("arbitrary"),
        name="fox_project",
    )(x, scale, shift, gain.reshape(1, D), w_main, wfh, wfl, b_f.reshape(1, H),
      jnp.tile(q_gain, 2).reshape(1, LANES), jnp.tile(k_gain, 2).reshape(1, LANES))
    return outs


def _fox_attn_kernel(q_ref, k_ref, v_ref, fc_ref, fr_ref, o_ref, *, hd, n_sub, hps):
    TQ = q_ref.shape[0]
    SQ = TQ // n_sub
    qi = pl.program_id(2)
    lane = lax.broadcasted_iota(I32, (SQ, LANES), 1)
    row = lax.broadcasted_iota(I32, (SQ, TQ), 0)
    col = lax.broadcasted_iota(I32, (SQ, TQ), 1)
    chains = [(hh, sb) for hh in range(hps) for sb in range(n_sub)]
    nch = len(chains)
    qms, fqs = [], []
    for hh, sb in chains:
        pb, ph = hh // 2, hh % 2
        q = q_ref[sb * SQ:(sb + 1) * SQ, pb * LANES:(pb + 1) * LANES]
        qms.append(jnp.where((lane >= ph * hd) & (lane < (ph + 1) * hd), q, jnp.zeros_like(q)))
        fqs.append(fc_ref[sb * SQ:(sb + 1) * SQ, hh:hh + 1])

    def block_scores(j):
        start = pl.multiple_of(j * TQ, TQ)
        kjs = [k_ref[pl.ds(start, TQ), pb * LANES:(pb + 1) * LANES] for pb in range(hps // 2)]
        return tuple(_dot_nt(qms[c], kjs[hh // 2]) + (fqs[c] - fr_ref[hh:hh + 1, pl.ds(start, TQ)])
                     for c, (hh, sb) in enumerate(chains))

    def update(j, state, scores, masked):
        start = pl.multiple_of(j * TQ, TQ)
        vjs = [v_ref[pl.ds(start, TQ), pb * LANES:(pb + 1) * LANES] for pb in range(hps // 2)]
        new = []
        for c, (hh, sb) in enumerate(chains):
            m, l, acc = state[c]
            s = jnp.where(col <= row + sb * SQ, scores[c], NEG_INF) if masked else scores[c]
            m_new = jnp.maximum(m, jnp.max(s, axis=-1, keepdims=True))
            alpha = jnp.exp(m - m_new)
            p = jnp.exp(s - m_new)
            new.append((m_new, alpha * l + jnp.sum(p, axis=-1, keepdims=True),
                        alpha * acc + _dot(p.astype(BF16), vjs[hh // 2])))
        return tuple(new)

    def step(j, carry):
        state, scores = carry
        nxt = block_scores(j + 1)
        return update(j, state, scores, False), nxt

    init = tuple((jnp.full((SQ, 1), NEG_INF, F32), jnp.zeros((SQ, 1), F32), jnp.zeros((SQ, LANES), F32))
                 for _ in chains)
    state, scores = lax.fori_loop(0, qi, step, (init, block_scores(0)))
    final = update(qi, state, scores, True)
    for pb in range(hps // 2):
        for sb in range(n_sub):
            even = final[(2 * pb) * n_sub + sb]
            odd = final[(2 * pb + 1) * n_sub + sb]
            o_ref[sb * SQ:(sb + 1) * SQ, pb * LANES:(pb + 1) * LANES] = jnp.where(
                lane < hd, even[2] / even[1], odd[2] / odd[1])


def fox_attend_prompt(q, kb, vb, F, *, n_heads, tq=256, hps=2, n_sub=2):
    B, S, D = q.shape
    hd = D // n_heads
    assert 2 * hd == LANES and hps % 2 == 0 and n_heads % hps == 0
    HG = n_heads // hps
    W = hps * hd
    fcol = F.reshape(B, S, HG, hps).transpose(0, 2, 1, 3)
    frow = fcol.transpose(0, 1, 3, 2)
    return pl.pallas_call(
        functools.partial(_fox_attn_kernel, hd=hd, n_sub=n_sub, hps=hps),
        grid=(B, HG, S // tq),
        in_specs=[pl.BlockSpec((None, tq, W), lambda b, h, i: (b, i, h)),
                  pl.BlockSpec((None, S, W), lambda b, h, i: (b, 0, h)),
                  pl.BlockSpec((None, S, W), lambda b, h, i: (b, 0, h)),
                  pl.BlockSpec((None, None, tq, hps), lambda b, h, i: (b, h, i, 0)),
                  pl.BlockSpec((None, None, hps, S), lambda b, h, i: (b, h, 0, 0))],
        out_specs=pl.BlockSpec((None, tq, W), lambda b, h, i: (b, i, h)),
        out_shape=jax.ShapeDtypeStruct((B, S, D), F32),
        compiler_params=_params("arbitrary", "arbitrary", "arbitrary"),
        name="fox_attend_prompt",
    )(q, kb, vb, fcol, frow)


def _fox_decode_kernel(pt_ref, qbd_ref, *refs, n_heads, pp):
    del pt_ref
    kt_refs, vt_refs, cf_refs = refs[:pp], refs[pp:2 * pp], refs[2 * pp:3 * pp]
    kn_ref, vn_ref, cn_ref, o_ref, m_scr, l_scr, acc_scr, carry_scr = refs[3 * pp:]
    H = n_heads
    step = pl.program_id(1)
    n_steps = pl.num_programs(1)
    R, D = qbd_ref.shape
    T = R // H
    P = cf_refs[0].shape[1]

    @pl.when(step == 0)
    def _():
        m_scr[...] = jnp.full(m_scr.shape, NEG_INF, F32)
        l_scr[...] = jnp.zeros(l_scr.shape, F32)
        acc_scr[...] = jnp.zeros(acc_scr.shape, F32)
        carry_scr[...] = jnp.zeros(carry_scr.shape, F32)

    qbd = qbd_ref[...]

    def merge(m_i, l_i, acc_i):
        m = m_scr[...]
        m_new = jnp.maximum(m, m_i)
        a_old = jnp.exp(m - m_new)
        a_new = jnp.exp(m_i - m_new)
        l_scr[...] = a_old * l_scr[...] + a_new * l_i
        acc_scr[...] = a_old * acc_scr[...] + a_new * acc_i
        m_scr[...] = m_new

    def scores(s, fk):
        return s - jnp.concatenate([fk] * T, axis=0)

    carry = carry_scr[...]
    ss = []
    for i in range(pp):
        cf = cf_refs[i][...]
        ss.append(scores(_dot(qbd, kt_refs[i][...].astype(BF16)), cf + carry))
        carry = carry + cf[:, P - 1:P]
    carry_scr[...] = carry
    m_s = jnp.max(ss[0], axis=-1, keepdims=True)
    for s in ss[1:]:
        m_s = jnp.maximum(m_s, jnp.max(s, axis=-1, keepdims=True))
    l_s = jnp.zeros_like(m_s)
    acc_s = jnp.zeros(acc_scr.shape, F32)
    for i, s in enumerate(ss):
        pr = jnp.exp(s - m_s)
        l_s = l_s + jnp.sum(pr, axis=-1, keepdims=True)
        acc_s = acc_s + _dot_nt(pr.astype(BF16), vt_refs[i][...].astype(BF16))
    merge(m_s, l_s, acc_s)

    @pl.when(step == n_steps - 1)
    def _():
        s2 = scores(_dot_nt(qbd, kn_ref[...]), cn_ref[...] + carry_scr[...])
        key = lax.broadcasted_iota(I32, s2.shape, 1)
        t = lax.broadcasted_iota(I32, s2.shape, 0) // H
        s2 = jnp.where(key <= t, s2, NEG_INF)
        m2 = jnp.max(s2, axis=-1, keepdims=True)
        p2 = jnp.exp(s2 - m2)
        merge(m2, jnp.sum(p2, axis=-1, keepdims=True), _dot(p2.astype(BF16), vn_ref[...]))
        out = acc_scr[...] / l_scr[...]
        rh = lax.broadcasted_iota(I32, (R, D), 0) % H
        ch = lax.broadcasted_iota(I32, (R, D), 1) // (D // H)
        out = jnp.where(rh == ch, out, 0.0)
        o_ref[...] = jnp.sum(out.reshape(T, H, D), axis=1)


def fox_attend_sample(q, kb_new, vb_new, lf_new, cache_kt, cache_vt, cache_cf, page_table, j, *, n_heads):
    Bd, T, D = q.shape
    H = n_heads
    hd = D // H
    n_pages = page_table.shape[1]
    n_fox, n_pool, _, P = cache_cf.shape
    R = T * H
    pp = next(c for c in (16, 8, 4, 2, 1) if n_pages % c == 0)
    assert T <= P
    q4 = q.reshape(Bd, T, H, hd)
    eye = jnp.eye(H, dtype=q.dtype)
    qbd = (q4[:, :, :, None, :] * eye[None, None, :, :, None]).reshape(Bd, R, D)
    pad = ((0, 0), (0, P - T), (0, 0))
    kn = jnp.pad(kb_new, pad)
    vn = jnp.pad(vb_new, pad)
    cn = seg_cumsum(jnp.pad(lf_new.transpose(0, 2, 1), ((0, 0), (0, 0), (0, P - T))), P)
    pt = page_table.reshape(-1).astype(I32)

    def page_spec(shape, i):
        return pl.BlockSpec((None, None) + shape, lambda b, s, pt: (j, pt[b * n_pages + s * pp + i], 0, 0))

    per_seq = lambda shape: pl.BlockSpec((None,) + shape, lambda b, s, pt: (b, 0, 0))
    grid_spec = pltpu.PrefetchScalarGridSpec(
        num_scalar_prefetch=1,
        grid=(Bd, n_pages // pp),
        in_specs=([per_seq((R, D))]
                  + [page_spec((D, P), i) for i in range(pp)]
                  + [page_spec((D, P), i) for i in range(pp)]
                  + [page_spec((H, P), i) for i in range(pp)]
                  + [per_seq((P, D)), per_seq((P, D)), per_seq((H, P))]),
        out_specs=per_seq((T, D)),
        scratch_shapes=[pltpu.VMEM((R, 1), F32), pltpu.VMEM((R, 1), F32),
                        pltpu.VMEM((R, D), F32), pltpu.VMEM((H, 1), F32)],
    )
    return pl.pallas_call(
        functools.partial(_fox_decode_kernel, n_heads=H, pp=pp),
        grid_spec=grid_spec,
        out_shape=jax.ShapeDtypeStruct((Bd, T, D), F32),
        compiler_params=_params("arbitrary", "arbitrary"),
        name="fox_attend_sample",
    )(pt, qbd, *([cache_kt] * pp), *([cache_vt] * pp), *([cache_cf] * pp), kn, vn, cn)


def prepare_cache(cache_k, cache_v, cache_logf):
    n_fox, n_pool, P, H, hd = cache_k.shape
    kt = cache_k.transpose(0, 1, 3, 4, 2).reshape(n_fox, n_pool, H * hd, P)
    vt = cache_v.transpose(0, 1, 3, 4, 2).reshape(n_fox, n_pool, H * hd, P)
    rows = n_fox * n_pool * H
    group = 2048 if rows % 2048 == 0 else H
    cf = seg_cumsum(cache_logf.transpose(0, 1, 3, 2).reshape(rows // group, group, P), P)
    return kt, vt, cf.reshape(n_fox, n_pool, H, P)


def _outproj_kernel(a_ref, gate_ref, w_ref, x_ref, g_ref, hn_ref, o_ref, *, head_norm):
    a = a_ref[...]
    D = a.shape[1]
    if head_norm:
        blocks = []
        for c in range(D // LANES):
            sl = slice(c * LANES, (c + 1) * LANES)
            blk = a[:, sl]
            ms = jnp.mean(blk * blk, axis=-1, keepdims=True)
            blocks.append(blk * lax.rsqrt(ms + RMS_EPS) * hn_ref[:, sl])
        a = jnp.concatenate(blocks, axis=-1)
    y = (a * gate_ref[...]).astype(BF16)
    o_ref[...] = x_ref[...] + g_ref[...] * _dot(y, w_ref[...])


def out_project(a, gate, w_out, x, g, head_gain, *, tm, tiles_per_group, head_norm):
    N, D = x.shape
    R = g.shape[1]
    row = lambda i: (i, 0)
    const = lambda i: (0, 0)
    return pl.pallas_call(
        functools.partial(_outproj_kernel, head_norm=head_norm),
        grid=(N // tm,),
        in_specs=[pl.BlockSpec((tm, D), row), pl.BlockSpec((tm, D), row),
                  pl.BlockSpec((D, D), const), pl.BlockSpec((tm, D), row),
                  _mod_spec(R, D, tiles_per_group), pl.BlockSpec((1, D), const)],
        out_specs=pl.BlockSpec((tm, D), row),
        out_shape=jax.ShapeDtypeStruct((N, D), F32),
        compiler_params=_params("arbitrary"),
        name="out_project",
    )(a, gate, w_out.astype(BF16), x, g, head_gain.reshape(1, D))


def _ml_proj_kernel(x_ref, sc_ref, sh_ref, gain_ref, w_ref, wgh_ref, wgl_ref, bi_ref, bf_ref,
                    q_ref, k_ref, v_ref, so_ref, il_ref, fl_ref, *, n_heads, qk_width):
    D = x_ref.shape[1]
    QW = qk_width
    h = _modnorm(x_ref[...], gain_ref[...], sc_ref[...], sh_ref[...])
    hh, hl = _split2(h)
    k_scale = (QW // n_heads) ** -0.5
    q_ref[...] = _dot(hh, w_ref[:, 0:QW]).astype(BF16)
    k_ref[...] = (_dot(hh, w_ref[:, QW:2 * QW]) * k_scale).astype(BF16)
    v_ref[...] = _dot(hh, w_ref[:, 2 * QW:2 * QW + D])
    so_ref[...] = jax.nn.sigmoid(_dot(hh, w_ref[:, 2 * QW + D:2 * QW + 2 * D]))
    gts = _dot(hh, wgh_ref[...]) + _dot(hh, wgl_ref[...]) + _dot(hl, wgh_ref[...])
    il_ref[...] = gts[:, 0:n_heads] + bi_ref[...]
    fl_ref[...] = _log_sigmoid(gts[:, n_heads:2 * n_heads] + bf_ref[...])


def ml_project(x, scale, shift, gain, w_in, b_i, b_f, *, tm, tiles_per_group, v_width):
    N, D = x.shape
    H = b_i.shape[0]
    QW = (w_in.shape[1] - 2 * v_width - 2 * H) // 2
    R = scale.shape[1]
    g0 = 2 * QW + v_width
    w_main = jnp.concatenate([w_in[:, :g0], w_in[:, g0 + 2 * H:]], axis=1).astype(BF16)
    w_g = jnp.pad(w_in[:, g0:g0 + 2 * H], ((0, 0), (0, LANES - 2 * H)))
    wgh = w_g.astype(BF16)
    wgl = (w_g - wgh.astype(F32)).astype(BF16)
    const = lambda i: (0, 0)
    row = lambda i: (i, 0)
    WM = w_main.shape[1]
    return pl.pallas_call(
        functools.partial(_ml_proj_kernel, n_heads=H, qk_width=QW),
        grid=(N // tm,),
        in_specs=[pl.BlockSpec((tm, D), row),
                  _mod_spec(R, D, tiles_per_group), _mod_spec(R, D, tiles_per_group),
                  pl.BlockSpec((1, D), const),
                  pl.BlockSpec((D, WM), const),
                  pl.BlockSpec((D, LANES), const), pl.BlockSpec((D, LANES), const),
                  pl.BlockSpec((1, H), const), pl.BlockSpec((1, H), const)],
        out_specs=[pl.BlockSpec((tm, QW), row), pl.BlockSpec((tm, QW), row),
                   pl.BlockSpec((tm, v_width), row), pl.BlockSpec((tm, v_width), row),
                   pl.BlockSpec((tm, H), row), pl.BlockSpec((tm, H), row)],
        out_shape=[jax.ShapeDtypeStruct((N, QW), BF16), jax.ShapeDtypeStruct((N, QW), BF16),
                   jax.ShapeDtypeStruct((N, v_width), F32), jax.ShapeDtypeStruct((N, v_width), F32),
                   jax.ShapeDtypeStruct((N, H), F32), jax.ShapeDtypeStruct((N, H), F32)],
        compiler_params=_params("arbitrary"),
        name="ml_project",
    )(x, scale, shift, gain.reshape(1, D), w_main, wgh, wgl, b_i.reshape(1, H), b_f.reshape(1, H))


def _mlstm_kernel(q_ref, k_ref, v_ref, ic_ref, ir_ref, bc_ref, br_ref, c0_ref, n0_ref, m0_ref,
                  h_ref, c_ref, n_ref, m_ref, *, n_heads, dk):
    nbb, L = q_ref.shape[0], q_ref.shape[1]
    ci = pl.program_id(1)

    @pl.when(ci == 0)
    def _():
        c_ref[...] = c0_ref[...]
        n_ref[...] = n0_ref[...]
        m_ref[...] = m0_ref[...]

    tri = lax.broadcasted_iota(I32, (L, L), 1) <= lax.broadcasted_iota(I32, (L, L), 0)
    lane = lax.broadcasted_iota(I32, (L, LANES), 1)
    lane1 = lax.broadcasted_iota(I32, (1, LANES), 1)
    rsub = lax.broadcasted_iota(I32, (LANES, LANES), 0)
    dv = LANES
    m_in = [m_ref[bb] for bb in range(nbb)]
    units = [(bb, h) for bb in range(nbb) for h in range(n_heads)]
    pairs = [(bb, pr) for bb in range(nbb) for pr in range(n_heads // 2)]
    qp = {(bb, pr): q_ref[bb, :, pr * LANES:(pr + 1) * LANES] for bb, pr in pairs}
    kp = {(bb, pr): k_ref[bb, :, pr * LANES:(pr + 1) * LANES] for bb, pr in pairs}
    c_in = {u: c_ref[u[0], u[1]] for u in pairs}
    n_in = {u: n_ref[u[0], u[1]] for u in pairs}
    sel = [(lane >= hh * dk) & (lane < (hh + 1) * dk) for hh in range(2)]
    qm = {(bb, h): jnp.where(sel[h % 2], qp[(bb, h // 2)], jnp.zeros_like(qp[(bb, h // 2)])) for bb, h in units}
    qk = {(bb, h): _dot_nt(qm[(bb, h)], kp[(bb, h // 2)]) for bb, h in units}
    qc = {(bb, h): _dot(qm[(bb, h)], c_in[(bb, h // 2)].astype(BF16)) for bb, h in units}
    st = {}
    for bb, h in units:
        b_l = bc_ref[bb][:, h:h + 1]
        b_s = br_ref[bb][h:h + 1, :]
        ig_s = ir_ref[bb][h:h + 1, :]
        ig_l = ic_ref[bb][:, h:h + 1]
        m = m_in[bb][:, h:h + 1]
        dm = jnp.where(tri, b_l - b_s + ig_s, NEG_INF)
        inter = b_l + m
        m_t = jnp.maximum(inter, jnp.max(dm, axis=-1, keepdims=True))
        w_inter = jnp.exp(inter - m_t)
        a = jnp.exp(dm - m_t) * qk[(bb, h)]
        b_last = b_l[L - 1:L, :]
        m_new = jnp.maximum(b_last + m, jnp.max(b_last - b_s + ig_s, axis=-1, keepdims=True))
        wk = jnp.exp(b_last - b_l + ig_l - m_new)
        decay = jnp.exp(b_last + m - m_new)
        vh = v_ref[bb, :, h * dv:(h + 1) * dv]
        st[(bb, h)] = (a, m_t, w_inter, m_new, wk, decay, vh)
    av = {u: _dot(st[u][0].astype(BF16), st[u][6].astype(BF16)) for u in units}
    upd = {(bb, h): _dot_tn(kp[(bb, h // 2)], (st[(bb, h)][4] * st[(bb, h)][6]).astype(BF16)) for bb, h in units}
    for bb, pr in pairs:
        c_pair, n_pair = c_in[(bb, pr)], n_in[(bb, pr)]
        qpf = qp[(bb, pr)].astype(F32)
        kpf = kp[(bb, pr)].astype(F32)
        new_c, new_n = [], []
        for hh in range(2):
            h = 2 * pr + hh
            a, m_t, w_inter, m_new, wk, decay, vh = st[(bb, h)]
            num = w_inter * qc[(bb, h)] + av[(bb, h)]
            qn = jnp.sum(jnp.where(sel[hh], qpf, 0.0) * n_pair, axis=-1, keepdims=True)
            den = w_inter * qn + jnp.sum(a, axis=-1, keepdims=True)
            h_ref[bb, :, h * dv:(h + 1) * dv] = num / jnp.maximum(jnp.abs(den), jnp.exp(-m_t))
            new_c.append(decay * c_pair + upd[(bb, h)])
            new_n.append(decay * n_pair + jnp.sum(wk * kpf, axis=0, keepdims=True))
            m_ref[bb, :, h:h + 1] = m_new
        c_ref[bb, pr] = jnp.where(rsub < dk, new_c[0], new_c[1])
        n_ref[bb, pr] = jnp.where(lane1 < dk, new_n[0], new_n[1])


def mlstm_chunked(q, k, v, i_log, f_log, C0, n0, m0, chunk):
    B, T, _ = q.shape
    H = i_log.shape[-1]
    dk = C0.shape[2]
    dv = C0.shape[3]
    assert 2 * dk == LANES and dv == LANES
    nc = T // chunk
    L = chunk
    Tp = -(-T // LANES) * LANES
    fT = jnp.pad(f_log.transpose(0, 2, 1), ((0, 0), (0, 0), (0, Tp - T)))
    bT = seg_cumsum(fT, chunk)[:, :, :T]
    br = bT.reshape(B, H, nc, L).transpose(0, 2, 1, 3)
    bc = br.transpose(0, 1, 3, 2)
    ic = i_log.reshape(B, nc, L, H)
    ir = ic.transpose(0, 1, 3, 2)
    HP = H // 2
    c0 = C0.reshape(B, HP, 2 * dk, dv)
    n0p = n0.reshape(B, HP, 1, 2 * dk)
    m0p = m0.reshape(B, 1, H)
    QW = H * dk
    VW = H * dv
    nbb = 2 if B % 2 == 0 else 1
    col_spec = pl.BlockSpec((nbb, None, L, H), lambda b, c: (b, c, 0, 0))
    row_spec = pl.BlockSpec((nbb, None, H, L), lambda b, c: (b, c, 0, 0))
    st = lambda b, c: (b, 0, 0, 0)
    h, Cn, nn, mn = pl.pallas_call(
        functools.partial(_mlstm_kernel, n_heads=H, dk=dk),
        grid=(B // nbb, nc),
        in_specs=[pl.BlockSpec((nbb, L, QW), lambda b, c: (b, c, 0)),
                  pl.BlockSpec((nbb, L, QW), lambda b, c: (b, c, 0)),
                  pl.BlockSpec((nbb, L, VW), lambda b, c: (b, c, 0)),
                  col_spec, row_spec, col_spec, row_spec,
                  pl.BlockSpec((nbb, HP, 2 * dk, dv), st),
                  pl.BlockSpec((nbb, HP, 1, 2 * dk), st),
                  pl.BlockSpec((nbb, 1, H), lambda b, c: (b, 0, 0))],
        out_specs=[pl.BlockSpec((nbb, L, VW), lambda b, c: (b, c, 0)),
                   pl.BlockSpec((nbb, HP, 2 * dk, dv), st),
                   pl.BlockSpec((nbb, HP, 1, 2 * dk), st),
                   pl.BlockSpec((nbb, 1, H), lambda b, c: (b, 0, 0))],
        out_shape=[jax.ShapeDtypeStruct((B, T, VW), F32),
                   jax.ShapeDtypeStruct((B, HP, 2 * dk, dv), F32),
                   jax.ShapeDtypeStruct((B, HP, 1, 2 * dk), F32),
                   jax.ShapeDtypeStruct((B, 1, H), F32)],
        compiler_params=_params("arbitrary", "arbitrary"),
        name="mlstm_chunked",
    )(q, k, v, ic, ir, bc, br, c0, n0p, m0p)
    return h, Cn.reshape(B, H, dk, dv), nn.reshape(B, H, dk), mn.reshape(B, H)


def _candidate_tiles(n):
    runs = []
    for a in range(n // 2):
        cnt = n // (a + 1)
        for b0 in range(0, cnt, 8):
            runs.append((a, b0, 0, min(8, cnt - b0)))
    runs.append((-1, 0, 0, n // 2))
    tiles, pos = {}, 0
    for a, b0, src, cnt in runs:
        while cnt:
            take = min(cnt, 8 - pos % 8)
            tiles.setdefault(pos // 8, []).append((a, b0, (pos - src) % 8, pos % 8, pos % 8 + take))
            pos, src, cnt = pos + take, src + take, cnt - take
    return [tiles[q] for q in sorted(tiles)]


def _extract_topk(ss, payloads, n):
    ss = list(ss)
    R = ss[0].shape[0]
    ridx = lax.broadcasted_iota(I32, ss[0].shape, 0).astype(F32)
    vals = [[] for _ in ss]
    pays = [[] for _ in ss]
    for it in range(n):
        for c in range(len(ss)):
            s = ss[c]
            m = jnp.max(s, axis=0, keepdims=True)
            am = jnp.min(jnp.where(s == m, ridx, float(R)), axis=0, keepdims=True)
            hit = ridx == am
            if payloads[c] is None:
                pays[c].append(am)
            else:
                pays[c].append(jnp.max(jnp.where(hit, payloads[c], -1.0), axis=0, keepdims=True))
            vals[c].append(m)
            if it + 1 < n:
                ss[c] = jnp.where(hit, NEG_INF, s)
    return vals, pays


def _peer_topk_kernel(x_ref, sc_ref, sh_ref, gain_ref, wq_ref, k1_ref, k2_ref,
                      hb_ref, e_ref, g_ref, q_scr, v1_scr, i1_scr, v2_scr, i2_scr, *, n_heads, n_keys):
    TM = x_ref.shape[0]
    K = PEER_TOPK
    half = k1_ref.shape[2]
    hb = _modnorm(x_ref[...], gain_ref[...], sc_ref[...], sh_ref[...]).astype(BF16)
    hb_ref[...] = hb
    q_scr[...] = _dot(hb, wq_ref[...]).astype(BF16)
    sub8 = lax.broadcasted_iota(I32, (8, LANES), 0)

    def head_body(hd, _):
        c0 = pl.multiple_of(hd * 2 * half, 2 * half)
        c1 = pl.multiple_of(hd * 2 * half + half, half)
        s1 = _dot_nt(k1_ref[hd], q_scr[:, pl.ds(c0, half)])
        s2 = _dot_nt(k2_ref[hd], q_scr[:, pl.ds(c1, half)])
        NB = TM // LANES
        blocks = [slice(blk * LANES, (blk + 1) * LANES) for blk in range(NB)]
        vals, idxs = _extract_topk([s[:, ls] for ls in blocks for s in (s1, s2)], [None] * (2 * NB), K)
        for blk in range(NB):
            for side, (v_scr, i_scr) in enumerate(((v1_scr, i1_scr), (v2_scr, i2_scr))):
                for r in range(K):
                    v_scr[blk, r:r + 1, :] = vals[2 * blk + side][r]
                    i_scr[blk, r:r + 1, :] = idxs[2 * blk + side][r]
        cands, pays = [], []
        for blk in range(NB):
            pieces = {}

            def piece(a, b0, shift, blk=blk, pieces=pieces):
                key = (a, b0, shift)
                if key not in pieces:
                    if a >= 0:
                        cv = v1_scr[blk, a:a + 1, :] + v2_scr[blk, b0:b0 + 8, :]
                        pv = i1_scr[blk, a:a + 1, :] * n_keys + i2_scr[blk, b0:b0 + 8, :]
                    else:
                        cv = v1_scr[blk, K // 2:K, :] + v2_scr[blk, 0:1, :]
                        pv = i1_scr[blk, K // 2:K, :] * n_keys + i2_scr[blk, 0:1, :]
                    if shift:
                        cv, pv = pltpu.roll(cv, shift, 0), pltpu.roll(pv, shift, 0)
                    pieces[key] = (cv, pv)
                return pieces[key]

            cand, pay = [], []
            for segs in _candidate_tiles(K):
                cv = jnp.full((8, LANES), NEG_INF, F32)
                pv = jnp.zeros((8, LANES), F32)
                for a, b0, shift, lo, hi in segs:
                    pc, pp_ = piece(a, b0, shift)
                    if (lo, hi) == (0, 8):
                        cv, pv = pc, pp_
                    else:
                        inside = (sub8 >= lo) & (sub8 < hi)
                        cv, pv = jnp.where(inside, pc, cv), jnp.where(inside, pp_, pv)
                cand.append(cv)
                pay.append(pv)
            cands.append(jnp.concatenate(cand, axis=0))
            pays.append(jnp.concatenate(pay, axis=0))
        tops, topes = _extract_topk(cands, pays, K)
        out_rows = pl.ds(pl.multiple_of(hd * K, K), K)
        for blk in range(NB):
            top_s, top_e = tops[blk], topes[blk]
            ex = [jnp.exp(t - top_s[0]) for t in top_s]
            den = ex[0]
            for t in ex[1:]:
                den = den + t
            inv = 1.0 / den
            for r in range(K):
                i1_scr[blk, r:r + 1, :] = top_e[r]
                v1_scr[blk, r:r + 1, :] = ex[r] * inv
            e_ref[out_rows, blocks[blk]] = i1_scr[blk].astype(I32)
            g_ref[out_rows, blocks[blk]] = v1_scr[blk]
        return 0

    lax.fori_loop(0, n_heads, head_body, 0)


def peer_retrieve(x, scale, shift, gain, w_q, sub_k1, sub_k2, *, tm, tiles_per_group):
    N, D = x.shape
    HP, NK, half = sub_k1.shape
    assert NK == LANES and half == LANES and PEER_TOPK % 8 == 0
    R = scale.shape[1]
    QW = w_q.shape[1]
    J = HP * PEER_TOPK
    const = lambda i: (0, 0)
    const3 = lambda i: (0, 0, 0)
    kspec = pl.BlockSpec((HP, NK, half), const3)
    return pl.pallas_call(
        functools.partial(_peer_topk_kernel, n_heads=HP, n_keys=NK),
        grid=(N // tm,),
        in_specs=[pl.BlockSpec((tm, D), lambda i: (i, 0)),
                  _mod_spec(R, D, tiles_per_group), _mod_spec(R, D, tiles_per_group),
                  pl.BlockSpec((1, D), const),
                  pl.BlockSpec((D, QW), const),
                  kspec, kspec],
        out_specs=[pl.BlockSpec((tm, D), lambda i: (i, 0)),
                   pl.BlockSpec((J, tm), lambda i: (0, i)),
                   pl.BlockSpec((J, tm), lambda i: (0, i))],
        out_shape=[jax.ShapeDtypeStruct((N, D), BF16),
                   jax.ShapeDtypeStruct((J, N), I32),
                   jax.ShapeDtypeStruct((J, N), F32)],
        scratch_shapes=[pltpu.VMEM((tm, QW), BF16),
                        ] + [pltpu.VMEM((tm // LANES, PEER_TOPK, LANES), F32)] * 4,
        compiler_params=_params("arbitrary"),
        name="peer_retrieve",
    )(x, scale, shift, gain.reshape(1, D), w_q.astype(BF16), sub_k1.astype(BF16), sub_k2.astype(BF16))


def _peer_mix_kernel(hb_ref, i1n_ref, i2n_ref, gn_ref, i1c_ref, i2c_ref, ut_ref, v_ref, x_ref, g2_ref, o_ref,
                     a_scr, w_scr, hbuf_scr, y_scr, *, n_keys, tg):
    TM = hb_ref.shape[0]
    TE = ut_ref.shape[1]
    J = i1n_ref.shape[1]
    NE = pl.num_programs(1) // 2
    NT = pl.num_programs(0) - 1
    t = pl.program_id(0)
    j = pl.program_id(1)
    NP2 = n_keys // 2
    W2 = 2 * n_keys
    per_step = TM // NE
    rows = lax.broadcasted_iota(I32, (NP2, J), 0)
    cols = lax.broadcasted_iota(I32, (W2, J), 0)

    n_groups = per_step // tg

    def group_grids(g):
        t0 = pl.multiple_of(j * per_step + g * tg, tg)
        grids = []
        for u in range(tg):
            i1 = i1c_ref[pl.ds(t0 + u, 1), :]
            i2 = i2c_ref[pl.ds(t0 + u, 1), :]
            w = w_scr[pl.ds(t0 + u, 1), :]
            p1 = jnp.where(rows == (i1 >> 1), w, 0.0).astype(BF16)
            p2 = jnp.where(cols == ((i1 & 1) * n_keys + i2), 1.0, 0.0).astype(BF16)
            grids.append(_dot_nt(p1, p2).astype(BF16))
        return jnp.stack(grids, axis=0)

    def group_store(g, stacked):
        t0 = pl.multiple_of(j * per_step + g * tg, tg)
        hbuf_scr[:, pl.ds(t0, tg), :] = pltpu.einshape("tpc->ptc", stacked)

    def build_slice():
        for g in range(n_groups):
            group_store(g, group_grids(g))

    def activations(with_build=False):
        CW = MXU_DIM
        n_chunks = TE // CW
        hb = hb_ref[...]
        i1 = i1n_ref[...]
        i2 = i2n_ref[...]
        acc = a_scr[...]
        events = {}
        if with_build:
            for e in range(2 * n_groups):
                events.setdefault(min(e * n_chunks // (2 * n_groups), n_chunks - 1), []).append(e)
        pending = {}

        def pick(acc, k, act):
            for c in range(CW // n_keys):
                r = j * (TE // n_keys) + k * (CW // n_keys) + c
                acc = jnp.where(i1 == r, jnp.take_along_axis(act[:, c * n_keys:(c + 1) * n_keys], i2, axis=1), acc)
            return acc

        prev = None
        for k in range(n_chunks):
            act = _dot(hb, ut_ref[:, k * CW:(k + 1) * CW])
            for e in events.get(k, []):
                if e % 2 == 0:
                    pending[e // 2] = group_grids(e // 2)
                else:
                    group_store(e // 2, pending.pop(e // 2))
            if prev is not None:
                acc = pick(acc, k - 1, prev)
            prev = act
        a_scr[...] = pick(acc, n_chunks - 1, prev)

    @pl.when((j == 0) & (t < NT))
    def _():
        a_scr[...] = jnp.zeros(a_scr.shape, F32)

    @pl.when((j < NE) & (t > 0) & (t < NT))
    def _():
        activations(with_build=True)

    @pl.when((j < NE) & (t == 0))
    def _():
        activations()

    @pl.when((j < NE) & (t == NT))
    def _():
        build_slice()

    @pl.when((j == NE - 1) & (t < NT))
    def _():
        a = a_scr[...]
        w_scr[...] = gn_ref[...] * (0.5 * a * (1.0 + lax.erf(a * math.sqrt(0.5))))

    @pl.when((j == NE) & (t > 0))
    def _():
        y_scr[...] = jnp.zeros(y_scr.shape, F32)

    @pl.when((j >= NE) & (t > 0))
    def _():
        jj = j - NE
        y = y_scr[...]
        for c in range(TE // W2):
            y = y + _dot(hbuf_scr[jj * (TE // W2) + c], v_ref[c * W2:(c + 1) * W2, :])
        y_scr[...] = y

    @pl.when((j == 2 * NE - 1) & (t > 0))
    def _():
        o_ref[...] = x_ref[...] + g2_ref[...] * y_scr[...]


def peer_mix(hb, i1, i2, g, ut_tab, v_tab, x, g2, *, tm, te, tiles_per_group, n_keys):
    N, D = x.shape
    E = v_tab.shape[0]
    J = i1.shape[1]
    NE = E // te
    NT = N // tm
    R = g2.shape[1]
    tg = 16
    assert te % (2 * n_keys) == 0 and n_keys == LANES and tm % (NE * tg) == 0
    nxt = lambda t, j: (jnp.minimum(t, NT - 1), 0)
    cur = lambda t, j: (jnp.maximum(t - 1, 0), 0)
    return pl.pallas_call(
        functools.partial(_peer_mix_kernel, n_keys=n_keys, tg=tg),
        grid=(NT + 1, 2 * NE),
        in_specs=[pl.BlockSpec((tm, D), nxt),
                  pl.BlockSpec((tm, J), nxt), pl.BlockSpec((tm, J), nxt), pl.BlockSpec((tm, J), nxt),
                  pl.BlockSpec((tm, J), cur), pl.BlockSpec((tm, J), cur),
                  pl.BlockSpec((D, te), lambda t, j: (0, jnp.where(t < NT, jnp.minimum(j, NE - 1), NE - 1))),
                  pl.BlockSpec((te, D), lambda t, j: (jnp.where(t > 0, jnp.maximum(j - NE, 0), 0), 0)),
                  pl.BlockSpec((tm, D), cur),
                  pl.BlockSpec((None, R, D), lambda t, j: (jnp.maximum(t - 1, 0) // tiles_per_group, 0, 0))],
        out_specs=pl.BlockSpec((tm, D), cur),
        out_shape=jax.ShapeDtypeStruct((N, D), F32),
        scratch_shapes=[pltpu.VMEM((tm, J), F32), pltpu.VMEM((tm, J), F32),
                        pltpu.VMEM((n_keys // 2, tm, 2 * n_keys), BF16),
                        pltpu.VMEM((tm, D), F32)],
        compiler_params=_params("arbitrary", "arbitrary"),
        name="peer_mix",
    )(hb, i1, i2, g, i1, i2, ut_tab, v_tab, x, g2)


def _trunk(x3, mods, past, params, tables, *, tm):
    (norm_mix, norm_ffn, fox_w_in, fox_b_f, fox_q_norm, fox_k_norm, fox_w_out,
     ml_w_in, ml_b_i, ml_b_f, ml_h_norm, ml_w_out, peer_w_q, peer_sub_k1, peer_sub_k2) = params
    ut_bf, v_bf = tables
    B, T, D = x3.shape
    N = B * T
    depth = norm_mix.shape[0]
    n_fox_heads = fox_b_f.shape[1]
    n_keys = peer_sub_k1.shape[2]
    x = x3.reshape(N, D)
    if T % tm == 0:
        tpg, rows = T // tm, 1
        expand = lambda a: a.reshape(B, 1, D)
    else:
        assert N == tm
        tpg, rows = 1, N
        expand = lambda a: jnp.repeat(a, T, axis=0).reshape(1, N, D)
    new_k, new_v, new_lf, new_C, new_n, new_m = [], [], [], [], [], []
    for layer in range(depth):
        sh1, sc1, g1, sh2, sc2, g2 = [expand(a) for a in jnp.split(mods[layer], 6, axis=-1)]
        j = layer // 2
        if layer % 2 == 0:
            q, k, v, kb, vb, sg, lf = fox_project(
                x, sc1, sh1, norm_mix[layer], fox_w_in[j], fox_b_f[j], fox_q_norm[j], fox_k_norm[j],
                tm=tm, tiles_per_group=tpg)
            if past is None:
                F = seg_cumsum(lf.reshape(B, T, n_fox_heads).transpose(0, 2, 1), T).transpose(0, 2, 1)
                o = fox_attend_prompt(q.reshape(B, T, D), kb.reshape(B, T, D), vb.reshape(B, T, D), F,
                                      n_heads=n_fox_heads)
            else:
                cache_kt, cache_vt, cache_cf, _, _, _, page_table = past
                o = fox_attend_sample(q.reshape(B, T, D), kb.reshape(B, T, D), vb.reshape(B, T, D),
                                      lf.reshape(B, T, n_fox_heads), cache_kt, cache_vt, cache_cf,
                                      page_table, j, n_heads=n_fox_heads)
            x = out_project(o.reshape(N, D), sg, fox_w_out[j], x, g1, jnp.ones((D,), F32),
                            tm=tm, tiles_per_group=tpg, head_norm=False)
            new_k.append(k.reshape(B, T, n_fox_heads, -1))
            new_v.append(v.reshape(B, T, n_fox_heads, -1))
            new_lf.append(lf.reshape(B, T, n_fox_heads))
        else:
            n_ml_heads = ml_b_i.shape[1]
            v_width = ml_w_out.shape[1]
            q, k, v, so, il, fl = ml_project(x, sc1, sh1, norm_mix[layer], ml_w_in[j], ml_b_i[j], ml_b_f[j],
                                             tm=tm, tiles_per_group=tpg, v_width=v_width)
            QW = q.shape[1]
            dk = QW // n_ml_heads
            dv = v_width // n_ml_heads
            q3, k3, v3 = q.reshape(B, T, QW), k.reshape(B, T, QW), v.reshape(B, T, v_width)
            il3, fl3 = il.reshape(B, T, n_ml_heads), fl.reshape(B, T, n_ml_heads)
            if past is None:
                C0 = jnp.zeros((B, n_ml_heads, dk, dv), F32)
                n0 = jnp.zeros((B, n_ml_heads, dk), F32)
                m0 = jnp.zeros((B, n_ml_heads), F32)
                hh, C, n_, m_ = mlstm_chunked(q3, k3, v3, il3, fl3, C0, n0, m0, 64)
            else:
                _, _, _, state_C, state_n, state_m, _ = past
                Tp = 16
                pad3 = ((0, 0), (0, Tp - T), (0, 0))
                hh, C, n_, m_ = mlstm_chunked(
                    jnp.pad(q3, pad3), jnp.pad(k3, pad3), jnp.pad(v3, pad3),
                    jnp.pad(il3, pad3, constant_values=-1e30), jnp.pad(fl3, pad3),
                    state_C[j], state_n[j], state_m[j], Tp)
                hh = hh[:, :T]
            x = out_project(hh.reshape(N, v_width), so, ml_w_out[j], x, g1, ml_h_norm[j].reshape(-1),
                            tm=tm, tiles_per_group=tpg, head_norm=True)
            new_C.append(C)
            new_n.append(n_)
            new_m.append(m_)
        hb, e, g = peer_retrieve(x, sc2, sh2, norm_ffn[layer], peer_w_q[layer], peer_sub_k1[layer],
                                 peer_sub_k2[layer], tm=tm, tiles_per_group=tpg)
        eT = e.T
        mix_tm = 2 * tm if tpg % 2 == 0 else tm
        x = peer_mix(hb, eT // n_keys, eT % n_keys, g.T, ut_bf[layer], v_bf[layer], x, g2,
                     tm=mix_tm, te=2048, tiles_per_group=tpg * tm // mix_tm, n_keys=n_keys)
    return (x.reshape(B, T, D), jnp.stack(new_k), jnp.stack(new_v), jnp.stack(new_lf),
            jnp.stack(new_C), jnp.stack(new_n), jnp.stack(new_m))


def kernel(x_prompt, x_sample, cache_k, cache_v, cache_logf, state_C, state_n, state_m, page_table,
           c_prompt, c_sample, ada_w, ada_b, norm_mix, norm_ffn, fox_w_in, fox_b_f, fox_q_norm, fox_k_norm,
           fox_w_out, ml_w_in, ml_b_i, ml_b_f, ml_h_norm, ml_w_out, peer_w_q, peer_sub_k1, peer_sub_k2,
           peer_u, peer_v):
    params = (norm_mix, norm_ffn, fox_w_in, fox_b_f, fox_q_norm, fox_k_norm, fox_w_out,
              ml_w_in, ml_b_i, ml_b_f, ml_h_norm, ml_w_out, peer_w_q, peer_sub_k1, peer_sub_k2)
    tables = (peer_u.astype(BF16).transpose(0, 2, 1), peer_v.astype(BF16))
    Bp = c_prompt.shape[0]
    mods = adaln(jnp.concatenate([c_prompt, c_sample], axis=0), ada_w, ada_b)
    y_p, k_p, v_p, lf_p, C_p, n_p, m_p = _trunk(x_prompt, mods[:, :Bp], None, params, tables, tm=256)
    past = prepare_cache(cache_k, cache_v, cache_logf) + (state_C, state_n, state_m, page_table)
    y_s, k_s, v_s, lf_s, C_s, n_s, m_s = _trunk(x_sample, mods[:, Bp:], past, params, tables,
                                               tm=x_sample.shape[0] * x_sample.shape[1])
    return (y_p, y_s, k_p, v_p, lf_p, C_p, n_p, m_p, k_s, v_s, lf_s, C_s, n_s, m_s)
```

```python
import functools
import math

import jax
import jax.numpy as jnp
from jax import lax
from jax.experimental import pallas as pl
from jax.experimental.pallas import tpu as pltpu

F32 = jnp.float32
BF16 = jnp.bfloat16
I32 = jnp.int32

RMS_EPS = 1e-6
LANES = 128
MXU_DIM = 256
VMEM_LIMIT = 52 * 1024 * 1024
PEER_TOPK = 16
PAGE_SIZE = 128
NEG_INF = float("-inf")


def _dot(a, b):
    return jnp.dot(a, b, preferred_element_type=F32)


def _dot_nt(a, b):
    return lax.dot_general(a, b, (((1,), (1,)), ((), ())), preferred_element_type=F32)


def _dot_tn(a, b):
    return lax.dot_general(a, b, (((0,), (0,)), ((), ())), preferred_element_type=F32)


def _split2(a):
    hi = a.astype(BF16)
    lo = (a - hi.astype(F32)).astype(BF16)
    return hi, lo


def _dot3(a, b):
    ah, al = _split2(a)
    bh, bl = _split2(b)
    return _dot(ah, bh) + _dot(ah, bl) + _dot(al, bh)


def _modnorm(x, gain, scale, shift):
    ms = jnp.mean(x * x, axis=-1, keepdims=True)
    return x * lax.rsqrt(ms + RMS_EPS) * gain * (1.0 + scale) + shift


def _log_sigmoid(z):
    return jnp.minimum(z, 0.0) - jnp.log1p(jnp.exp(-jnp.abs(z)))


def _params(*sem):
    return pltpu.CompilerParams(dimension_semantics=sem, vmem_limit_bytes=VMEM_LIMIT)


def _mod_spec(rows, d, tiles_per_group):
    return pl.BlockSpec((None, rows, d), lambda i: (i // tiles_per_group, 0, 0))


def _adaln_kernel(c_ref, w_ref, b_ref, o_ref):
    c = c_ref[...]
    o_ref[...] = _dot3(c * jax.nn.sigmoid(c), w_ref[...]) + b_ref[...]


def adaln(c, ada_w, ada_b, tn=1536):
    L, D, D6 = ada_w.shape
    R = c.shape[0]
    return pl.pallas_call(
        _adaln_kernel,
        grid=(L, D6 // tn),
        in_specs=[pl.BlockSpec((R, D), lambda l, j: (0, 0)),
                  pl.BlockSpec((None, D, tn), lambda l, j: (l, 0, j)),
                  pl.BlockSpec((None, 1, tn), lambda l, j: (l, 0, j))],
        out_specs=pl.BlockSpec((None, R, tn), lambda l, j: (l, 0, j)),
        out_shape=jax.ShapeDtypeStruct((L, R, D6), F32),
        compiler_params=_params("arbitrary", "arbitrary"),
        name="adaln",
    )(c, ada_w, ada_b.reshape(L, 1, D6))


def _segcumsum_kernel(x_ref, o_ref, *, seg):
    C, T = x_ref.shape
    lane = lax.broadcasted_iota(I32, (C, LANES), 1)
    w = min(seg, LANES)
    carry = None
    for blk in range(T // LANES):
        x = x_ref[:, blk * LANES:(blk + 1) * LANES]
        k = 1
        while k < w:
            x = x + jnp.where((lane & (w - 1)) >= k, pltpu.roll(x, k, 1), 0.0)
            k *= 2
        if seg > LANES:
            if blk % (seg // LANES) != 0:
                x = x + carry
            carry = x[:, LANES - 1:LANES]
        o_ref[:, blk * LANES:(blk + 1) * LANES] = x


def seg_cumsum(x, seg):
    R, C, T = x.shape
    assert T % LANES == 0 and (seg & (seg - 1)) == 0
    assert T % seg == 0 and (seg <= LANES or seg % LANES == 0)
    return pl.pallas_call(
        functools.partial(_segcumsum_kernel, seg=seg),
        grid=(R,),
        in_specs=[pl.BlockSpec((None, C, T), lambda r: (r, 0, 0))],
        out_specs=pl.BlockSpec((None, C, T), lambda r: (r, 0, 0)),
        out_shape=jax.ShapeDtypeStruct((R, C, T), F32),
        compiler_params=_params("arbitrary"),
        name="seg_cumsum",
    )(x)


def _pair_headnorm(blk, gain2, hd):
    lane = lax.broadcasted_iota(I32, blk.shape, 1)
    lo = lane < hd
    sq = blk * blk
    s_lo = jnp.sum(jnp.where(lo, sq, 0.0), axis=-1, keepdims=True)
    s_hi = jnp.sum(jnp.where(lo, 0.0, sq), axis=-1, keepdims=True)
    inv = lax.rsqrt(jnp.where(lo, s_lo, s_hi) * (1.0 / hd) + RMS_EPS)
    return blk * inv * gain2


def _fox_proj_kernel(x_ref, sc_ref, sh_ref, gain_ref, w_ref, wfh_ref, wfl_ref, bf_ref, qg_ref, kg_ref,
                     q_ref, k_ref, v_ref, kb_ref, vb_ref, sg_ref, lf_ref, *, n_heads):
    D = x_ref.shape[1]
    hd = D // n_heads
    h = _modnorm(x_ref[...], gain_ref[...], sc_ref[...], sh_ref[...])
    hh, hl = _split2(h)
    qk_scale = hd ** -0.5
    q = _dot(hh, w_ref[:, 0:D])
    for c in range(D // LANES):
        sl = slice(c * LANES, (c + 1) * LANES)
        q_ref[:, sl] = (_pair_headnorm(q[:, sl], qg_ref[...], hd) * qk_scale).astype(BF16)
    k = _dot(hh, w_ref[:, D:2 * D])
    for c in range(D // LANES):
        sl = slice(c * LANES, (c + 1) * LANES)
        kn = _pair_headnorm(k[:, sl], kg_ref[...], hd)
        k_ref[:, sl] = kn
        kb_ref[:, sl] = kn.astype(BF16)
    v = _dot(hh, w_ref[:, 2 * D:3 * D])
    v_ref[...] = v
    vb_ref[...] = v.astype(BF16)
    sg_ref[...] = jax.nn.sigmoid(_dot(hh, w_ref[:, 3 * D:4 * D]))
    f = _dot(hh, wfh_ref[...]) + _dot(hh, wfl_ref[...]) + _dot(hl, wfh_ref[...])
    lf_ref[...] = _log_sigmoid(f[:, 0:n_heads] + bf_ref[...])


def fox_project(x, scale, shift, gain, w_in, b_f, q_gain, k_gain, *, tm, tiles_per_group):
    N, D = x.shape
    H = b_f.shape[0]
    hd = D // H
    assert 2 * hd == LANES
    R = scale.shape[1]
    w_main = jnp.concatenate([w_in[:, :3 * D], w_in[:, 3 * D + H:]], axis=1).astype(BF16)
    w_f = jnp.pad(w_in[:, 3 * D:3 * D + H], ((0, 0), (0, LANES - H)))
    wfh = w_f.astype(BF16)
    wfl = (w_f - wfh.astype(F32)).astype(BF16)
    const = lambda i: (0, 0)
    row = lambda i: (i, 0)
    outs = pl.pallas_call(
        functools.partial(_fox_proj_kernel, n_heads=H),
        grid=(N // tm,),
        in_specs=[pl.BlockSpec((tm, D), row),
                  _mod_spec(R, D, tiles_per_group), _mod_spec(R, D, tiles_per_group),
                  pl.BlockSpec((1, D), const),
                  pl.BlockSpec((D, 4 * D), const),
                  pl.BlockSpec((D, LANES), const), pl.BlockSpec((D, LANES), const),
                  pl.BlockSpec((1, H), const),
                  pl.BlockSpec((1, LANES), const), pl.BlockSpec((1, LANES), const)],
        out_specs=[pl.BlockSpec((tm, D), row)] * 6 + [pl.BlockSpec((tm, H), row)],
        out_shape=[jax.ShapeDtypeStruct((N, D), BF16),
                   jax.ShapeDtypeStruct((N, D), F32), jax.ShapeDtypeStruct((N, D), F32),
                   jax.ShapeDtypeStruct((N, D), BF16), jax.ShapeDtypeStruct((N, D), BF16),
                   jax.ShapeDtypeStruct((N, D), F32),
                   jax.ShapeDtypeStruct((N, H), F32)],
        compiler_params=_params("arbitrary"),
        name="fox_project",
    )(x, scale, shift, gain.reshape(1, D), w_main, wfh, wfl, b_f.reshape(1, H),
      jnp.tile(q_gain, 2).reshape(1, LANES), jnp.tile(k_gain, 2).reshape(1, LANES))
    return outs


def _fox_attn_kernel(q_ref, k_ref, v_ref, fc_ref, fr_ref, o_ref, *, hd, n_sub, hps):
    TQ = q_ref.shape[0]
    SQ = TQ // n_sub
    qi = pl.program_id(2)
    lane = lax.broadcasted_iota(I32, (SQ, LANES), 1)
    row = lax.broadcasted_iota(I32, (SQ, TQ), 0)
    col = lax.broadcasted_iota(I32, (SQ, TQ), 1)
    chains = [(hh, sb) for hh in range(hps) for sb in range(n_sub)]
    nch = len(chains)
    qms, fqs = [], []
    for hh, sb in chains:
        pb, ph = hh // 2, hh % 2
        q = q_ref[sb * SQ:(sb + 1) * SQ, pb * LANES:(pb + 1) * LANES]
        qms.append(jnp.where((lane >= ph * hd) & (lane < (ph + 1) * hd), q, jnp.zeros_like(q)))
        fqs.append(fc_ref[sb * SQ:(sb + 1) * SQ, hh:hh + 1])

    def block_scores(j):
        start = pl.multiple_of(j * TQ, TQ)
        kjs = [k_ref[pl.ds(start, TQ), pb * LANES:(pb + 1) * LANES] for pb in range(hps // 2)]
        return tuple(_dot_nt(qms[c], kjs[hh // 2]) + (fqs[c] - fr_ref[hh:hh + 1, pl.ds(start, TQ)])
                     for c, (hh, sb) in enumerate(chains))

    def update(j, state, scores, masked):
        start = pl.multiple_of(j * TQ, TQ)
        vjs = [v_ref[pl.ds(start, TQ), pb * LANES:(pb + 1) * LANES] for pb in range(hps // 2)]
        new = []
        for c, (hh, sb) in enumerate(chains):
            m, l, acc = state[c]
            s = jnp.where(col <= row + sb * SQ, scores[c], NEG_INF) if masked else scores[c]
            m_new = jnp.maximum(m, jnp.max(s, axis=-1, keepdims=True))
            alpha = jnp.exp(m - m_new)
            p = jnp.exp(s - m_new)
            new.append((m_new, alpha * l + jnp.sum(p, axis=-1, keepdims=True),
                        alpha * acc + _dot(p.astype(BF16), vjs[hh // 2])))
        return tuple(new)

    def step(j, carry):
        state, scores = carry
        nxt = block_scores(j + 1)
        return update(j, state, scores, False), nxt

    init = tuple((jnp.full((SQ, 1), NEG_INF, F32), jnp.zeros((SQ, 1), F32), jnp.zeros((SQ, LANES), F32))
                 for _ in chains)
    state, scores = lax.fori_loop(0, qi, step, (init, block_scores(0)))
    final = update(qi, state, scores, True)
    for pb in range(hps // 2):
        for sb in range(n_sub):
            even = final[(2 * pb) * n_sub + sb]
            odd = final[(2 * pb + 1) * n_sub + sb]
            o_ref[sb * SQ:(sb + 1) * SQ, pb * LANES:(pb + 1) * LANES] = jnp.where(
                lane < hd, even[2] / even[1], odd[2] / odd[1])


def fox_attend_prompt(q, kb, vb, F, *, n_heads, tq=256, hps=2, n_sub=2):
    B, S, D = q.shape
    hd = D // n_heads
    assert 2 * hd == LANES and hps % 2 == 0 and n_heads % hps == 0
    HG = n_heads // hps
    W = hps * hd
    fcol = F.reshape(B, S, HG, hps).transpose(0, 2, 1, 3)
    frow = fcol.transpose(0, 1, 3, 2)
    return pl.pallas_call(
        functools.partial(_fox_attn_kernel, hd=hd, n_sub=n_sub, hps=hps),
        grid=(B, HG, S // tq),
        in_specs=[pl.BlockSpec((None, tq, W), lambda b, h, i: (b, i, h)),
                  pl.BlockSpec((None, S, W), lambda b, h, i: (b, 0, h)),
                  pl.BlockSpec((None, S, W), lambda b, h, i: (b, 0, h)),
                  pl.BlockSpec((None, None, tq, hps), lambda b, h, i: (b, h, i, 0)),
                  pl.BlockSpec((None, None, hps, S), lambda b, h, i: (b, h, 0, 0))],
        out_specs=pl.BlockSpec((None, tq, W), lambda b, h, i: (b, i, h)),
        out_shape=jax.ShapeDtypeStruct((B, S, D), F32),
        compiler_params=_params("arbitrary", "arbitrary", "arbitrary"),
        name="fox_attend_prompt",
    )(q, kb, vb, fcol, frow)


def _fox_decode_kernel(pt_ref, qbd_ref, *refs, n_heads, pp):
    del pt_ref
    kt_refs, vt_refs, cf_refs = refs[:pp], refs[pp:2 * pp], refs[2 * pp:3 * pp]
    kn_ref, vn_ref, cn_ref, o_ref, m_scr, l_scr, acc_scr, carry_scr = refs[3 * pp:]
    H = n_heads
    step = pl.program_id(1)
    n_steps = pl.num_programs(1)
    R, D = qbd_ref.shape
    T = R // H
    P = cf_refs[0].shape[1]

    @pl.when(step == 0)
    def _():
        m_scr[...] = jnp.full(m_scr.shape, NEG_INF, F32)
        l_scr[...] = jnp.zeros(l_scr.shape, F32)
        acc_scr[...] = jnp.zeros(acc_scr.shape, F32)
        carry_scr[...] = jnp.zeros(carry_scr.shape, F32)

    qbd = qbd_ref[...]

    def merge(m_i, l_i, acc_i):
        m = m_scr[...]
        m_new = jnp.maximum(m, m_i)
        a_old = jnp.exp(m - m_new)
        a_new = jnp.exp(m_i - m_new)
        l_scr[...] = a_old * l_scr[...] + a_new * l_i
        acc_scr[...] = a_old * acc_scr[...] + a_new * acc_i
        m_scr[...] = m_new

    def scores(s, fk):
        return s - jnp.concatenate([fk] * T, axis=0)

    carry = carry_scr[...]
    ss = []
    for i in range(pp):
        cf = cf_refs[i][...]
        ss.append(scores(_dot(qbd, kt_refs[i][...].astype(BF16)), cf + carry))
        carry = carry + cf[:, P - 1:P]
    carry_scr[...] = carry
    m_s = jnp.max(ss[0], axis=-1, keepdims=True)
    for s in ss[1:]:
        m_s = jnp.maximum(m_s, jnp.max(s, axis=-1, keepdims=True))
    l_s = jnp.zeros_like(m_s)
    acc_s = jnp.zeros(acc_scr.shape, F32)
    for i, s in enumerate(ss):
        pr = jnp.exp(s - m_s)
        l_s = l_s + jnp.sum(pr, axis=-1, keepdims=True)
        acc_s = acc_s + _dot_nt(pr.astype(BF16), vt_refs[i][...].astype(BF16))
    merge(m_s, l_s, acc_s)

    @pl.when(step == n_steps - 1)
    def _():
        s2 = scores(_dot_nt(qbd, kn_ref[...]), cn_ref[...] + carry_scr[...])
        key = lax.broadcasted_iota(I32, s2.shape, 1)
        t = lax.broadcasted_iota(I32, s2.shape, 0) // H
        s2 = jnp.where(key <= t, s2, NEG_INF)
        m2 = jnp.max(s2, axis=-1, keepdims=True)
        p2 = jnp.exp(s2 - m2)
        merge(m2, jnp.sum(p2, axis=-1, keepdims=True), _dot(p2.astype(BF16), vn_ref[...]))
        out = acc_scr[...] / l_scr[...]
        rh = lax.broadcasted_iota(I32, (R, D), 0) % H
        ch = lax.broadcasted_iota(I32, (R, D), 1) // (D // H)
        out = jnp.where(rh == ch, out, 0.0)
        o_ref[...] = jnp.sum(out.reshape(T, H, D), axis=1)


def fox_attend_sample(q, kb_new, vb_new, lf_new, cache_kt, cache_vt, cache_cf, page_table, j, *, n_heads):
    Bd, T, D = q.shape
    H = n_heads
    hd = D // H
    n_pages = page_table.shape[1]
    n_fox, n_pool, _, P = cache_cf.shape
    R = T * H
    pp = next(c for c in (16, 8, 4, 2, 1) if n_pages % c == 0)
    assert T <= P
    q4 = q.reshape(Bd, T, H, hd)
    eye = jnp.eye(H, dtype=q.dtype)
    qbd = (q4[:, :, :, None, :] * eye[None, None, :, :, None]).reshape(Bd, R, D)
    pad = ((0, 0), (0, P - T), (0, 0))
    kn = jnp.pad(kb_new, pad)
    vn = jnp.pad(vb_new, pad)
    cn = seg_cumsum(jnp.pad(lf_new.transpose(0, 2, 1), ((0, 0), (0, 0), (0, P - T))), P)
    pt = page_table.reshape(-1).astype(I32)

    def page_spec(shape, i):
        return pl.BlockSpec((None, None) + shape, lambda b, s, pt: (j, pt[b * n_pages + s * pp + i], 0, 0))

    per_seq = lambda shape: pl.BlockSpec((None,) + shape, lambda b, s, pt: (b, 0, 0))
    grid_spec = pltpu.PrefetchScalarGridSpec(
        num_scalar_prefetch=1,
        grid=(Bd, n_pages // pp),
        in_specs=([per_seq((R, D))]
                  + [page_spec((D, P), i) for i in range(pp)]
                  + [page_spec((D, P), i) for i in range(pp)]
                  + [page_spec((H, P), i) for i in range(pp)]
                  + [per_seq((P, D)), per_seq((P, D)), per_seq((H, P))]),
        out_specs=per_seq((T, D)),
        scratch_shapes=[pltpu.VMEM((R, 1), F32), pltpu.VMEM((R, 1), F32),
                        pltpu.VMEM((R, D), F32), pltpu.VMEM((H, 1), F32)],
    )
    return pl.pallas_call(
        functools.partial(_fox_decode_kernel, n_heads=H, pp=pp),
        grid_spec=grid_spec,
        out_shape=jax.ShapeDtypeStruct((Bd, T, D), F32),
        compiler_params=_params("arbitrary", "arbitrary"),
        name="fox_attend_sample",
    )(pt, qbd, *([cache_kt] * pp), *([cache_vt] * pp), *([cache_cf] * pp), kn, vn, cn)


def prepare_cache(cache_k, cache_v, cache_logf):
    n_fox, n_pool, P, H, hd = cache_k.shape
    kt = cache_k.transpose(0, 1, 3, 4, 2).reshape(n_fox, n_pool, H * hd, P)
    vt = cache_v.transpose(0, 1, 3, 4, 2).reshape(n_fox, n_pool, H * hd, P)
    rows = n_fox * n_pool * H
    group = 2048 if rows % 2048 == 0 else H
    cf = seg_cumsum(cache_logf.transpose(0, 1, 3, 2).reshape(rows // group, group, P), P)
    return kt, vt, cf.reshape(n_fox, n_pool, H, P)


def _outproj_kernel(a_ref, gate_ref, w_ref, x_ref, g_ref, hn_ref, o_ref, *, head_norm):
    a = a_ref[...]
    D = a.shape[1]
    if head_norm:
        blocks = []
        for c in range(D // LANES):
            sl = slice(c * LANES, (c + 1) * LANES)
            blk = a[:, sl]
            ms = jnp.mean(blk * blk, axis=-1, keepdims=True)
            blocks.append(blk * lax.rsqrt(ms + RMS_EPS) * hn_ref[:, sl])
        a = jnp.concatenate(blocks, axis=-1)
    y = (a * gate_ref[...]).astype(BF16)
    o_ref[...] = x_ref[...] + g_ref[...] * _dot(y, w_ref[...])


def out_project(a, gate, w_out, x, g, head_gain, *, tm, tiles_per_group, head_norm):
    N, D = x.shape
    R = g.shape[1]
    row = lambda i: (i, 0)
    const = lambda i: (0, 0)
    return pl.pallas_call(
        functools.partial(_outproj_kernel, head_norm=head_norm),
        grid=(N // tm,),
        in_specs=[pl.BlockSpec((tm, D), row), pl.BlockSpec((tm, D), row),
                  pl.BlockSpec((D, D), const), pl.BlockSpec((tm, D), row),
                  _mod_spec(R, D, tiles_per_group), pl.BlockSpec((1, D), const)],
        out_specs=pl.BlockSpec((tm, D), row),
        out_shape=jax.ShapeDtypeStruct((N, D), F32),
        compiler_params=_params("arbitrary"),
        name="out_project",
    )(a, gate, w_out.astype(BF16), x, g, head_gain.reshape(1, D))


def _ml_proj_kernel(x_ref, sc_ref, sh_ref, gain_ref, w_ref, wgh_ref, wgl_ref, bi_ref, bf_ref,
                    q_ref, k_ref, v_ref, so_ref, il_ref, fl_ref, *, n_heads, qk_width):
    D = x_ref.shape[1]
    QW = qk_width
    h = _modnorm(x_ref[...], gain_ref[...], sc_ref[...], sh_ref[...])
    hh, hl = _split2(h)
    k_scale = (QW // n_heads) ** -0.5
    q_ref[...] = _dot(hh, w_ref[:, 0:QW]).astype(BF16)
    k_ref[...] = (_dot(hh, w_ref[:, QW:2 * QW]) * k_scale).astype(BF16)
    v_ref[...] = _dot(hh, w_ref[:, 2 * QW:2 * QW + D])
    so_ref[...] = jax.nn.sigmoid(_dot(hh, w_ref[:, 2 * QW + D:2 * QW + 2 * D]))
    gts = _dot(hh, wgh_ref[...]) + _dot(hh, wgl_ref[...]) + _dot(hl, wgh_ref[...])
    il_ref[...] = gts[:, 0:n_heads] + bi_ref[...]
    fl_ref[...] = _log_sigmoid(gts[:, n_heads:2 * n_heads] + bf_ref[...])


def ml_project(x, scale, shift, gain, w_in, b_i, b_f, *, tm, tiles_per_group, v_width):
    N, D = x.shape
    H = b_i.shape[0]
    QW = (w_in.shape[1] - 2 * v_width - 2 * H) // 2
    R = scale.shape[1]
    g0 = 2 * QW + v_width
    w_main = jnp.concatenate([w_in[:, :g0], w_in[:, g0 + 2 * H:]], axis=1).astype(BF16)
    w_g = jnp.pad(w_in[:, g0:g0 + 2 * H], ((0, 0), (0, LANES - 2 * H)))
    wgh = w_g.astype(BF16)
    wgl = (w_g - wgh.astype(F32)).astype(BF16)
    const = lambda i: (0, 0)
    row = lambda i: (i, 0)
    WM = w_main.shape[1]
    return pl.pallas_call(
        functools.partial(_ml_proj_kernel, n_heads=H, qk_width=QW),
        grid=(N // tm,),
        in_specs=[pl.BlockSpec((tm, D), row),
                  _mod_spec(R, D, tiles_per_group), _mod_spec(R, D, tiles_per_group),
                  pl.BlockSpec((1, D), const),
                  pl.BlockSpec((D, WM), const),
                  pl.BlockSpec((D, LANES), const), pl.BlockSpec((D, LANES), const),
                  pl.BlockSpec((1, H), const), pl.BlockSpec((1, H), const)],
        out_specs=[pl.BlockSpec((tm, QW), row), pl.BlockSpec((tm, QW), row),
                   pl.BlockSpec((tm, v_width), row), pl.BlockSpec((tm, v_width), row),
                   pl.BlockSpec((tm, H), row), pl.BlockSpec((tm, H), row)],
        out_shape=[jax.ShapeDtypeStruct((N, QW), BF16), jax.ShapeDtypeStruct((N, QW), BF16),
                   jax.ShapeDtypeStruct((N, v_width), F32), jax.ShapeDtypeStruct((N, v_width), F32),
                   jax.ShapeDtypeStruct((N, H), F32), jax.ShapeDtypeStruct((N, H), F32)],
        compiler_params=_params("arbitrary"),
        name="ml_project",
    )(x, scale, shift, gain.reshape(1, D), w_main, wgh, wgl, b_i.reshape(1, H), b_f.reshape(1, H))


def _mlstm_kernel(q_ref, k_ref, v_ref, ic_ref, ir_ref, bc_ref, br_ref, c0_ref, n0_ref, m0_ref,
                  h_ref, c_ref, n_ref, m_ref, *, n_heads, dk):
    nbb, L = q_ref.shape[0], q_ref.shape[1]
    ci = pl.program_id(1)

    @pl.when(ci == 0)
    def _():
        c_ref[...] = c0_ref[...]
        n_ref[...] = n0_ref[...]
        m_ref[...] = m0_ref[...]

    tri = lax.broadcasted_iota(I32, (L, L), 1) <= lax.broadcasted_iota(I32, (L, L), 0)
    lane = lax.broadcasted_iota(I32, (L, LANES), 1)
    lane1 = lax.broadcasted_iota(I32, (1, LANES), 1)
    rsub = lax.broadcasted_iota(I32, (LANES, LANES), 0)
    dv = LANES
    m_in = [m_ref[bb] for bb in range(nbb)]
    units = [(bb, h) for bb in range(nbb) for h in range(n_heads)]
    pairs = [(bb, pr) for bb in range(nbb) for pr in range(n_heads // 2)]
    qp = {(bb, pr): q_ref[bb, :, pr * LANES:(pr + 1) * LANES] for bb, pr in pairs}
    kp = {(bb, pr): k_ref[bb, :, pr * LANES:(pr + 1) * LANES] for bb, pr in pairs}
    c_in = {u: c_ref[u[0], u[1]] for u in pairs}
    n_in = {u: n_ref[u[0], u[1]] for u in pairs}
    sel = [(lane >= hh * dk) & (lane < (hh + 1) * dk) for hh in range(2)]
    qm = {(bb, h): jnp.where(sel[h % 2], qp[(bb, h // 2)], jnp.zeros_like(qp[(bb, h // 2)])) for bb, h in units}
    qk = {(bb, h): _dot_nt(qm[(bb, h)], kp[(bb, h // 2)]) for bb, h in units}
    qc = {(bb, h): _dot(qm[(bb, h)], c_in[(bb, h // 2)].astype(BF16)) for bb, h in units}
    st = {}
    for bb, h in units:
        b_l = bc_ref[bb][:, h:h + 1]
        b_s = br_ref[bb][h:h + 1, :]
        ig_s = ir_ref[bb][h:h + 1, :]
        ig_l = ic_ref[bb][:, h:h + 1]
        m = m_in[bb][:, h:h + 1]
        dm = jnp.where(tri, b_l - b_s + ig_s, NEG_INF)
        inter = b_l + m
        m_t = jnp.maximum(inter, jnp.max(dm, axis=-1, keepdims=True))
        w_inter = jnp.exp(inter - m_t)
        a = jnp.exp(dm - m_t) * qk[(bb, h)]
        b_last = b_l[L - 1:L, :]
        m_new = jnp.maximum(b_last + m, jnp.max(b_last - b_s + ig_s, axis=-1, keepdims=True))
        wk = jnp.exp(b_last - b_l + ig_l - m_new)
        decay = jnp.exp(b_last + m - m_new)
        vh = v_ref[bb, :, h * dv:(h + 1) * dv]
        st[(bb, h)] = (a, m_t, w_inter, m_new, wk, decay, vh)
    av = {u: _dot(st[u][0].astype(BF16), st[u][6].astype(BF16)) for u in units}
    upd = {(bb, h): _dot_tn(kp[(bb, h // 2)], (st[(bb, h)][4] * st[(bb, h)][6]).astype(BF16)) for bb, h in units}
    for bb, pr in pairs:
        c_pair, n_pair = c_in[(bb, pr)], n_in[(bb, pr)]
        qpf = qp[(bb, pr)].astype(F32)
        kpf = kp[(bb, pr)].astype(F32)
        new_c, new_n = [], []
        for hh in range(2):
            h = 2 * pr + hh
            a, m_t, w_inter, m_new, wk, decay, vh = st[(bb, h)]
            num = w_inter * qc[(bb, h)] + av[(bb, h)]
            qn = jnp.sum(jnp.where(sel[hh], qpf, 0.0) * n_pair, axis=-1, keepdims=True)
            den = w_inter * qn + jnp.sum(a, axis=-1, keepdims=True)
            h_ref[bb, :, h * dv:(h + 1) * dv] = num / jnp.maximum(jnp.abs(den), jnp.exp(-m_t))
            new_c.append(decay * c_pair + upd[(bb, h)])
            new_n.append(decay * n_pair + jnp.sum(wk * kpf, axis=0, keepdims=True))
            m_ref[bb, :, h:h + 1] = m_new
        c_ref[bb, pr] = jnp.where(rsub < dk, new_c[0], new_c[1])
        n_ref[bb, pr] = jnp.where(lane1 < dk, new_n[0], new_n[1])


def mlstm_chunked(q, k, v, i_log, f_log, C0, n0, m0, chunk):
    B, T, _ = q.shape
    H = i_log.shape[-1]
    dk = C0.shape[2]
    dv = C0.shape[3]
    assert 2 * dk == LANES and dv == LANES
    nc = T // chunk
    L = chunk
    Tp = -(-T // LANES) * LANES
    fT = jnp.pad(f_log.transpose(0, 2, 1), ((0, 0), (0, 0), (0, Tp - T)))
    bT = seg_cumsum(fT, chunk)[:, :, :T]
    br = bT.reshape(B, H, nc, L).transpose(0, 2, 1, 3)
    bc = br.transpose(0, 1, 3, 2)
    ic = i_log.reshape(B, nc, L, H)
    ir = ic.transpose(0, 1, 3, 2)
    HP = H // 2
    c0 = C0.reshape(B, HP, 2 * dk, dv)
    n0p = n0.reshape(B, HP, 1, 2 * dk)
    m0p = m0.reshape(B, 1, H)
    QW = H * dk
    VW = H * dv
    nbb = 2 if B % 2 == 0 else 1
    col_spec = pl.BlockSpec((nbb, None, L, H), lambda b, c: (b, c, 0, 0))
    row_spec = pl.BlockSpec((nbb, None, H, L), lambda b, c: (b, c, 0, 0))
    st = lambda b, c: (b, 0, 0, 0)
    h, Cn, nn, mn = pl.pallas_call(
        functools.partial(_mlstm_kernel, n_heads=H, dk=dk),
        grid=(B // nbb, nc),
        in_specs=[pl.BlockSpec((nbb, L, QW), lambda b, c: (b, c, 0)),
                  pl.BlockSpec((nbb, L, QW), lambda b, c: (b, c, 0)),
                  pl.BlockSpec((nbb, L, VW), lambda b, c: (b, c, 0)),
                  col_spec, row_spec, col_spec, row_spec,
                  pl.BlockSpec((nbb, HP, 2 * dk, dv), st),
                  pl.BlockSpec((nbb, HP, 1, 2 * dk), st),
                  pl.BlockSpec((nbb, 1, H), lambda b, c: (b, 0, 0))],
        out_specs=[pl.BlockSpec((nbb, L, VW), lambda b, c: (b, c, 0)),
                   pl.BlockSpec((nbb, HP, 2 * dk, dv), st),
                   pl.BlockSpec((nbb, HP, 1, 2 * dk), st),
                   pl.BlockSpec((nbb, 1, H), lambda b, c: (b, 0, 0))],
        out_shape=[jax.ShapeDtypeStruct((B, T, VW), F32),
                   jax.ShapeDtypeStruct((B, HP, 2 * dk, dv), F32),
                   jax.ShapeDtypeStruct((B, HP, 1, 2 * dk), F32),
                   jax.ShapeDtypeStruct((B, 1, H), F32)],
        compiler_params=_params("arbitrary", "arbitrary"),
        name="mlstm_chunked",
    )(q, k, v, ic, ir, bc, br, c0, n0p, m0p)
    return h, Cn.reshape(B, H, dk, dv), nn.reshape(B, H, dk), mn.reshape(B, H)


def _candidate_tiles(n):
    runs = []
    for a in range(n // 2):
        cnt = n // (a + 1)
        for b0 in range(0, cnt, 8):
            runs.append((a, b0, 0, min(8, cnt - b0)))
    runs.append((-1, 0, 0, n // 2))
    tiles, pos = {}, 0
    for a, b0, src, cnt in runs:
        while cnt:
            take = min(cnt, 8 - pos % 8)
            tiles.setdefault(pos // 8, []).append((a, b0, (pos - src) % 8, pos % 8, pos % 8 + take))
            pos, src, cnt = pos + take, src + take, cnt - take
    return [tiles[q] for q in sorted(tiles)]


def _extract_topk(ss, payloads, n):
    ss = list(ss)
    R = ss[0].shape[0]
    ridx = lax.broadcasted_iota(I32, ss[0].shape, 0).astype(F32)
    vals = [[] for _ in ss]
    pays = [[] for _ in ss]
    for it in range(n):
        for c in range(len(ss)):
            s = ss[c]
            m = jnp.max(s, axis=0, keepdims=True)
            am = jnp.min(jnp.where(s == m, ridx, float(R)), axis=0, keepdims=True)
            hit = ridx == am
            if payloads[c] is None:
                pays[c].append(am)
            else:
                pays[c].append(jnp.max(jnp.where(hit, payloads[c], -1.0), axis=0, keepdims=True))
            vals[c].append(m)
            if it + 1 < n:
                ss[c] = jnp.where(hit, NEG_INF, s)
    return vals, pays


def _peer_topk_kernel(x_ref, sc_ref, sh_ref, gain_ref, wq_ref, k1_ref, k2_ref,
                      hb_ref, e_ref, g_ref, q_scr, v1_scr, i1_scr, v2_scr, i2_scr, *, n_heads, n_keys):
    TM = x_ref.shape[0]
    K = PEER_TOPK
    half = k1_ref.shape[2]
    hb = _modnorm(x_ref[...], gain_ref[...], sc_ref[...], sh_ref[...]).astype(BF16)
    hb_ref[...] = hb
    q_scr[...] = _dot(hb, wq_ref[...]).astype(BF16)
    sub8 = lax.broadcasted_iota(I32, (8, LANES), 0)

    def head_body(hd, _):
        c0 = pl.multiple_of(hd * 2 * half, 2 * half)
        c1 = pl.multiple_of(hd * 2 * half + half, half)
        s1 = _dot_nt(k1_ref[hd], q_scr[:, pl.ds(c0, half)])
        s2 = _dot_nt(k2_ref[hd], q_scr[:, pl.ds(c1, half)])
        NB = TM // LANES
        blocks = [slice(blk * LANES, (blk + 1) * LANES) for blk in range(NB)]
        vals, idxs = _extract_topk([s[:, ls] for ls in blocks for s in (s1, s2)], [None] * (2 * NB), K)
        for blk in range(NB):
            for side, (v_scr, i_scr) in enumerate(((v1_scr, i1_scr), (v2_scr, i2_scr))):
                for r in range(K):
                    v_scr[blk, r:r + 1, :] = vals[2 * blk + side][r]
                    i_scr[blk, r:r + 1, :] = idxs[2 * blk + side][r]
        cands, pays = [], []
        for blk in range(NB):
            pieces = {}

            def piece(a, b0, shift, blk=blk, pieces=pieces):
                key = (a, b0, shift)
                if key not in pieces:
                    if a >= 0:
                        cv = v1_scr[blk, a:a + 1, :] + v2_scr[blk, b0:b0 + 8, :]
                        pv = i1_scr[blk, a:a + 1, :] * n_keys + i2_scr[blk, b0:b0 + 8, :]
                    else:
                        cv = v1_scr[blk, K // 2:K, :] + v2_scr[blk, 0:1, :]
                        pv = i1_scr[blk, K // 2:K, :] * n_keys + i2_scr[blk, 0:1, :]
                    if shift:
                        cv, pv = pltpu.roll(cv, shift, 0), pltpu.roll(pv, shift, 0)
                    pieces[key] = (cv, pv)
                return pieces[key]

            cand, pay = [], []
            for segs in _candidate_tiles(K):
                cv = jnp.full((8, LANES), NEG_INF, F32)
                pv = jnp.zeros((8, LANES), F32)
                for a, b0, shift, lo, hi in segs:
                    pc, pp_ = piece(a, b0, shift)
                    if (lo, hi) == (0, 8):
                        cv, pv = pc, pp_
                    else:
                        inside = (sub8 >= lo) & (sub8 < hi)
                        cv, pv = jnp.where(inside, pc, cv), jnp.where(inside, pp_, pv)
                cand.append(cv)
                pay.append(pv)
            cands.append(jnp.concatenate(cand, axis=0))
            pays.append(jnp.concatenate(pay, axis=0))
        tops, topes = _extract_topk(cands, pays, K)
        out_rows = pl.ds(pl.multiple_of(hd * K, K), K)
        for blk in range(NB):
            top_s, top_e = tops[blk], topes[blk]
            ex = [jnp.exp(t - top_s[0]) for t in top_s]
            den = ex[0]
            for t in ex[1:]:
                den = den + t
            inv = 1.0 / den
            for r in range(K):
                i1_scr[blk, r:r + 1, :] = top_e[r]
                v1_scr[blk, r:r + 1, :] = ex[r] * inv
            e_ref[out_rows, blocks[blk]] = i1_scr[blk].astype(I32)
            g_ref[out_rows, blocks[blk]] = v1_scr[blk]
        return 0

    lax.fori_loop(0, n_heads, head_body, 0)


def peer_retrieve(x, scale, shift, gain, w_q, sub_k1, sub_k2, *, tm, tiles_per_group):
    N, D = x.shape
    HP, NK, half = sub_k1.shape
    assert NK == LANES and half == LANES and PEER_TOPK % 8 == 0
    R = scale.shape[1]
    QW = w_q.shape[1]
    J = HP * PEER_TOPK
    const = lambda i: (0, 0)
    const3 = lambda i: (0, 0, 0)
    kspec = pl.BlockSpec((HP, NK, half), const3)
    return pl.pallas_call(
        functools.partial(_peer_topk_kernel, n_heads=HP, n_keys=NK),
        grid=(N // tm,),
        in_specs=[pl.BlockSpec((tm, D), lambda i: (i, 0)),
                  _mod_spec(R, D, tiles_per_group), _mod_spec(R, D, tiles_per_group),
                  pl.BlockSpec((1, D), const),
                  pl.BlockSpec((D, QW), const),
                  kspec, kspec],
        out_specs=[pl.BlockSpec((tm, D), lambda i: (i, 0)),
                   pl.BlockSpec((J, tm), lambda i: (0, i)),
                   pl.BlockSpec((J, tm), lambda i: (0, i))],
        out_shape=[jax.ShapeDtypeStruct((N, D), BF16),
                   jax.ShapeDtypeStruct((J, N), I32),
                   jax.ShapeDtypeStruct((J, N), F32)],
        scratch_shapes=[pltpu.VMEM((tm, QW), BF16),
                        ] + [pltpu.VMEM((tm // LANES, PEER_TOPK, LANES), F32)] * 4,
        compiler_params=_params("arbitrary"),
        name="peer_retrieve",
    )(x, scale, shift, gain.reshape(1, D), w_q.astype(BF16), sub_k1.astype(BF16), sub_k2.astype(BF16))


def _peer_mix_kernel(hb_ref, i1n_ref, i2n_ref, gn_ref, i1c_ref, i2c_ref, ut_ref, v_ref, x_ref, g2_ref, o_ref,
                     a_scr, w_scr, hbuf_scr, y_scr, *, n_keys, tg):
    TM = hb_ref.shape[0]
    TE = ut_ref.shape[1]
    J = i1n_ref.shape[1]
    NE = pl.num_programs(1) // 2
    NT = pl.num_programs(0) - 1
    t = pl.program_id(0)
    j = pl.program_id(1)
    NP2 = n_keys // 2
    W2 = 2 * n_keys
    per_step = TM // NE
    rows = lax.broadcasted_iota(I32, (NP2, J), 0)
    cols = lax.broadcasted_iota(I32, (W2, J), 0)

    n_groups = per_step // tg

    def group_grids(g):
        t0 = pl.multiple_of(j * per_step + g * tg, tg)
        grids = []
        for u in range(tg):
            i1 = i1c_ref[pl.ds(t0 + u, 1), :]
            i2 = i2c_ref[pl.ds(t0 + u, 1), :]
            w = w_scr[pl.ds(t0 + u, 1), :]
            p1 = jnp.where(rows == (i1 >> 1), w, 0.0).astype(BF16)
            p2 = jnp.where(cols == ((i1 & 1) * n_keys + i2), 1.0, 0.0).astype(BF16)
            grids.append(_dot_nt(p1, p2).astype(BF16))
        return jnp.stack(grids, axis=0)

    def group_store(g, stacked):
        t0 = pl.multiple_of(j * per_step + g * tg, tg)
        hbuf_scr[:, pl.ds(t0, tg), :] = pltpu.einshape("tpc->ptc", stacked)

    def build_slice():
        for g in range(n_groups):
            group_store(g, group_grids(g))

    def activations(with_build=False):
        CW = MXU_DIM
        n_chunks = TE // CW
        hb = hb_ref[...]
        i1 = i1n_ref[...]
        i2 = i2n_ref[...]
        acc = a_scr[...]
        events = {}
        if with_build:
            for e in range(2 * n_groups):
                events.setdefault(min(e * n_chunks // (2 * n_groups), n_chunks - 1), []).append(e)
        pending = {}

        def pick(acc, k, act):
            for c in range(CW // n_keys):
                r = j * (TE // n_keys) + k * (CW // n_keys) + c
                acc = jnp.where(i1 == r, jnp.take_along_axis(act[:, c * n_keys:(c + 1) * n_keys], i2, axis=1), acc)
            return acc

        prev = None
        for k in range(n_chunks):
            act = _dot(hb, ut_ref[:, k * CW:(k + 1) * CW])
            for e in events.get(k, []):
                if e % 2 == 0:
                    pending[e // 2] = group_grids(e // 2)
                else:
                    group_store(e // 2, pending.pop(e // 2))
            if prev is not None:
                acc = pick(acc, k - 1, prev)
            prev = act
        a_scr[...] = pick(acc, n_chunks - 1, prev)

    @pl.when((j == 0) & (t < NT))
    def _():
        a_scr[...] = jnp.zeros(a_scr.shape, F32)

    @pl.when((j < NE) & (t > 0) & (t < NT))
    def _():
        activations(with_build=True)

    @pl.when((j < NE) & (t == 0))
    def _():
        activations()

    @pl.when((j < NE) & (t == NT))
    def _():
        build_slice()

    @pl.when((j == NE - 1) & (t < NT))
    def _():
        a = a_scr[...]
        w_scr[...] = gn_ref[...] * (0.5 * a * (1.0 + lax.erf(a * math.sqrt(0.5))))

    @pl.when((j == NE) & (t > 0))
    def _():
        y_scr[...] = jnp.zeros(y_scr.shape, F32)

    @pl.when((j >= NE) & (t > 0))
    def _():
        jj = j - NE
        y = y_scr[...]
        for c in range(TE // W2):
            y = y + _dot(hbuf_scr[jj * (TE // W2) + c], v_ref[c * W2:(c + 1) * W2, :])
        y_scr[...] = y

    @pl.when((j == 2 * NE - 1) & (t > 0))
    def _():
        o_ref[...] = x_ref[...] + g2_ref[...] * y_scr[...]


def peer_mix(hb, i1, i2, g, ut_tab, v_tab, x, g2, *, tm, te, tiles_per_group, n_keys):
    N, D = x.shape
    E = v_tab.shape[0]
    J = i1.shape[1]
    NE = E // te
    NT = N // tm
    R = g2.shape[1]
    tg = 16
    assert te % (2 * n_keys) == 0 and n_keys == LANES and tm % (NE * tg) == 0
    nxt = lambda t, j: (jnp.minimum(t, NT - 1), 0)
    cur = lambda t, j: (jnp.maximum(t - 1, 0), 0)
    return pl.pallas_call(
        functools.partial(_peer_mix_kernel, n_keys=n_keys, tg=tg),
        grid=(NT + 1, 2 * NE),
        in_specs=[pl.BlockSpec((tm, D), nxt),
                  pl.BlockSpec((tm, J), nxt), pl.BlockSpec((tm, J), nxt), pl.BlockSpec((tm, J), nxt),
                  pl.BlockSpec((tm, J), cur), pl.BlockSpec((tm, J), cur),
                  pl.BlockSpec((D, te), lambda t, j: (0, jnp.where(t < NT, jnp.minimum(j, NE - 1), NE - 1))),
                  pl.BlockSpec((te, D), lambda t, j: (jnp.where(t > 0, jnp.maximum(j - NE, 0), 0), 0)),
                  pl.BlockSpec((tm, D), cur),
                  pl.BlockSpec((None, R, D), lambda t, j: (jnp.maximum(t - 1, 0) // tiles_per_group, 0, 0))],
        out_specs=pl.BlockSpec((tm, D), cur),
        out_shape=jax.ShapeDtypeStruct((N, D), F32),
        scratch_shapes=[pltpu.VMEM((tm, J), F32), pltpu.VMEM((tm, J), F32),
                        pltpu.VMEM((n_keys // 2, tm, 2 * n_keys), BF16),
                        pltpu.VMEM((tm, D), F32)],
        compiler_params=_params("arbitrary", "arbitrary"),
        name="peer_mix",
    )(hb, i1, i2, g, i1, i2, ut_tab, v_tab, x, g2)


def _trunk(x3, mods, past, params, tables, *, tm):
    (norm_mix, norm_ffn, fox_w_in, fox_b_f, fox_q_norm, fox_k_norm, fox_w_out,
     ml_w_in, ml_b_i, ml_b_f, ml_h_norm, ml_w_out, peer_w_q, peer_sub_k1, peer_sub_k2) = params
    ut_bf, v_bf = tables
    B, T, D = x3.shape
    N = B * T
    depth = norm_mix.shape[0]
    n_fox_heads = fox_b_f.shape[1]
    n_keys = peer_sub_k1.shape[2]
    x = x3.reshape(N, D)
    if T % tm == 0:
        tpg, rows = T // tm, 1
        expand = lambda a: a.reshape(B, 1, D)
    else:
        assert N == tm
        tpg, rows = 1, N
        expand = lambda a: jnp.repeat(a, T, axis=0).reshape(1, N, D)
    big_tm = 2 * tm if tpg % 2 == 0 else tm
    big_tpg = tpg * tm // big_tm
    new_k, new_v, new_lf, new_C, new_n, new_m = [], [], [], [], [], []
    for layer in range(depth):
        sh1, sc1, g1, sh2, sc2, g2 = [expand(a) for a in jnp.split(mods[layer], 6, axis=-1)]
        j = layer // 2
        if layer % 2 == 0:
            q, k, v, kb, vb, sg, lf = fox_project(
                x, sc1, sh1, norm_mix[layer], fox_w_in[j], fox_b_f[j], fox_q_norm[j], fox_k_norm[j],
                tm=big_tm, tiles_per_group=big_tpg)
            if past is None:
                F = seg_cumsum(lf.reshape(B, T, n_fox_heads).transpose(0, 2, 1), T).transpose(0, 2, 1)
                o = fox_attend_prompt(q.reshape(B, T, D), kb.reshape(B, T, D), vb.reshape(B, T, D), F,
                                      n_heads=n_fox_heads)
            else:
                cache_kt, cache_vt, cache_cf, _, _, _, page_table = past
                o = fox_attend_sample(q.reshape(B, T, D), kb.reshape(B, T, D), vb.reshape(B, T, D),
                                      lf.reshape(B, T, n_fox_heads), cache_kt, cache_vt, cache_cf,
                                      page_table, j, n_heads=n_fox_heads)
            x = out_project(o.reshape(N, D), sg, fox_w_out[j], x, g1, jnp.ones((D,), F32),
                            tm=big_tm, tiles_per_group=big_tpg, head_norm=False)
            new_k.append(k.reshape(B, T, n_fox_heads, -1))
            new_v.append(v.reshape(B, T, n_fox_heads, -1))
            new_lf.append(lf.reshape(B, T, n_fox_heads))
        else:
            n_ml_heads = ml_b_i.shape[1]
            v_width = ml_w_out.shape[1]
            q, k, v, so, il, fl = ml_project(x, sc1, sh1, norm_mix[layer], ml_w_in[j], ml_b_i[j], ml_b_f[j],
                                             tm=big_tm, tiles_per_group=big_tpg, v_width=v_width)
            QW = q.shape[1]
            dk = QW // n_ml_heads
            dv = v_width // n_ml_heads
            q3, k3, v3 = q.reshape(B, T, QW), k.reshape(B, T, QW), v.reshape(B, T, v_width)
            il3, fl3 = il.reshape(B, T, n_ml_heads), fl.reshape(B, T, n_ml_heads)
            if past is None:
                C0 = jnp.zeros((B, n_ml_heads, dk, dv), F32)
                n0 = jnp.zeros((B, n_ml_heads, dk), F32)
                m0 = jnp.zeros((B, n_ml_heads), F32)
                hh, C, n_, m_ = mlstm_chunked(q3, k3, v3, il3, fl3, C0, n0, m0, 64)
            else:
                _, _, _, state_C, state_n, state_m, _ = past
                Tp = 16
                pad3 = ((0, 0), (0, Tp - T), (0, 0))
                hh, C, n_, m_ = mlstm_chunked(
                    jnp.pad(q3, pad3), jnp.pad(k3, pad3), jnp.pad(v3, pad3),
                    jnp.pad(il3, pad3, constant_values=-1e30), jnp.pad(fl3, pad3),
                    state_C[j], state_n[j], state_m[j], Tp)
                hh = hh[:, :T]
            x = out_project(hh.reshape(N, v_width), so, ml_w_out[j], x, g1, ml_h_norm[j].reshape(-1),
                            tm=big_tm, tiles_per_group=big_tpg, head_norm=True)
            new_C.append(C)
            new_n.append(n_)
            new_m.append(m_)
        hb, e, g = peer_retrieve(x, sc2, sh2, norm_ffn[layer], peer_w_q[layer], peer_sub_k1[layer],
                                 peer_sub_k2[layer], tm=tm, tiles_per_group=tpg)
        eT = e.T
        x = peer_mix(hb, eT // n_keys, eT % n_keys, g.T, ut_bf[layer], v_bf[layer], x, g2,
                     tm=big_tm, te=2048, tiles_per_group=big_tpg, n_keys=n_keys)
    return (x.reshape(B, T, D), jnp.stack(new_k), jnp.stack(new_v), jnp.stack(new_lf),
            jnp.stack(new_C), jnp.stack(new_n), jnp.stack(new_m))


def kernel(x_prompt, x_sample, cache_k, cache_v, cache_logf, state_C, state_n, state_m, page_table,
           c_prompt, c_sample, ada_w, ada_b, norm_mix, norm_ffn, fox_w_in, fox_b_f, fox_q_norm, fox_k_norm,
           fox_w_out, ml_w_in, ml_b_i, ml_b_f, ml_h_norm, ml_w_out, peer_w_q, peer_sub_k1, peer_sub_k2,
           peer_u, peer_v):
    params = (norm_mix, norm_ffn, fox_w_in, fox_b_f, fox_q_norm, fox_k_norm, fox_w_out,
              ml_w_in, ml_b_i, ml_b_f, ml_h_norm, ml_w_out, peer_w_q, peer_sub_k1, peer_sub_k2)
    tables = (peer_u.astype(BF16).transpose(0, 2, 1), peer_v.astype(BF16))
    Bp = c_prompt.shape[0]
    mods = adaln(jnp.concatenate([c_prompt, c_sample], axis=0), ada_w, ada_b)
    y_p, k_p, v_p, lf_p, C_p, n_p, m_p = _trunk(x_prompt, mods[:, :Bp], None, params, tables, tm=256)
    past = prepare_cache(cache_k, cache_v, cache_logf) + (state_C, state_n, state_m, page_table)
    y_s, k_s, v_s, lf_s, C_s, n_s, m_s = _trunk(x_sample, mods[:, Bp:], past, params, tables,
                                               tm=x_sample.shape[0] * x_sample.shape[1])
    return (y_p, y_s, k_p, v_p, lf_p, C_p, n_p, m_p, k_s, v_s, lf_s, C_s, n_s, m_s)
```

```python
import functools
import math

import jax
import jax.numpy as jnp
from jax import lax
from jax.experimental import pallas as pl
from jax.experimental.pallas import tpu as pltpu

F32 = jnp.float32
BF16 = jnp.bfloat16
I32 = jnp.int32

RMS_EPS = 1e-6
LANES = 128
MXU_DIM = 256
VMEM_LIMIT = 52 * 1024 * 1024
PEER_TOPK = 16
PAGE_SIZE = 128
NEG_INF = float("-inf")


def _dot(a, b):
    return jnp.dot(a, b, preferred_element_type=F32)


def _dot_nt(a, b):
    return lax.dot_general(a, b, (((1,), (1,)), ((), ())), preferred_element_type=F32)


def _dot_tn(a, b):
    return lax.dot_general(a, b, (((0,), (0,)), ((), ())), preferred_element_type=F32)


def _split2(a):
    hi = a.astype(BF16)
    lo = (a - hi.astype(F32)).astype(BF16)
    return hi, lo


def _dot3(a, b):
    ah, al = _split2(a)
    bh, bl = _split2(b)
    return _dot(ah, bh) + _dot(ah, bl) + _dot(al, bh)


def _modnorm(x, gain, scale, shift):
    ms = jnp.mean(x * x, axis=-1, keepdims=True)
    return x * lax.rsqrt(ms + RMS_EPS) * gain * (1.0 + scale) + shift


def _log_sigmoid(z):
    return jnp.minimum(z, 0.0) - jnp.log1p(jnp.exp(-jnp.abs(z)))


def _params(*sem):
    return pltpu.CompilerParams(dimension_semantics=sem, vmem_limit_bytes=VMEM_LIMIT)


def _mod_spec(rows, d, tiles_per_group):
    return pl.BlockSpec((None, rows, d), lambda i: (i // tiles_per_group, 0, 0))


def _adaln_kernel(c_ref, w_ref, b_ref, o_ref):
    c = c_ref[...]
    o_ref[...] = _dot3(c * jax.nn.sigmoid(c), w_ref[...]) + b_ref[...]


def adaln(c, ada_w, ada_b, tn=1536):
    L, D, D6 = ada_w.shape
    R = c.shape[0]
    return pl.pallas_call(
        _adaln_kernel,
        grid=(L, D6 // tn),
        in_specs=[pl.BlockSpec((R, D), lambda l, j: (0, 0)),
                  pl.BlockSpec((None, D, tn), lambda l, j: (l, 0, j)),
                  pl.BlockSpec((None, 1, tn), lambda l, j: (l, 0, j))],
        out_specs=pl.BlockSpec((None, R, tn), lambda l, j: (l, 0, j)),
        out_shape=jax.ShapeDtypeStruct((L, R, D6), F32),
        compiler_params=_params("arbitrary", "arbitrary"),
        name="adaln",
    )(c, ada_w, ada_b.reshape(L, 1, D6))


def _segcumsum_kernel(x_ref, o_ref, *, seg):
    C, T = x_ref.shape
    lane = lax.broadcasted_iota(I32, (C, LANES), 1)
    w = min(seg, LANES)
    carry = None
    for blk in range(T // LANES):
        x = x_ref[:, blk * LANES:(blk + 1) * LANES]
        k = 1
        while k < w:
            x = x + jnp.where((lane & (w - 1)) >= k, pltpu.roll(x, k, 1), 0.0)
            k *= 2
        if seg > LANES:
            if blk % (seg // LANES) != 0:
                x = x + carry
            carry = x[:, LANES - 1:LANES]
        o_ref[:, blk * LANES:(blk + 1) * LANES] = x


def seg_cumsum(x, seg):
    R, C, T = x.shape
    assert T % LANES == 0 and (seg & (seg - 1)) == 0
    assert T % seg == 0 and (seg <= LANES or seg % LANES == 0)
    return pl.pallas_call(
        functools.partial(_segcumsum_kernel, seg=seg),
        grid=(R,),
        in_specs=[pl.BlockSpec((None, C, T), lambda r: (r, 0, 0))],
        out_specs=pl.BlockSpec((None, C, T), lambda r: (r, 0, 0)),
        out_shape=jax.ShapeDtypeStruct((R, C, T), F32),
        compiler_params=_params("arbitrary"),
        name="seg_cumsum",
    )(x)


def _pair_headnorm(blk, gain2, hd):
    lane = lax.broadcasted_iota(I32, blk.shape, 1)
    lo = lane < hd
    sq = blk * blk
    s_lo = jnp.sum(jnp.where(lo, sq, 0.0), axis=-1, keepdims=True)
    s_hi = jnp.sum(jnp.where(lo, 0.0, sq), axis=-1, keepdims=True)
    inv = lax.rsqrt(jnp.where(lo, s_lo, s_hi) * (1.0 / hd) + RMS_EPS)
    return blk * inv * gain2


def _fox_proj_kernel(x_ref, sc_ref, sh_ref, gain_ref, w_ref, wfh_ref, wfl_ref, bf_ref, qg_ref, kg_ref,
                     q_ref, k_ref, v_ref, kb_ref, vb_ref, sg_ref, lf_ref, *, n_heads):
    D = x_ref.shape[1]
    hd = D // n_heads
    h = _modnorm(x_ref[...], gain_ref[...], sc_ref[...], sh_ref[...])
    hh, hl = _split2(h)
    qk_scale = hd ** -0.5
    q = _dot(hh, w_ref[:, 0:D])
    for c in range(D // LANES):
        sl = slice(c * LANES, (c + 1) * LANES)
        q_ref[:, sl] = (_pair_headnorm(q[:, sl], qg_ref[...], hd) * qk_scale).astype(BF16)
    k = _dot(hh, w_ref[:, D:2 * D])
    for c in range(D // LANES):
        sl = slice(c * LANES, (c + 1) * LANES)
        kn = _pair_headnorm(k[:, sl], kg_ref[...], hd)
        k_ref[:, sl] = kn
        kb_ref[:, sl] = kn.astype(BF16)
    v = _dot(hh, w_ref[:, 2 * D:3 * D])
    v_ref[...] = v
    vb_ref[...] = v.astype(BF16)
    sg_ref[...] = jax.nn.sigmoid(_dot(hh, w_ref[:, 3 * D:4 * D]))
    f = _dot(hh, wfh_ref[...]) + _dot(hh, wfl_ref[...]) + _dot(hl, wfh_ref[...])
    lf_ref[...] = _log_sigmoid(f[:, 0:n_heads] + bf_ref[...])


def fox_project(x, scale, shift, gain, w_in, b_f, q_gain, k_gain, *, tm, tiles_per_group):
    N, D = x.shape
    H = b_f.shape[0]
    hd = D // H
    assert 2 * hd == LANES
    R = scale.shape[1]
    w_main = jnp.concatenate([w_in[:, :3 * D], w_in[:, 3 * D + H:]], axis=1).astype(BF16)
    w_f = jnp.pad(w_in[:, 3 * D:3 * D + H], ((0, 0), (0, LANES - H)))
    wfh = w_f.astype(BF16)
    wfl = (w_f - wfh.astype(F32)).astype(BF16)
    const = lambda i: (0, 0)
    row = lambda i: (i, 0)
    outs = pl.pallas_call(
        functools.partial(_fox_proj_kernel, n_heads=H),
        grid=(N // tm,),
        in_specs=[pl.BlockSpec((tm, D), row),
                  _mod_spec(R, D, tiles_per_group), _mod_spec(R, D, tiles_per_group),
                  pl.BlockSpec((1, D), const),
                  pl.BlockSpec((D, 4 * D), const),
                  pl.BlockSpec((D, LANES), const), pl.BlockSpec((D, LANES), const),
                  pl.BlockSpec((1, H), const),
                  pl.BlockSpec((1, LANES), const), pl.BlockSpec((1, LANES), const)],
        out_specs=[pl.BlockSpec((tm, D), row)] * 6 + [pl.BlockSpec((tm, H), row)],
        out_shape=[jax.ShapeDtypeStruct((N, D), BF16),
                   jax.ShapeDtypeStruct((N, D), F32), jax.ShapeDtypeStruct((N, D), F32),
                   jax.ShapeDtypeStruct((N, D), BF16), jax.ShapeDtypeStruct((N, D), BF16),
                   jax.ShapeDtypeStruct((N, D), F32),
                   jax.ShapeDtypeStruct((N, H), F32)],
        compiler_params=_params("arbitrary"),
        name="fox_project",
    )(x, scale, shift, gain.reshape(1, D), w_main, wfh, wfl, b_f.reshape(1, H),
      jnp.tile(q_gain, 2).reshape(1, LANES), jnp.tile(k_gain, 2).reshape(1, LANES))
    return outs


def _fox_attn_kernel(q_ref, k_ref, v_ref, fc_ref, fr_ref, o_ref, *, hd, n_sub, hps):
    TQ = q_ref.shape[0]
    SQ = TQ // n_sub
    qi = pl.program_id(2)
    lane = lax.broadcasted_iota(I32, (SQ, LANES), 1)
    row = lax.broadcasted_iota(I32, (SQ, TQ), 0)
    col = lax.broadcasted_iota(I32, (SQ, TQ), 1)
    chains = [(hh, sb) for hh in range(hps) for sb in range(n_sub)]
    nch = len(chains)
    qms, fqs = [], []
    for hh, sb in chains:
        pb, ph = hh // 2, hh % 2
        q = q_ref[sb * SQ:(sb + 1) * SQ, pb * LANES:(pb + 1) * LANES]
        qms.append(jnp.where((lane >= ph * hd) & (lane < (ph + 1) * hd), q, jnp.zeros_like(q)))
        fqs.append(fc_ref[sb * SQ:(sb + 1) * SQ, hh:hh + 1])

    def block_scores(j):
        start = pl.multiple_of(j * TQ, TQ)
        kjs = [k_ref[pl.ds(start, TQ), pb * LANES:(pb + 1) * LANES] for pb in range(hps // 2)]
        return tuple(_dot_nt(qms[c], kjs[hh // 2]) + (fqs[c] - fr_ref[hh:hh + 1, pl.ds(start, TQ)])
                     for c, (hh, sb) in enumerate(chains))

    def update(j, state, scores, masked):
        start = pl.multiple_of(j * TQ, TQ)
        vjs = [v_ref[pl.ds(start, TQ), pb * LANES:(pb + 1) * LANES] for pb in range(hps // 2)]
        new = []
        for c, (hh, sb) in enumerate(chains):
            m, l, acc = state[c]
            s = jnp.where(col <= row + sb * SQ, scores[c], NEG_INF) if masked else scores[c]
            m_new = jnp.maximum(m, jnp.max(s, axis=-1, keepdims=True))
            alpha = jnp.exp(m - m_new)
            p = jnp.exp(s - m_new)
            new.append((m_new, alpha * l + jnp.sum(p, axis=-1, keepdims=True),
                        alpha * acc + _dot(p.astype(BF16), vjs[hh // 2])))
        return tuple(new)

    def step(j, carry):
        state, scores = carry
        nxt = block_scores(j + 1)
        return update(j, state, scores, False), nxt

    init = tuple((jnp.full((SQ, 1), NEG_INF, F32), jnp.zeros((SQ, 1), F32), jnp.zeros((SQ, LANES), F32))
                 for _ in chains)
    state, scores = lax.fori_loop(0, qi, step, (init, block_scores(0)))
    final = update(qi, state, scores, True)
    for pb in range(hps // 2):
        for sb in range(n_sub):
            even = final[(2 * pb) * n_sub + sb]
            odd = final[(2 * pb + 1) * n_sub + sb]
            o_ref[sb * SQ:(sb + 1) * SQ, pb * LANES:(pb + 1) * LANES] = jnp.where(
                lane < hd, even[2] / even[1], odd[2] / odd[1])


def fox_attend_prompt(q, kb, vb, F, *, n_heads, tq=256, hps=2, n_sub=2):
    B, S, D = q.shape
    hd = D // n_heads
    assert 2 * hd == LANES and hps % 2 == 0 and n_heads % hps == 0
    HG = n_heads // hps
    W = hps * hd
    fcol = F.reshape(B, S, HG, hps).transpose(0, 2, 1, 3)
    frow = fcol.transpose(0, 1, 3, 2)
    return pl.pallas_call(
        functools.partial(_fox_attn_kernel, hd=hd, n_sub=n_sub, hps=hps),
        grid=(B, HG, S // tq),
        in_specs=[pl.BlockSpec((None, tq, W), lambda b, h, i: (b, i, h)),
                  pl.BlockSpec((None, S, W), lambda b, h, i: (b, 0, h)),
                  pl.BlockSpec((None, S, W), lambda b, h, i: (b, 0, h)),
                  pl.BlockSpec((None, None, tq, hps), lambda b, h, i: (b, h, i, 0)),
                  pl.BlockSpec((None, None, hps, S), lambda b, h, i: (b, h, 0, 0))],
        out_specs=pl.BlockSpec((None, tq, W), lambda b, h, i: (b, i, h)),
        out_shape=jax.ShapeDtypeStruct((B, S, D), F32),
        compiler_params=_params("arbitrary", "arbitrary", "arbitrary"),
        name="fox_attend_prompt",
    )(q, kb, vb, fcol, frow)


def _fox_decode_kernel(pt_ref, qbd_ref, *refs, n_heads, pp):
    del pt_ref
    kt_refs, vt_refs, cf_refs = refs[:pp], refs[pp:2 * pp], refs[2 * pp:3 * pp]
    kn_ref, vn_ref, cn_ref, o_ref, m_scr, l_scr, acc_scr, carry_scr = refs[3 * pp:]
    H = n_heads
    step = pl.program_id(1)
    n_steps = pl.num_programs(1)
    R, D = qbd_ref.shape
    T = R // H
    P = cf_refs[0].shape[1]

    @pl.when(step == 0)
    def _():
        m_scr[...] = jnp.full(m_scr.shape, NEG_INF, F32)
        l_scr[...] = jnp.zeros(l_scr.shape, F32)
        acc_scr[...] = jnp.zeros(acc_scr.shape, F32)
        carry_scr[...] = jnp.zeros(carry_scr.shape, F32)

    qbd = qbd_ref[...]

    def merge(m_i, l_i, acc_i):
        m = m_scr[...]
        m_new = jnp.maximum(m, m_i)
        a_old = jnp.exp(m - m_new)
        a_new = jnp.exp(m_i - m_new)
        l_scr[...] = a_old * l_scr[...] + a_new * l_i
        acc_scr[...] = a_old * acc_scr[...] + a_new * acc_i
        m_scr[...] = m_new

    def scores(s, fk):
        return s - jnp.concatenate([fk] * T, axis=0)

    carry = carry_scr[...]
    ss = []
    for i in range(pp):
        cf = cf_refs[i][...]
        ss.append(scores(_dot(qbd, kt_refs[i][...].astype(BF16)), cf + carry))
        carry = carry + cf[:, P - 1:P]
    carry_scr[...] = carry
    m_s = jnp.max(ss[0], axis=-1, keepdims=True)
    for s in ss[1:]:
        m_s = jnp.maximum(m_s, jnp.max(s, axis=-1, keepdims=True))
    l_s = jnp.zeros_like(m_s)
    acc_s = jnp.zeros(acc_scr.shape, F32)
    for i, s in enumerate(ss):
        pr = jnp.exp(s - m_s)
        l_s = l_s + jnp.sum(pr, axis=-1, keepdims=True)
        acc_s = acc_s + _dot_nt(pr.astype(BF16), vt_refs[i][...].astype(BF16))
    merge(m_s, l_s, acc_s)

    @pl.when(step == n_steps - 1)
    def _():
        s2 = scores(_dot_nt(qbd, kn_ref[...]), cn_ref[...] + carry_scr[...])
        key = lax.broadcasted_iota(I32, s2.shape, 1)
        t = lax.broadcasted_iota(I32, s2.shape, 0) // H
        s2 = jnp.where(key <= t, s2, NEG_INF)
        m2 = jnp.max(s2, axis=-1, keepdims=True)
        p2 = jnp.exp(s2 - m2)
        merge(m2, jnp.sum(p2, axis=-1, keepdims=True), _dot(p2.astype(BF16), vn_ref[...]))
        out = acc_scr[...] / l_scr[...]
        rh = lax.broadcasted_iota(I32, (R, D), 0) % H
        ch = lax.broadcasted_iota(I32, (R, D), 1) // (D // H)
        out = jnp.where(rh == ch, out, 0.0)
        o_ref[...] = jnp.sum(out.reshape(T, H, D), axis=1)


def fox_attend_sample(q, kb_new, vb_new, lf_new, cache_kt, cache_vt, cache_cf, page_table, j, *, n_heads):
    Bd, T, D = q.shape
    H = n_heads
    hd = D // H
    n_pages = page_table.shape[1]
    n_fox, n_pool, _, P = cache_cf.shape
    R = T * H
    pp = next(c for c in (16, 8, 4, 2, 1) if n_pages % c == 0)
    assert T <= P
    q4 = q.reshape(Bd, T, H, hd)
    eye = jnp.eye(H, dtype=q.dtype)
    qbd = (q4[:, :, :, None, :] * eye[None, None, :, :, None]).reshape(Bd, R, D)
    pad = ((0, 0), (0, P - T), (0, 0))
    kn = jnp.pad(kb_new, pad)
    vn = jnp.pad(vb_new, pad)
    cn = seg_cumsum(jnp.pad(lf_new.transpose(0, 2, 1), ((0, 0), (0, 0), (0, P - T))), P)
    pt = page_table.reshape(-1).astype(I32)

    def page_spec(shape, i):
        return pl.BlockSpec((None, None) + shape, lambda b, s, pt: (j, pt[b * n_pages + s * pp + i], 0, 0))

    per_seq = lambda shape: pl.BlockSpec((None,) + shape, lambda b, s, pt: (b, 0, 0))
    grid_spec = pltpu.PrefetchScalarGridSpec(
        num_scalar_prefetch=1,
        grid=(Bd, n_pages // pp),
        in_specs=([per_seq((R, D))]
                  + [page_spec((D, P), i) for i in range(pp)]
                  + [page_spec((D, P), i) for i in range(pp)]
                  + [page_spec((H, P), i) for i in range(pp)]
                  + [per_seq((P, D)), per_seq((P, D)), per_seq((H, P))]),
        out_specs=per_seq((T, D)),
        scratch_shapes=[pltpu.VMEM((R, 1), F32), pltpu.VMEM((R, 1), F32),
                        pltpu.VMEM((R, D), F32), pltpu.VMEM((H, 1), F32)],
    )
    return pl.pallas_call(
        functools.partial(_fox_decode_kernel, n_heads=H, pp=pp),
        grid_spec=grid_spec,
        out_shape=jax.ShapeDtypeStruct((Bd, T, D), F32),
        compiler_params=_params("arbitrary", "arbitrary"),
        name="fox_attend_sample",
    )(pt, qbd, *([cache_kt] * pp), *([cache_vt] * pp), *([cache_cf] * pp), kn, vn, cn)


def prepare_cache(cache_k, cache_v, cache_logf):
    n_fox, n_pool, P, H, hd = cache_k.shape
    kt = cache_k.transpose(0, 1, 3, 4, 2).reshape(n_fox, n_pool, H * hd, P)
    vt = cache_v.transpose(0, 1, 3, 4, 2).reshape(n_fox, n_pool, H * hd, P)
    rows = n_fox * n_pool * H
    group = 2048 if rows % 2048 == 0 else H
    cf = seg_cumsum(cache_logf.transpose(0, 1, 3, 2).reshape(rows // group, group, P), P)
    return kt, vt, cf.reshape(n_fox, n_pool, H, P)


def _outproj_kernel(a_ref, gate_ref, w_ref, x_ref, g_ref, hn_ref, o_ref, *, head_norm):
    a = a_ref[...]
    D = a.shape[1]
    if head_norm:
        blocks = []
        for c in range(D // LANES):
            sl = slice(c * LANES, (c + 1) * LANES)
            blk = a[:, sl]
            ms = jnp.mean(blk * blk, axis=-1, keepdims=True)
            blocks.append(blk * lax.rsqrt(ms + RMS_EPS) * hn_ref[:, sl])
        a = jnp.concatenate(blocks, axis=-1)
    y = (a * gate_ref[...]).astype(BF16)
    o_ref[...] = x_ref[...] + g_ref[...] * _dot(y, w_ref[...])


def out_project(a, gate, w_out, x, g, head_gain, *, tm, tiles_per_group, head_norm):
    N, D = x.shape
    R = g.shape[1]
    row = lambda i: (i, 0)
    const = lambda i: (0, 0)
    return pl.pallas_call(
        functools.partial(_outproj_kernel, head_norm=head_norm),
        grid=(N // tm,),
        in_specs=[pl.BlockSpec((tm, D), row), pl.BlockSpec((tm, D), row),
                  pl.BlockSpec((D, D), const), pl.BlockSpec((tm, D), row),
                  _mod_spec(R, D, tiles_per_group), pl.BlockSpec((1, D), const)],
        out_specs=pl.BlockSpec((tm, D), row),
        out_shape=jax.ShapeDtypeStruct((N, D), F32),
        compiler_params=_params("arbitrary"),
        name="out_project",
    )(a, gate, w_out.astype(BF16), x, g, head_gain.reshape(1, D))


def _ml_proj_kernel(x_ref, sc_ref, sh_ref, gain_ref, w_ref, wgh_ref, wgl_ref, bi_ref, bf_ref,
                    q_ref, k_ref, v_ref, so_ref, il_ref, fl_ref, *, n_heads, qk_width):
    D = x_ref.shape[1]
    QW = qk_width
    h = _modnorm(x_ref[...], gain_ref[...], sc_ref[...], sh_ref[...])
    hh, hl = _split2(h)
    k_scale = (QW // n_heads) ** -0.5
    q_ref[...] = _dot(hh, w_ref[:, 0:QW]).astype(BF16)
    k_ref[...] = (_dot(hh, w_ref[:, QW:2 * QW]) * k_scale).astype(BF16)
    v_ref[...] = _dot(hh, w_ref[:, 2 * QW:2 * QW + D])
    so_ref[...] = jax.nn.sigmoid(_dot(hh, w_ref[:, 2 * QW + D:2 * QW + 2 * D]))
    gts = _dot(hh, wgh_ref[...]) + _dot(hh, wgl_ref[...]) + _dot(hl, wgh_ref[...])
    il_ref[...] = gts[:, 0:n_heads] + bi_ref[...]
    fl_ref[...] = _log_sigmoid(gts[:, n_heads:2 * n_heads] + bf_ref[...])


def ml_project(x, scale, shift, gain, w_in, b_i, b_f, *, tm, tiles_per_group, v_width):
    N, D = x.shape
    H = b_i.shape[0]
    QW = (w_in.shape[1] - 2 * v_width - 2 * H) // 2
    R = scale.shape[1]
    g0 = 2 * QW + v_width
    w_main = jnp.concatenate([w_in[:, :g0], w_in[:, g0 + 2 * H:]], axis=1).astype(BF16)
    w_g = jnp.pad(w_in[:, g0:g0 + 2 * H], ((0, 0), (0, LANES - 2 * H)))
    wgh = w_g.astype(BF16)
    wgl = (w_g - wgh.astype(F32)).astype(BF16)
    const = lambda i: (0, 0)
    row = lambda i: (i, 0)
    WM = w_main.shape[1]
    return pl.pallas_call(
        functools.partial(_ml_proj_kernel, n_heads=H, qk_width=QW),
        grid=(N // tm,),
        in_specs=[pl.BlockSpec((tm, D), row),
                  _mod_spec(R, D, tiles_per_group), _mod_spec(R, D, tiles_per_group),
                  pl.BlockSpec((1, D), const),
                  pl.BlockSpec((D, WM), const),
                  pl.BlockSpec((D, LANES), const), pl.BlockSpec((D, LANES), const),
                  pl.BlockSpec((1, H), const), pl.BlockSpec((1, H), const)],
        out_specs=[pl.BlockSpec((tm, QW), row), pl.BlockSpec((tm, QW), row),
                   pl.BlockSpec((tm, v_width), row), pl.BlockSpec((tm, v_width), row),
                   pl.BlockSpec((tm, H), row), pl.BlockSpec((tm, H), row)],
        out_shape=[jax.ShapeDtypeStruct((N, QW), BF16), jax.ShapeDtypeStruct((N, QW), BF16),
                   jax.ShapeDtypeStruct((N, v_width), F32), jax.ShapeDtypeStruct((N, v_width), F32),
                   jax.ShapeDtypeStruct((N, H), F32), jax.ShapeDtypeStruct((N, H), F32)],
        compiler_params=_params("arbitrary"),
        name="ml_project",
    )(x, scale, shift, gain.reshape(1, D), w_main, wgh, wgl, b_i.reshape(1, H), b_f.reshape(1, H))


def _mlstm_kernel(q_ref, k_ref, v_ref, ic_ref, ir_ref, bc_ref, br_ref, c0_ref, n0_ref, m0_ref,
                  h_ref, c_ref, n_ref, m_ref, *, n_heads, dk):
    nbb, L = q_ref.shape[0], q_ref.shape[1]
    ci = pl.program_id(1)

    @pl.when(ci == 0)
    def _():
        c_ref[...] = c0_ref[...]
        n_ref[...] = n0_ref[...]
        m_ref[...] = m0_ref[...]

    tri = lax.broadcasted_iota(I32, (L, L), 1) <= lax.broadcasted_iota(I32, (L, L), 0)
    lane = lax.broadcasted_iota(I32, (L, LANES), 1)
    lane1 = lax.broadcasted_iota(I32, (1, LANES), 1)
    rsub = lax.broadcasted_iota(I32, (LANES, LANES), 0)
    dv = LANES
    m_in = [m_ref[bb] for bb in range(nbb)]
    units = [(bb, h) for bb in range(nbb) for h in range(n_heads)]
    pairs = [(bb, pr) for bb in range(nbb) for pr in range(n_heads // 2)]
    qp = {(bb, pr): q_ref[bb, :, pr * LANES:(pr + 1) * LANES] for bb, pr in pairs}
    kp = {(bb, pr): k_ref[bb, :, pr * LANES:(pr + 1) * LANES] for bb, pr in pairs}
    c_in = {u: c_ref[u[0], u[1]] for u in pairs}
    n_in = {u: n_ref[u[0], u[1]] for u in pairs}
    sel = [(lane >= hh * dk) & (lane < (hh + 1) * dk) for hh in range(2)]
    qm = {(bb, h): jnp.where(sel[h % 2], qp[(bb, h // 2)], jnp.zeros_like(qp[(bb, h // 2)])) for bb, h in units}
    qk = {(bb, h): _dot_nt(qm[(bb, h)], kp[(bb, h // 2)]) for bb, h in units}
    qc = {(bb, h): _dot(qm[(bb, h)], c_in[(bb, h // 2)].astype(BF16)) for bb, h in units}
    st = {}
    for bb, h in units:
        b_l = bc_ref[bb][:, h:h + 1]
        b_s = br_ref[bb][h:h + 1, :]
        ig_s = ir_ref[bb][h:h + 1, :]
        ig_l = ic_ref[bb][:, h:h + 1]
        m = m_in[bb][:, h:h + 1]
        dm = jnp.where(tri, b_l - b_s + ig_s, NEG_INF)
        inter = b_l + m
        m_t = jnp.maximum(inter, jnp.max(dm, axis=-1, keepdims=True))
        w_inter = jnp.exp(inter - m_t)
        a = jnp.exp(dm - m_t) * qk[(bb, h)]
        b_last = b_l[L - 1:L, :]
        m_new = jnp.maximum(b_last + m, jnp.max(b_last - b_s + ig_s, axis=-1, keepdims=True))
        wk = jnp.exp(b_last - b_l + ig_l - m_new)
        decay = jnp.exp(b_last + m - m_new)
        vh = v_ref[bb, :, h * dv:(h + 1) * dv]
        st[(bb, h)] = (a, m_t, w_inter, m_new, wk, decay, vh)
    av = {u: _dot(st[u][0].astype(BF16), st[u][6].astype(BF16)) for u in units}
    upd = {(bb, h): _dot_tn(kp[(bb, h // 2)], (st[(bb, h)][4] * st[(bb, h)][6]).astype(BF16)) for bb, h in units}
    for bb, pr in pairs:
        c_pair, n_pair = c_in[(bb, pr)], n_in[(bb, pr)]
        qpf = qp[(bb, pr)].astype(F32)
        kpf = kp[(bb, pr)].astype(F32)
        new_c, new_n = [], []
        for hh in range(2):
            h = 2 * pr + hh
            a, m_t, w_inter, m_new, wk, decay, vh = st[(bb, h)]
            num = w_inter * qc[(bb, h)] + av[(bb, h)]
            qn = jnp.sum(jnp.where(sel[hh], qpf, 0.0) * n_pair, axis=-1, keepdims=True)
            den = w_inter * qn + jnp.sum(a, axis=-1, keepdims=True)
            h_ref[bb, :, h * dv:(h + 1) * dv] = num / jnp.maximum(jnp.abs(den), jnp.exp(-m_t))
            new_c.append(decay * c_pair + upd[(bb, h)])
            new_n.append(decay * n_pair + jnp.sum(wk * kpf, axis=0, keepdims=True))
            m_ref[bb, :, h:h + 1] = m_new
        c_ref[bb, pr] = jnp.where(rsub < dk, new_c[0], new_c[1])
        n_ref[bb, pr] = jnp.where(lane1 < dk, new_n[0], new_n[1])


def mlstm_chunked(q, k, v, i_log, f_log, C0, n0, m0, chunk):
    B, T, _ = q.shape
    H = i_log.shape[-1]
    dk = C0.shape[2]
    dv = C0.shape[3]
    assert 2 * dk == LANES and dv == LANES
    nc = T // chunk
    L = chunk
    Tp = -(-T // LANES) * LANES
    fT = jnp.pad(f_log.transpose(0, 2, 1), ((0, 0), (0, 0), (0, Tp - T)))
    bT = seg_cumsum(fT, chunk)[:, :, :T]
    br = bT.reshape(B, H, nc, L).transpose(0, 2, 1, 3)
    bc = br.transpose(0, 1, 3, 2)
    ic = i_log.reshape(B, nc, L, H)
    ir = ic.transpose(0, 1, 3, 2)
    HP = H // 2
    c0 = C0.reshape(B, HP, 2 * dk, dv)
    n0p = n0.reshape(B, HP, 1, 2 * dk)
    m0p = m0.reshape(B, 1, H)
    QW = H * dk
    VW = H * dv
    nbb = 2 if B % 2 == 0 else 1
    col_spec = pl.BlockSpec((nbb, None, L, H), lambda b, c: (b, c, 0, 0))
    row_spec = pl.BlockSpec((nbb, None, H, L), lambda b, c: (b, c, 0, 0))
    st = lambda b, c: (b, 0, 0, 0)
    h, Cn, nn, mn = pl.pallas_call(
        functools.partial(_mlstm_kernel, n_heads=H, dk=dk),
        grid=(B // nbb, nc),
        in_specs=[pl.BlockSpec((nbb, L, QW), lambda b, c: (b, c, 0)),
                  pl.BlockSpec((nbb, L, QW), lambda b, c: (b, c, 0)),
                  pl.BlockSpec((nbb, L, VW), lambda b, c: (b, c, 0)),
                  col_spec, row_spec, col_spec, row_spec,
                  pl.BlockSpec((nbb, HP, 2 * dk, dv), st),
                  pl.BlockSpec((nbb, HP, 1, 2 * dk), st),
                  pl.BlockSpec((nbb, 1, H), lambda b, c: (b, 0, 0))],
        out_specs=[pl.BlockSpec((nbb, L, VW), lambda b, c: (b, c, 0)),
                   pl.BlockSpec((nbb, HP, 2 * dk, dv), st),
                   pl.BlockSpec((nbb, HP, 1, 2 * dk), st),
                   pl.BlockSpec((nbb, 1, H), lambda b, c: (b, 0, 0))],
        out_shape=[jax.ShapeDtypeStruct((B, T, VW), F32),
                   jax.ShapeDtypeStruct((B, HP, 2 * dk, dv), F32),
                   jax.ShapeDtypeStruct((B, HP, 1, 2 * dk), F32),
                   jax.ShapeDtypeStruct((B, 1, H), F32)],
        compiler_params=_params("arbitrary", "arbitrary"),
        name="mlstm_chunked",
    )(q, k, v, ic, ir, bc, br, c0, n0p, m0p)
    return h, Cn.reshape(B, H, dk, dv), nn.reshape(B, H, dk), mn.reshape(B, H)


def _candidate_tiles(n):
    runs = []
    for a in range(n // 2):
        cnt = n // (a + 1)
        for b0 in range(0, cnt, 8):
            runs.append((a, b0, 0, min(8, cnt - b0)))
    runs.append((-1, 0, 0, n // 2))
    tiles, pos = {}, 0
    for a, b0, src, cnt in runs:
        while cnt:
            take = min(cnt, 8 - pos % 8)
            tiles.setdefault(pos // 8, []).append((a, b0, (pos - src) % 8, pos % 8, pos % 8 + take))
            pos, src, cnt = pos + take, src + take, cnt - take
    return [tiles[q] for q in sorted(tiles)]


def _extract_topk(ss, payloads, n):
    ss = list(ss)
    R = ss[0].shape[0]
    ridx = lax.broadcasted_iota(I32, ss[0].shape, 0).astype(F32)
    vals = [[] for _ in ss]
    pays = [[] for _ in ss]
    for it in range(n):
        for c in range(len(ss)):
            s = ss[c]
            m = jnp.max(s, axis=0, keepdims=True)
            am = jnp.min(jnp.where(s == m, ridx, float(R)), axis=0, keepdims=True)
            hit = ridx == am
            if payloads[c] is None:
                pays[c].append(am)
            else:
                pays[c].append(jnp.max(jnp.where(hit, payloads[c], -1.0), axis=0, keepdims=True))
            vals[c].append(m)
            if it + 1 < n:
                ss[c] = jnp.where(hit, NEG_INF, s)
    return vals, pays


def _peer_topk_kernel(x_ref, sc_ref, sh_ref, gain_ref, wq_ref, k1_ref, k2_ref,
                      hb_ref, e_ref, g_ref, q_scr, v1_scr, i1_scr, v2_scr, i2_scr, *, n_heads, n_keys):
    TM = x_ref.shape[0]
    K = PEER_TOPK
    half = k1_ref.shape[2]
    hb = _modnorm(x_ref[...], gain_ref[...], sc_ref[...], sh_ref[...]).astype(BF16)
    hb_ref[...] = hb
    q_scr[...] = _dot(hb, wq_ref[...]).astype(BF16)
    sub8 = lax.broadcasted_iota(I32, (8, LANES), 0)

    def head_body(hd, _):
        c0 = pl.multiple_of(hd * 2 * half, 2 * half)
        c1 = pl.multiple_of(hd * 2 * half + half, half)
        s1 = _dot_nt(k1_ref[hd], q_scr[:, pl.ds(c0, half)])
        s2 = _dot_nt(k2_ref[hd], q_scr[:, pl.ds(c1, half)])
        NB = TM // LANES
        blocks = [slice(blk * LANES, (blk + 1) * LANES) for blk in range(NB)]
        vals, idxs = _extract_topk([s[:, ls] for ls in blocks for s in (s1, s2)], [None] * (2 * NB), K)
        for blk in range(NB):
            for side, (v_scr, i_scr) in enumerate(((v1_scr, i1_scr), (v2_scr, i2_scr))):
                for r in range(K):
                    v_scr[blk, r:r + 1, :] = vals[2 * blk + side][r]
                    i_scr[blk, r:r + 1, :] = idxs[2 * blk + side][r]
        cands, pays = [], []
        for blk in range(NB):
            pieces = {}

            def piece(a, b0, shift, blk=blk, pieces=pieces):
                key = (a, b0, shift)
                if key not in pieces:
                    if a >= 0:
                        cv = v1_scr[blk, a:a + 1, :] + v2_scr[blk, b0:b0 + 8, :]
                        pv = i1_scr[blk, a:a + 1, :] * n_keys + i2_scr[blk, b0:b0 + 8, :]
                    else:
                        cv = v1_scr[blk, K // 2:K, :] + v2_scr[blk, 0:1, :]
                        pv = i1_scr[blk, K // 2:K, :] * n_keys + i2_scr[blk, 0:1, :]
                    if shift:
                        cv, pv = pltpu.roll(cv, shift, 0), pltpu.roll(pv, shift, 0)
                    pieces[key] = (cv, pv)
                return pieces[key]

            cand, pay = [], []
            for segs in _candidate_tiles(K):
                cv = jnp.full((8, LANES), NEG_INF, F32)
                pv = jnp.zeros((8, LANES), F32)
                for a, b0, shift, lo, hi in segs:
                    pc, pp_ = piece(a, b0, shift)
                    if (lo, hi) == (0, 8):
                        cv, pv = pc, pp_
                    else:
                        inside = (sub8 >= lo) & (sub8 < hi)
                        cv, pv = jnp.where(inside, pc, cv), jnp.where(inside, pp_, pv)
                cand.append(cv)
                pay.append(pv)
            cands.append(jnp.concatenate(cand, axis=0))
            pays.append(jnp.concatenate(pay, axis=0))
        tops, topes = _extract_topk(cands, pays, K)
        out_rows = pl.ds(pl.multiple_of(hd * K, K), K)
        for blk in range(NB):
            top_s, top_e = tops[blk], topes[blk]
            ex = [jnp.exp(t - top_s[0]) for t in top_s]
            den = ex[0]
            for t in ex[1:]:
                den = den + t
            inv = 1.0 / den
            for r in range(K):
                i1_scr[blk, r:r + 1, :] = top_e[r]
                v1_scr[blk, r:r + 1, :] = ex[r] * inv
            e_ref[out_rows, blocks[blk]] = i1_scr[blk].astype(I32)
            g_ref[out_rows, blocks[blk]] = v1_scr[blk]
        return 0

    lax.fori_loop(0, n_heads, head_body, 0)


def peer_retrieve(x, scale, shift, gain, w_q, sub_k1, sub_k2, *, tm, tiles_per_group):
    N, D = x.shape
    HP, NK, half = sub_k1.shape
    assert NK == LANES and half == LANES and PEER_TOPK % 8 == 0
    R = scale.shape[1]
    QW = w_q.shape[1]
    J = HP * PEER_TOPK
    const = lambda i: (0, 0)
    const3 = lambda i: (0, 0, 0)
    kspec = pl.BlockSpec((HP, NK, half), const3)
    return pl.pallas_call(
        functools.partial(_peer_topk_kernel, n_heads=HP, n_keys=NK),
        grid=(N // tm,),
        in_specs=[pl.BlockSpec((tm, D), lambda i: (i, 0)),
                  _mod_spec(R, D, tiles_per_group), _mod_spec(R, D, tiles_per_group),
                  pl.BlockSpec((1, D), const),
                  pl.BlockSpec((D, QW), const),
                  kspec, kspec],
        out_specs=[pl.BlockSpec((tm, D), lambda i: (i, 0)),
                   pl.BlockSpec((J, tm), lambda i: (0, i)),
                   pl.BlockSpec((J, tm), lambda i: (0, i))],
        out_shape=[jax.ShapeDtypeStruct((N, D), BF16),
                   jax.ShapeDtypeStruct((J, N), I32),
                   jax.ShapeDtypeStruct((J, N), F32)],
        scratch_shapes=[pltpu.VMEM((tm, QW), BF16),
                        ] + [pltpu.VMEM((tm // LANES, PEER_TOPK, LANES), F32)] * 4,
        compiler_params=_params("arbitrary"),
        name="peer_retrieve",
    )(x, scale, shift, gain.reshape(1, D), w_q.astype(BF16), sub_k1.astype(BF16), sub_k2.astype(BF16))


def _peer_mix_kernel(hb_ref, i1n_ref, i2n_ref, gn_ref, i1c_ref, i2c_ref, ut_ref, v_ref, x_ref, g2_ref, o_ref,
                     a_scr, w_scr, hbuf_scr, y_scr, *, n_keys, tg):
    TM = hb_ref.shape[0]
    TE = ut_ref.shape[1]
    J = i1n_ref.shape[1]
    NE = pl.num_programs(1) // 2
    NT = pl.num_programs(0) - 1
    t = pl.program_id(0)
    j = pl.program_id(1)
    NP2 = n_keys // 2
    W2 = 2 * n_keys
    per_step = TM // NE
    rows = lax.broadcasted_iota(I32, (NP2, J), 0)
    cols = lax.broadcasted_iota(I32, (W2, J), 0)

    n_groups = per_step // tg

    def group_grids(g):
        t0 = pl.multiple_of(j * per_step + g * tg, tg)
        grids = []
        for u in range(tg):
            i1 = i1c_ref[pl.ds(t0 + u, 1), :]
            i2 = i2c_ref[pl.ds(t0 + u, 1), :]
            w = w_scr[pl.ds(t0 + u, 1), :]
            p1 = jnp.where(rows == (i1 >> 1), w, 0.0).astype(BF16)
            p2 = jnp.where(cols == ((i1 & 1) * n_keys + i2), 1.0, 0.0).astype(BF16)
            grids.append(_dot_nt(p1, p2).astype(BF16))
        return jnp.stack(grids, axis=0)

    def group_store(g, stacked):
        t0 = pl.multiple_of(j * per_step + g * tg, tg)
        hbuf_scr[:, pl.ds(t0, tg), :] = pltpu.einshape("tpc->ptc", stacked)

    def build_slice():
        for g in range(n_groups):
            group_store(g, group_grids(g))

    def activations(with_build=False):
        CW = MXU_DIM
        n_chunks = TE // CW
        hb = hb_ref[...]
        i1 = i1n_ref[...]
        i2 = i2n_ref[...]
        acc = a_scr[...]
        events = {}
        if with_build:
            for e in range(2 * n_groups):
                events.setdefault(min(e * n_chunks // (2 * n_groups), n_chunks - 1), []).append(e)
        pending = {}

        def pick(acc, k, act):
            for c in range(CW // n_keys):
                r = j * (TE // n_keys) + k * (CW // n_keys) + c
                acc = jnp.where(i1 == r, jnp.take_along_axis(act[:, c * n_keys:(c + 1) * n_keys], i2, axis=1), acc)
            return acc

        prev = None
        for k in range(n_chunks):
            act = _dot(hb, ut_ref[:, k * CW:(k + 1) * CW])
            for e in events.get(k, []):
                if e % 2 == 0:
                    pending[e // 2] = group_grids(e // 2)
                else:
                    group_store(e // 2, pending.pop(e // 2))
            if prev is not None:
                acc = pick(acc, k - 1, prev)
            prev = act
        a_scr[...] = pick(acc, n_chunks - 1, prev)

    @pl.when((j == 0) & (t < NT))
    def _():
        a_scr[...] = jnp.zeros(a_scr.shape, F32)

    @pl.when((j < NE) & (t > 0) & (t < NT))
    def _():
        activations(with_build=True)

    @pl.when((j < NE) & (t == 0))
    def _():
        activations()

    @pl.when((j < NE) & (t == NT))
    def _():
        build_slice()

    @pl.when((j == NE - 1) & (t < NT))
    def _():
        a = a_scr[...]
        w_scr[...] = gn_ref[...] * (0.5 * a * (1.0 + lax.erf(a * math.sqrt(0.5))))

    @pl.when((j == NE) & (t > 0))
    def _():
        y_scr[...] = jnp.zeros(y_scr.shape, F32)

    @pl.when((j >= NE) & (t > 0))
    def _():
        jj = j - NE
        y = y_scr[...]
        for c in range(TE // W2):
            y = y + _dot(hbuf_scr[jj * (TE // W2) + c], v_ref[c * W2:(c + 1) * W2, :])
        y_scr[...] = y

    @pl.when((j == 2 * NE - 1) & (t > 0))
    def _():
        o_ref[...] = x_ref[...] + g2_ref[...] * y_scr[...]


def peer_mix(hb, i1, i2, g, ut_tab, v_tab, x, g2, *, tm, te, tiles_per_group, n_keys):
    N, D = x.shape
    E = v_tab.shape[0]
    J = i1.shape[1]
    NE = E // te
    NT = N // tm
    R = g2.shape[1]
    tg = 16
    assert te % (2 * n_keys) == 0 and n_keys == LANES and tm % (NE * tg) == 0
    nxt = lambda t, j: (jnp.minimum(t, NT - 1), 0)
    cur = lambda t, j: (jnp.maximum(t - 1, 0), 0)
    return pl.pallas_call(
        functools.partial(_peer_mix_kernel, n_keys=n_keys, tg=tg),
        grid=(NT + 1, 2 * NE),
        in_specs=[pl.BlockSpec((tm, D), nxt),
                  pl.BlockSpec((tm, J), nxt), pl.BlockSpec((tm, J), nxt), pl.BlockSpec((tm, J), nxt),
                  pl.BlockSpec((tm, J), cur), pl.BlockSpec((tm, J), cur),
                  pl.BlockSpec((D, te), lambda t, j: (0, jnp.where(t < NT, jnp.minimum(j, NE - 1), NE - 1))),
                  pl.BlockSpec((te, D), lambda t, j: (jnp.where(t > 0, jnp.maximum(j - NE, 0), 0), 0)),
                  pl.BlockSpec((tm, D), cur),
                  pl.BlockSpec((None, R, D), lambda t, j: (jnp.maximum(t - 1, 0) // tiles_per_group, 0, 0))],
        out_specs=pl.BlockSpec((tm, D), cur),
        out_shape=jax.ShapeDtypeStruct((N, D), F32),
        scratch_shapes=[pltpu.VMEM((tm, J), F32), pltpu.VMEM((tm, J), F32),
                        pltpu.VMEM((n_keys // 2, tm, 2 * n_keys), BF16),
                        pltpu.VMEM((tm, D), F32)],
        compiler_params=_params("arbitrary", "arbitrary"),
        name="peer_mix",
    )(hb, i1, i2, g, i1, i2, ut_tab, v_tab, x, g2)


def _trunk(x3, mods, past, params, tables, *, tm):
    (norm_mix, norm_ffn, fox_w_in, fox_b_f, fox_q_norm, fox_k_norm, fox_w_out,
     ml_w_in, ml_b_i, ml_b_f, ml_h_norm, ml_w_out, peer_w_q, peer_sub_k1, peer_sub_k2) = params
    ut_bf, v_bf = tables
    B, T, D = x3.shape
    N = B * T
    depth = norm_mix.shape[0]
    n_fox_heads = fox_b_f.shape[1]
    n_keys = peer_sub_k1.shape[2]
    x = x3.reshape(N, D)
    if T % tm == 0:
        tpg, rows = T // tm, 1
        expand = lambda a: a.reshape(B, 1, D)
    else:
        assert N == tm
        tpg, rows = 1, N
        expand = lambda a: jnp.repeat(a, T, axis=0).reshape(1, N, D)
    big_tm = 2 * tm if tpg % 2 == 0 else tm
    big_tpg = tpg * tm // big_tm
    new_k, new_v, new_lf, new_C, new_n, new_m = [], [], [], [], [], []
    for layer in range(depth):
        sh1, sc1, g1, sh2, sc2, g2 = [expand(a) for a in jnp.split(mods[layer], 6, axis=-1)]
        j = layer // 2
        if layer % 2 == 0:
            q, k, v, kb, vb, sg, lf = fox_project(
                x, sc1, sh1, norm_mix[layer], fox_w_in[j], fox_b_f[j], fox_q_norm[j], fox_k_norm[j],
                tm=big_tm, tiles_per_group=big_tpg)
            if past is None:
                F = seg_cumsum(lf.reshape(B, T, n_fox_heads).transpose(0, 2, 1), T).transpose(0, 2, 1)
                o = fox_attend_prompt(q.reshape(B, T, D), kb.reshape(B, T, D), vb.reshape(B, T, D), F,
                                      n_heads=n_fox_heads)
            else:
                cache_kt, cache_vt, cache_cf, _, _, _, page_table = past
                o = fox_attend_sample(q.reshape(B, T, D), kb.reshape(B, T, D), vb.reshape(B, T, D),
                                      lf.reshape(B, T, n_fox_heads), cache_kt, cache_vt, cache_cf,
                                      page_table, j, n_heads=n_fox_heads)
            x = out_project(o.reshape(N, D), sg, fox_w_out[j], x, g1, jnp.ones((D,), F32),
                            tm=big_tm, tiles_per_group=big_tpg, head_norm=False)
            new_k.append(k.reshape(B, T, n_fox_heads, -1))
            new_v.append(v.reshape(B, T, n_fox_heads, -1))
            new_lf.append(lf.reshape(B, T, n_fox_heads))
        else:
            n_ml_heads = ml_b_i.shape[1]
            v_width = ml_w_out.shape[1]
            q, k, v, so, il, fl = ml_project(x, sc1, sh1, norm_mix[layer], ml_w_in[j], ml_b_i[j], ml_b_f[j],
                                             tm=big_tm, tiles_per_group=big_tpg, v_width=v_width)
            QW = q.shape[1]
            dk = QW // n_ml_heads
            dv = v_width // n_ml_heads
            q3, k3, v3 = q.reshape(B, T, QW), k.reshape(B, T, QW), v.reshape(B, T, v_width)
            il3, fl3 = il.reshape(B, T, n_ml_heads), fl.reshape(B, T, n_ml_heads)
            if past is None:
                C0 = jnp.zeros((B, n_ml_heads, dk, dv), F32)
                n0 = jnp.zeros((B, n_ml_heads, dk), F32)
                m0 = jnp.zeros((B, n_ml_heads), F32)
                hh, C, n_, m_ = mlstm_chunked(q3, k3, v3, il3, fl3, C0, n0, m0, 64)
            else:
                _, _, _, state_C, state_n, state_m, _ = past
                Tp = 16
                pad3 = ((0, 0), (0, Tp - T), (0, 0))
                hh, C, n_, m_ = mlstm_chunked(
                    jnp.pad(q3, pad3), jnp.pad(k3, pad3), jnp.pad(v3, pad3),
                    jnp.pad(il3, pad3, constant_values=-1e30), jnp.pad(fl3, pad3),
                    state_C[j], state_n[j], state_m[j], Tp)
                hh = hh[:, :T]
            x = out_project(hh.reshape(N, v_width), so, ml_w_out[j], x, g1, ml_h_norm[j].reshape(-1),
                            tm=big_tm, tiles_per_group=big_tpg, head_norm=True)
            new_C.append(C)
            new_n.append(n_)
            new_m.append(m_)
        hb, e, g = peer_retrieve(x, sc2, sh2, norm_ffn[layer], peer_w_q[layer], peer_sub_k1[layer],
                                 peer_sub_k2[layer], tm=big_tm, tiles_per_group=big_tpg)
        eT = e.T
        x = peer_mix(hb, eT // n_keys, eT % n_keys, g.T, ut_bf[layer], v_bf[layer], x, g2,
                     tm=big_tm, te=2048, tiles_per_group=big_tpg, n_keys=n_keys)
    return (x.reshape(B, T, D), jnp.stack(new_k), jnp.stack(new_v), jnp.stack(new_lf),
            jnp.stack(new_C), jnp.stack(new_n), jnp.stack(new_m))


def kernel(x_prompt, x_sample, cache_k, cache_v, cache_logf, state_C, state_n, state_m, page_table,
           c_prompt, c_sample, ada_w, ada_b, norm_mix, norm_ffn, fox_w_in, fox_b_f, fox_q_norm, fox_k_norm,
           fox_w_out, ml_w_in, ml_b_i, ml_b_f, ml_h_norm, ml_w_out, peer_w_q, peer_sub_k1, peer_sub_k2,
           peer_u, peer_v):
    params = (norm_mix, norm_ffn, fox_w_in, fox_b_f, fox_q_norm, fox_k_norm, fox_w_out,
              ml_w_in, ml_b_i, ml_b_f, ml_h_norm, ml_w_out, peer_w_q, peer_sub_k1, peer_sub_k2)
    tables = (peer_u.astype(BF16).transpose(0, 2, 1), peer_v.astype(BF16))
    Bp = c_prompt.shape[0]
    mods = adaln(jnp.concatenate([c_prompt, c_sample], axis=0), ada_w, ada_b)
    y_p, k_p, v_p, lf_p, C_p, n_p, m_p = _trunk(x_prompt, mods[:, :Bp], None, params, tables, tm=256)
    past = prepare_cache(cache_k, cache_v, cache_logf) + (state_C, state_n, state_m, page_table)
    y_s, k_s, v_s, lf_s, C_s, n_s, m_s = _trunk(x_sample, mods[:, Bp:], past, params, tables,
                                               tm=x_sample.shape[0] * x_sample.shape[1])
    return (y_p, y_s, k_p, v_p, lf_p, C_p, n_p, m_p, k_s, v_s, lf_s, C_s, n_s, m_s)
```

```python
import functools
import math

import jax
import jax.numpy as jnp
from jax import lax
from jax.experimental import pallas as pl
from jax.experimental.pallas import tpu as pltpu

F32 = jnp.float32
BF16 = jnp.bfloat16
I32 = jnp.int32

RMS_EPS = 1e-6
LANES = 128
MXU_DIM = 256
VMEM_LIMIT = 52 * 1024 * 1024
PEER_TOPK = 16
PAGE_SIZE = 128
NEG_INF = float("-inf")


def _dot(a, b):
    return jnp.dot(a, b, preferred_element_type=F32)


def _dot_nt(a, b):
    return lax.dot_general(a, b, (((1,), (1,)), ((), ())), preferred_element_type=F32)


def _dot_tn(a, b):
    return lax.dot_general(a, b, (((0,), (0,)), ((), ())), preferred_element_type=F32)


def _split2(a):
    hi = a.astype(BF16)
    lo = (a - hi.astype(F32)).astype(BF16)
    return hi, lo


def _dot3(a, b):
    ah, al = _split2(a)
    bh, bl = _split2(b)
    return _dot(ah, bh) + _dot(ah, bl) + _dot(al, bh)


def _modnorm(x, gain, scale, shift):
    ms = jnp.mean(x * x, axis=-1, keepdims=True)
    return x * lax.rsqrt(ms + RMS_EPS) * gain * (1.0 + scale) + shift


def _log_sigmoid(z):
    return jnp.minimum(z, 0.0) - jnp.log1p(jnp.exp(-jnp.abs(z)))


def _params(*sem):
    return pltpu.CompilerParams(dimension_semantics=sem, vmem_limit_bytes=VMEM_LIMIT)


def _mod_spec(rows, d, tiles_per_group):
    return pl.BlockSpec((None, rows, d), lambda i: (i // tiles_per_group, 0, 0))


def _adaln_kernel(c_ref, w_ref, b_ref, o_ref):
    c = c_ref[...]
    o_ref[...] = _dot3(c * jax.nn.sigmoid(c), w_ref[...]) + b_ref[...]


def adaln(c, ada_w, ada_b, tn=1536):
    L, D, D6 = ada_w.shape
    R = c.shape[0]
    return pl.pallas_call(
        _adaln_kernel,
        grid=(L, D6 // tn),
        in_specs=[pl.BlockSpec((R, D), lambda l, j: (0, 0)),
                  pl.BlockSpec((None, D, tn), lambda l, j: (l, 0, j)),
                  pl.BlockSpec((None, 1, tn), lambda l, j: (l, 0, j))],
        out_specs=pl.BlockSpec((None, R, tn), lambda l, j: (l, 0, j)),
        out_shape=jax.ShapeDtypeStruct((L, R, D6), F32),
        compiler_params=_params("arbitrary", "arbitrary"),
        name="adaln",
    )(c, ada_w, ada_b.reshape(L, 1, D6))


def _segcumsum_kernel(x_ref, o_ref, *, seg):
    C, T = x_ref.shape
    lane = lax.broadcasted_iota(I32, (C, LANES), 1)
    w = min(seg, LANES)
    carry = None
    for blk in range(T // LANES):
        x = x_ref[:, blk * LANES:(blk + 1) * LANES]
        k = 1
        while k < w:
            x = x + jnp.where((lane & (w - 1)) >= k, pltpu.roll(x, k, 1), 0.0)
            k *= 2
        if seg > LANES:
            if blk % (seg // LANES) != 0:
                x = x + carry
            carry = x[:, LANES - 1:LANES]
        o_ref[:, blk * LANES:(blk + 1) * LANES] = x


def seg_cumsum(x, seg):
    R, C, T = x.shape
    assert T % LANES == 0 and (seg & (seg - 1)) == 0
    assert T % seg == 0 and (seg <= LANES or seg % LANES == 0)
    return pl.pallas_call(
        functools.partial(_segcumsum_kernel, seg=seg),
        grid=(R,),
        in_specs=[pl.BlockSpec((None, C, T), lambda r: (r, 0, 0))],
        out_specs=pl.BlockSpec((None, C, T), lambda r: (r, 0, 0)),
        out_shape=jax.ShapeDtypeStruct((R, C, T), F32),
        compiler_params=_params("arbitrary"),
        name="seg_cumsum",
    )(x)


def _pair_headnorm(blk, gain2, hd):
    lane = lax.broadcasted_iota(I32, blk.shape, 1)
    lo = lane < hd
    sq = blk * blk
    s_lo = jnp.sum(jnp.where(lo, sq, 0.0), axis=-1, keepdims=True)
    s_hi = jnp.sum(jnp.where(lo, 0.0, sq), axis=-1, keepdims=True)
    inv = lax.rsqrt(jnp.where(lo, s_lo, s_hi) * (1.0 / hd) + RMS_EPS)
    return blk * inv * gain2


def _fox_proj_kernel(x_ref, sc_ref, sh_ref, gain_ref, w_ref, wfh_ref, wfl_ref, bf_ref, qg_ref, kg_ref,
                     q_ref, k_ref, v_ref, kb_ref, vb_ref, sg_ref, lf_ref, *, n_heads):
    D = x_ref.shape[1]
    hd = D // n_heads
    h = _modnorm(x_ref[...], gain_ref[...], sc_ref[...], sh_ref[...])
    hh, hl = _split2(h)
    qk_scale = hd ** -0.5
    q = _dot(hh, w_ref[:, 0:D])
    for c in range(D // LANES):
        sl = slice(c * LANES, (c + 1) * LANES)
        q_ref[:, sl] = (_pair_headnorm(q[:, sl], qg_ref[...], hd) * qk_scale).astype(BF16)
    k = _dot(hh, w_ref[:, D:2 * D])
    for c in range(D // LANES):
        sl = slice(c * LANES, (c + 1) * LANES)
        kn = _pair_headnorm(k[:, sl], kg_ref[...], hd)
        k_ref[:, sl] = kn
        kb_ref[:, sl] = kn.astype(BF16)
    v = _dot(hh, w_ref[:, 2 * D:3 * D])
    v_ref[...] = v
    vb_ref[...] = v.astype(BF16)
    sg_ref[...] = jax.nn.sigmoid(_dot(hh, w_ref[:, 3 * D:4 * D]))
    f = _dot(hh, wfh_ref[...]) + _dot(hh, wfl_ref[...]) + _dot(hl, wfh_ref[...])
    lf_ref[...] = _log_sigmoid(f[:, 0:n_heads] + bf_ref[...])


def fox_project(x, scale, shift, gain, w_in, b_f, q_gain, k_gain, *, tm, tiles_per_group):
    N, D = x.shape
    H = b_f.shape[0]
    hd = D // H
    assert 2 * hd == LANES
    R = scale.shape[1]
    w_main = jnp.concatenate([w_in[:, :3 * D], w_in[:, 3 * D + H:]], axis=1).astype(BF16)
    w_f = jnp.pad(w_in[:, 3 * D:3 * D + H], ((0, 0), (0, LANES - H)))
    wfh = w_f.astype(BF16)
    wfl = (w_f - wfh.astype(F32)).astype(BF16)
    const = lambda i: (0, 0)
    row = lambda i: (i, 0)
    outs = pl.pallas_call(
        functools.partial(_fox_proj_kernel, n_heads=H),
        grid=(N // tm,),
        in_specs=[pl.BlockSpec((tm, D), row),
                  _mod_spec(R, D, tiles_per_group), _mod_spec(R, D, tiles_per_group),
                  pl.BlockSpec((1, D), const),
                  pl.BlockSpec((D, 4 * D), const),
                  pl.BlockSpec((D, LANES), const), pl.BlockSpec((D, LANES), const),
                  pl.BlockSpec((1, H), const),
                  pl.BlockSpec((1, LANES), const), pl.BlockSpec((1, LANES), const)],
        out_specs=[pl.BlockSpec((tm, D), row)] * 6 + [pl.BlockSpec((tm, H), row)],
        out_shape=[jax.ShapeDtypeStruct((N, D), BF16),
                   jax.ShapeDtypeStruct((N, D), F32), jax.ShapeDtypeStruct((N, D), F32),
                   jax.ShapeDtypeStruct((N, D), BF16), jax.ShapeDtypeStruct((N, D), BF16),
                   jax.ShapeDtypeStruct((N, D), F32),
                   jax.ShapeDtypeStruct((N, H), F32)],
        compiler_params=_params("arbitrary"),
        name="fox_project",
    )(x, scale, shift, gain.reshape(1, D), w_main, wfh, wfl, b_f.reshape(1, H),
      jnp.tile(q_gain, 2).reshape(1, LANES), jnp.tile(k_gain, 2).reshape(1, LANES))
    return outs


def _fox_attn_kernel(q_ref, k_ref, v_ref, fc_ref, fr_ref, o_ref, *, hd, n_sub, hps):
    TQ = q_ref.shape[0]
    SQ = TQ // n_sub
    qi = pl.program_id(2)
    lane = lax.broadcasted_iota(I32, (SQ, LANES), 1)
    row = lax.broadcasted_iota(I32, (SQ, TQ), 0)
    col = lax.broadcasted_iota(I32, (SQ, TQ), 1)
    chains = [(hh, sb) for hh in range(hps) for sb in range(n_sub)]
    nch = len(chains)
    qms, fqs = [], []
    for hh, sb in chains:
        pb, ph = hh // 2, hh % 2
        q = q_ref[sb * SQ:(sb + 1) * SQ, pb * LANES:(pb + 1) * LANES]
        qms.append(jnp.where((lane >= ph * hd) & (lane < (ph + 1) * hd), q, jnp.zeros_like(q)))
        fqs.append(fc_ref[sb * SQ:(sb + 1) * SQ, hh:hh + 1])

    def block_scores(j):
        start = pl.multiple_of(j * TQ, TQ)
        kjs = [k_ref[pl.ds(start, TQ), pb * LANES:(pb + 1) * LANES] for pb in range(hps // 2)]
        return tuple(_dot_nt(qms[c], kjs[hh // 2]) + (fqs[c] - fr_ref[hh:hh + 1, pl.ds(start, TQ)])
                     for c, (hh, sb) in enumerate(chains))

    def update(j, state, scores, masked):
        start = pl.multiple_of(j * TQ, TQ)
        vjs = [v_ref[pl.ds(start, TQ), pb * LANES:(pb + 1) * LANES] for pb in range(hps // 2)]
        new = []
        for c, (hh, sb) in enumerate(chains):
            m, l, acc = state[c]
            s = jnp.where(col <= row + sb * SQ, scores[c], NEG_INF) if masked else scores[c]
            m_new = jnp.maximum(m, jnp.max(s, axis=-1, keepdims=True))
            alpha = jnp.exp(m - m_new)
            p = jnp.exp(s - m_new)
            new.append((m_new, alpha * l + jnp.sum(p, axis=-1, keepdims=True),
                        alpha * acc + _dot(p.astype(BF16), vjs[hh // 2])))
        return tuple(new)

    def step(j, carry):
        state, scores = carry
        nxt = block_scores(j + 1)
        return update(j, state, scores, False), nxt

    init = tuple((jnp.full((SQ, 1), NEG_INF, F32), jnp.zeros((SQ, 1), F32), jnp.zeros((SQ, LANES), F32))
                 for _ in chains)
    state, scores = lax.fori_loop(0, qi, step, (init, block_scores(0)))
    final = update(qi, state, scores, True)
    for pb in range(hps // 2):
        for sb in range(n_sub):
            even = final[(2 * pb) * n_sub + sb]
            odd = final[(2 * pb + 1) * n_sub + sb]
            o_ref[sb * SQ:(sb + 1) * SQ, pb * LANES:(pb + 1) * LANES] = jnp.where(
                lane < hd, even[2] / even[1], odd[2] / odd[1])


def fox_attend_prompt(q, kb, vb, F, *, n_heads, tq=256, hps=2, n_sub=2):
    B, S, D = q.shape
    hd = D // n_heads
    assert 2 * hd == LANES and hps % 2 == 0 and n_heads % hps == 0
    HG = n_heads // hps
    W = hps * hd
    fcol = F.reshape(B, S, HG, hps).transpose(0, 2, 1, 3)
    frow = fcol.transpose(0, 1, 3, 2)
    return pl.pallas_call(
        functools.partial(_fox_attn_kernel, hd=hd, n_sub=n_sub, hps=hps),
        grid=(B, HG, S // tq),
        in_specs=[pl.BlockSpec((None, tq, W), lambda b, h, i: (b, i, h)),
                  pl.BlockSpec((None, S, W), lambda b, h, i: (b, 0, h)),
                  pl.BlockSpec((None, S, W), lambda b, h, i: (b, 0, h)),
                  pl.BlockSpec((None, None, tq, hps), lambda b, h, i: (b, h, i, 0)),
                  pl.BlockSpec((None, None, hps, S), lambda b, h, i: (b, h, 0, 0))],
        out_specs=pl.BlockSpec((None, tq, W), lambda b, h, i: (b, i, h)),
        out_shape=jax.ShapeDtypeStruct((B, S, D), F32),
        compiler_params=_params("arbitrary", "arbitrary", "arbitrary"),
        name="fox_attend_prompt",
    )(q, kb, vb, fcol, frow)


def _fox_decode_kernel(pt_ref, qbd_ref, *refs, n_heads, pp):
    del pt_ref
    kt_refs, vt_refs, cf_refs = refs[:pp], refs[pp:2 * pp], refs[2 * pp:3 * pp]
    kn_ref, vn_ref, cn_ref, o_ref, m_scr, l_scr, acc_scr, carry_scr = refs[3 * pp:]
    H = n_heads
    step = pl.program_id(1)
    n_steps = pl.num_programs(1)
    R, D = qbd_ref.shape
    T = R // H
    P = cf_refs[0].shape[1]

    @pl.when(step == 0)
    def _():
        m_scr[...] = jnp.full(m_scr.shape, NEG_INF, F32)
        l_scr[...] = jnp.zeros(l_scr.shape, F32)
        acc_scr[...] = jnp.zeros(acc_scr.shape, F32)
        carry_scr[...] = jnp.zeros(carry_scr.shape, F32)

    qbd = qbd_ref[...]

    def merge(m_i, l_i, acc_i):
        m = m_scr[...]
        m_new = jnp.maximum(m, m_i)
        a_old = jnp.exp(m - m_new)
        a_new = jnp.exp(m_i - m_new)
        l_scr[...] = a_old * l_scr[...] + a_new * l_i
        acc_scr[...] = a_old * acc_scr[...] + a_new * acc_i
        m_scr[...] = m_new

    def scores(s, fk):
        return s - jnp.concatenate([fk] * T, axis=0)

    carry = carry_scr[...]
    ss = []
    for i in range(pp):
        cf = cf_refs[i][...]
        ss.append(scores(_dot(qbd, kt_refs[i][...].astype(BF16)), cf + carry))
        carry = carry + cf[:, P - 1:P]
    carry_scr[...] = carry
    m_s = jnp.max(ss[0], axis=-1, keepdims=True)
    for s in ss[1:]:
        m_s = jnp.maximum(m_s, jnp.max(s, axis=-1, keepdims=True))
    l_s = jnp.zeros_like(m_s)
    acc_s = jnp.zeros(acc_scr.shape, F32)
    for i, s in enumerate(ss):
        pr = jnp.exp(s - m_s)
        l_s = l_s + jnp.sum(pr, axis=-1, keepdims=True)
        acc_s = acc_s + _dot_nt(pr.astype(BF16), vt_refs[i][...].astype(BF16))
    merge(m_s, l_s, acc_s)

    @pl.when(step == n_steps - 1)
    def _():
        s2 = scores(_dot_nt(qbd, kn_ref[...]), cn_ref[...] + carry_scr[...])
        key = lax.broadcasted_iota(I32, s2.shape, 1)
        t = lax.broadcasted_iota(I32, s2.shape, 0) // H
        s2 = jnp.where(key <= t, s2, NEG_INF)
        m2 = jnp.max(s2, axis=-1, keepdims=True)
        p2 = jnp.exp(s2 - m2)
        merge(m2, jnp.sum(p2, axis=-1, keepdims=True), _dot(p2.astype(BF16), vn_ref[...]))
        out = acc_scr[...] / l_scr[...]
        rh = lax.broadcasted_iota(I32, (R, D), 0) % H
        ch = lax.broadcasted_iota(I32, (R, D), 1) // (D // H)
        out = jnp.where(rh == ch, out, 0.0)
        o_ref[...] = jnp.sum(out.reshape(T, H, D), axis=1)


def fox_attend_sample(q, kb_new, vb_new, lf_new, cache_kt, cache_vt, cache_cf, page_table, j, *, n_heads):
    Bd, T, D = q.shape
    H = n_heads
    hd = D // H
    n_pages = page_table.shape[1]
    n_fox, n_pool, _, P = cache_cf.shape
    R = T * H
    pp = next(c for c in (16, 8, 4, 2, 1) if n_pages % c == 0)
    assert T <= P
    q4 = q.reshape(Bd, T, H, hd)
    eye = jnp.eye(H, dtype=q.dtype)
    qbd = (q4[:, :, :, None, :] * eye[None, None, :, :, None]).reshape(Bd, R, D)
    pad = ((0, 0), (0, P - T), (0, 0))
    kn = jnp.pad(kb_new, pad)
    vn = jnp.pad(vb_new, pad)
    cn = seg_cumsum(jnp.pad(lf_new.transpose(0, 2, 1), ((0, 0), (0, 0), (0, P - T))), P)
    pt = page_table.reshape(-1).astype(I32)

    def page_spec(shape, i):
        return pl.BlockSpec((None, None) + shape, lambda b, s, pt: (j, pt[b * n_pages + s * pp + i], 0, 0))

    per_seq = lambda shape: pl.BlockSpec((None,) + shape, lambda b, s, pt: (b, 0, 0))
    grid_spec = pltpu.PrefetchScalarGridSpec(
        num_scalar_prefetch=1,
        grid=(Bd, n_pages // pp),
        in_specs=([per_seq((R, D))]
                  + [page_spec((D, P), i) for i in range(pp)]
                  + [page_spec((D, P), i) for i in range(pp)]
                  + [page_spec((H, P), i) for i in range(pp)]
                  + [per_seq((P, D)), per_seq((P, D)), per_seq((H, P))]),
        out_specs=per_seq((T, D)),
        scratch_shapes=[pltpu.VMEM((R, 1), F32), pltpu.VMEM((R, 1), F32),
                        pltpu.VMEM((R, D), F32), pltpu.VMEM((H, 1), F32)],
    )
    return pl.pallas_call(
        functools.partial(_fox_decode_kernel, n_heads=H, pp=pp),
        grid_spec=grid_spec,
        out_shape=jax.ShapeDtypeStruct((Bd, T, D), F32),
        compiler_params=_params("arbitrary", "arbitrary"),
        name="fox_attend_sample",
    )(pt, qbd, *([cache_kt] * pp), *([cache_vt] * pp), *([cache_cf] * pp), kn, vn, cn)


def prepare_cache(cache_k, cache_v, cache_logf):
    n_fox, n_pool, P, H, hd = cache_k.shape
    kt = cache_k.transpose(0, 1, 3, 4, 2).reshape(n_fox, n_pool, H * hd, P)
    vt = cache_v.transpose(0, 1, 3, 4, 2).reshape(n_fox, n_pool, H * hd, P)
    rows = n_fox * n_pool * H
    group = 2048 if rows % 2048 == 0 else H
    cf = seg_cumsum(cache_logf.transpose(0, 1, 3, 2).reshape(rows // group, group, P), P)
    return kt, vt, cf.reshape(n_fox, n_pool, H, P)


def _outproj_kernel(a_ref, gate_ref, w_ref, x_ref, g_ref, hn_ref, o_ref, *, head_norm):
    a = a_ref[...]
    D = a.shape[1]
    if head_norm:
        blocks = []
        for c in range(D // LANES):
            sl = slice(c * LANES, (c + 1) * LANES)
            blk = a[:, sl]
            ms = jnp.mean(blk * blk, axis=-1, keepdims=True)
            blocks.append(blk * lax.rsqrt(ms + RMS_EPS) * hn_ref[:, sl])
        a = jnp.concatenate(blocks, axis=-1)
    y = (a * gate_ref[...]).astype(BF16)
    o_ref[...] = x_ref[...] + g_ref[...] * _dot(y, w_ref[...])


def out_project(a, gate, w_out, x, g, head_gain, *, tm, tiles_per_group, head_norm):
    N, D = x.shape
    R = g.shape[1]
    row = lambda i: (i, 0)
    const = lambda i: (0, 0)
    return pl.pallas_call(
        functools.partial(_outproj_kernel, head_norm=head_norm),
        grid=(N // tm,),
        in_specs=[pl.BlockSpec((tm, D), row), pl.BlockSpec((tm, D), row),
                  pl.BlockSpec((D, D), const), pl.BlockSpec((tm, D), row),
                  _mod_spec(R, D, tiles_per_group), pl.BlockSpec((1, D), const)],
        out_specs=pl.BlockSpec((tm, D), row),
        out_shape=jax.ShapeDtypeStruct((N, D), F32),
        compiler_params=_params("arbitrary"),
        name="out_project",
    )(a, gate, w_out.astype(BF16), x, g, head_gain.reshape(1, D))


def _ml_proj_kernel(x_ref, sc_ref, sh_ref, gain_ref, w_ref, wgh_ref, wgl_ref, bi_ref, bf_ref,
                    q_ref, k_ref, v_ref, so_ref, il_ref, fl_ref, *, n_heads, qk_width):
    D = x_ref.shape[1]
    QW = qk_width
    h = _modnorm(x_ref[...], gain_ref[...], sc_ref[...], sh_ref[...])
    hh, hl = _split2(h)
    k_scale = (QW // n_heads) ** -0.5
    q_ref[...] = _dot(hh, w_ref[:, 0:QW]).astype(BF16)
    k_ref[...] = (_dot(hh, w_ref[:, QW:2 * QW]) * k_scale).astype(BF16)
    v_ref[...] = _dot(hh, w_ref[:, 2 * QW:2 * QW + D])
    so_ref[...] = jax.nn.sigmoid(_dot(hh, w_ref[:, 2 * QW + D:2 * QW + 2 * D]))
    gts = _dot(hh, wgh_ref[...]) + _dot(hh, wgl_ref[...]) + _dot(hl, wgh_ref[...])
    il_ref[...] = gts[:, 0:n_heads] + bi_ref[...]
    fl_ref[...] = _log_sigmoid(gts[:, n_heads:2 * n_heads] + bf_ref[...])


def ml_project(x, scale, shift, gain, w_in, b_i, b_f, *, tm, tiles_per_group, v_width):
    N, D = x.shape
    H = b_i.shape[0]
    QW = (w_in.shape[1] - 2 * v_width - 2 * H) // 2
    R = scale.shape[1]
    g0 = 2 * QW + v_width
    w_main = jnp.concatenate([w_in[:, :g0], w_in[:, g0 + 2 * H:]], axis=1).astype(BF16)
    w_g = jnp.pad(w_in[:, g0:g0 + 2 * H], ((0, 0), (0, LANES - 2 * H)))
    wgh = w_g.astype(BF16)
    wgl = (w_g - wgh.astype(F32)).astype(BF16)
    const = lambda i: (0, 0)
    row = lambda i: (i, 0)
    WM = w_main.shape[1]
    return pl.pallas_call(
        functools.partial(_ml_proj_kernel, n_heads=H, qk_width=QW),
        grid=(N // tm,),
        in_specs=[pl.BlockSpec((tm, D), row),
                  _mod_spec(R, D, tiles_per_group), _mod_spec(R, D, tiles_per_group),
                  pl.BlockSpec((1, D), const),
                  pl.BlockSpec((D, WM), const),
                  pl.BlockSpec((D, LANES), const), pl.BlockSpec((D, LANES), const),
                  pl.BlockSpec((1, H), const), pl.BlockSpec((1, H), const)],
        out_specs=[pl.BlockSpec((tm, QW), row), pl.BlockSpec((tm, QW), row),
                   pl.BlockSpec((tm, v_width), row), pl.BlockSpec((tm, v_width), row),
                   pl.BlockSpec((tm, H), row), pl.BlockSpec((tm, H), row)],
        out_shape=[jax.ShapeDtypeStruct((N, QW), BF16), jax.ShapeDtypeStruct((N, QW), BF16),
                   jax.ShapeDtypeStruct((N, v_width), F32), jax.ShapeDtypeStruct((N, v_width), F32),
                   jax.ShapeDtypeStruct((N, H), F32), jax.ShapeDtypeStruct((N, H), F32)],
        compiler_params=_params("arbitrary"),
        name="ml_project",
    )(x, scale, shift, gain.reshape(1, D), w_main, wgh, wgl, b_i.reshape(1, H), b_f.reshape(1, H))


def _mlstm_kernel(q_ref, k_ref, v_ref, ic_ref, ir_ref, bc_ref, br_ref, c0_ref, n0_ref, m0_ref,
                  h_ref, c_ref, n_ref, m_ref, *, n_heads, dk):
    nbb, L = q_ref.shape[0], q_ref.shape[1]
    ci = pl.program_id(1)

    @pl.when(ci == 0)
    def _():
        c_ref[...] = c0_ref[...]
        n_ref[...] = n0_ref[...]
        m_ref[...] = m0_ref[...]

    tri = lax.broadcasted_iota(I32, (L, L), 1) <= lax.broadcasted_iota(I32, (L, L), 0)
    lane = lax.broadcasted_iota(I32, (L, LANES), 1)
    lane1 = lax.broadcasted_iota(I32, (1, LANES), 1)
    rsub = lax.broadcasted_iota(I32, (LANES, LANES), 0)
    dv = LANES
    m_in = [m_ref[bb] for bb in range(nbb)]
    units = [(bb, h) for bb in range(nbb) for h in range(n_heads)]
    pairs = [(bb, pr) for bb in range(nbb) for pr in range(n_heads // 2)]
    qp = {(bb, pr): q_ref[bb, :, pr * LANES:(pr + 1) * LANES] for bb, pr in pairs}
    kp = {(bb, pr): k_ref[bb, :, pr * LANES:(pr + 1) * LANES] for bb, pr in pairs}
    c_in = {u: c_ref[u[0], u[1]] for u in pairs}
    n_in = {u: n_ref[u[0], u[1]] for u in pairs}
    sel = [(lane >= hh * dk) & (lane < (hh + 1) * dk) for hh in range(2)]
    qm = {(bb, h): jnp.where(sel[h % 2], qp[(bb, h // 2)], jnp.zeros_like(qp[(bb, h // 2)])) for bb, h in units}
    qk = {(bb, h): _dot_nt(qm[(bb, h)], kp[(bb, h // 2)]) for bb, h in units}
    qc = {(bb, h): _dot(qm[(bb, h)], c_in[(bb, h // 2)].astype(BF16)) for bb, h in units}
    st = {}
    for bb, h in units:
        b_l = bc_ref[bb][:, h:h + 1]
        b_s = br_ref[bb][h:h + 1, :]
        ig_s = ir_ref[bb][h:h + 1, :]
        ig_l = ic_ref[bb][:, h:h + 1]
        m = m_in[bb][:, h:h + 1]
        dm = jnp.where(tri, b_l - b_s + ig_s, NEG_INF)
        inter = b_l + m
        m_t = jnp.maximum(inter, jnp.max(dm, axis=-1, keepdims=True))
        w_inter = jnp.exp(inter - m_t)
        a = jnp.exp(dm - m_t) * qk[(bb, h)]
        b_last = b_l[L - 1:L, :]
        m_new = jnp.maximum(b_last + m, jnp.max(b_last - b_s + ig_s, axis=-1, keepdims=True))
        wk = jnp.exp(b_last - b_l + ig_l - m_new)
        decay = jnp.exp(b_last + m - m_new)
        vh = v_ref[bb, :, h * dv:(h + 1) * dv]
        st[(bb, h)] = (a, m_t, w_inter, m_new, wk, decay, vh)
    av = {u: _dot(st[u][0].astype(BF16), st[u][6].astype(BF16)) for u in units}
    upd = {(bb, h): _dot_tn(kp[(bb, h // 2)], (st[(bb, h)][4] * st[(bb, h)][6]).astype(BF16)) for bb, h in units}
    for bb, pr in pairs:
        c_pair, n_pair = c_in[(bb, pr)], n_in[(bb, pr)]
        qpf = qp[(bb, pr)].astype(F32)
        kpf = kp[(bb, pr)].astype(F32)
        new_c, new_n = [], []
        for hh in range(2):
            h = 2 * pr + hh
            a, m_t, w_inter, m_new, wk, decay, vh = st[(bb, h)]
            num = w_inter * qc[(bb, h)] + av[(bb, h)]
            qn = jnp.sum(jnp.where(sel[hh], qpf, 0.0) * n_pair, axis=-1, keepdims=True)
            den = w_inter * qn + jnp.sum(a, axis=-1, keepdims=True)
            h_ref[bb, :, h * dv:(h + 1) * dv] = num / jnp.maximum(jnp.abs(den), jnp.exp(-m_t))
            new_c.append(decay * c_pair + upd[(bb, h)])
            new_n.append(decay * n_pair + jnp.sum(wk * kpf, axis=0, keepdims=True))
            m_ref[bb, :, h:h + 1] = m_new
        c_ref[bb, pr] = jnp.where(rsub < dk, new_c[0], new_c[1])
        n_ref[bb, pr] = jnp.where(lane1 < dk, new_n[0], new_n[1])


def mlstm_chunked(q, k, v, i_log, f_log, C0, n0, m0, chunk):
    B, T, _ = q.shape
    H = i_log.shape[-1]
    dk = C0.shape[2]
    dv = C0.shape[3]
    assert 2 * dk == LANES and dv == LANES
    nc = T // chunk
    L = chunk
    Tp = -(-T // LANES) * LANES
    fT = jnp.pad(f_log.transpose(0, 2, 1), ((0, 0), (0, 0), (0, Tp - T)))
    bT = seg_cumsum(fT, chunk)[:, :, :T]
    br = bT.reshape(B, H, nc, L).transpose(0, 2, 1, 3)
    bc = br.transpose(0, 1, 3, 2)
    ic = i_log.reshape(B, nc, L, H)
    ir = ic.transpose(0, 1, 3, 2)
    HP = H // 2
    c0 = C0.reshape(B, HP, 2 * dk, dv)
    n0p = n0.reshape(B, HP, 1, 2 * dk)
    m0p = m0.reshape(B, 1, H)
    QW = H * dk
    VW = H * dv
    nbb = next(c for c in (4, 2, 1) if B % c == 0)
    col_spec = pl.BlockSpec((nbb, None, L, H), lambda b, c: (b, c, 0, 0))
    row_spec = pl.BlockSpec((nbb, None, H, L), lambda b, c: (b, c, 0, 0))
    st = lambda b, c: (b, 0, 0, 0)
    h, Cn, nn, mn = pl.pallas_call(
        functools.partial(_mlstm_kernel, n_heads=H, dk=dk),
        grid=(B // nbb, nc),
        in_specs=[pl.BlockSpec((nbb, L, QW), lambda b, c: (b, c, 0)),
                  pl.BlockSpec((nbb, L, QW), lambda b, c: (b, c, 0)),
                  pl.BlockSpec((nbb, L, VW), lambda b, c: (b, c, 0)),
                  col_spec, row_spec, col_spec, row_spec,
                  pl.BlockSpec((nbb, HP, 2 * dk, dv), st),
                  pl.BlockSpec((nbb, HP, 1, 2 * dk), st),
                  pl.BlockSpec((nbb, 1, H), lambda b, c: (b, 0, 0))],
        out_specs=[pl.BlockSpec((nbb, L, VW), lambda b, c: (b, c, 0)),
                   pl.BlockSpec((nbb, HP, 2 * dk, dv), st),
                   pl.BlockSpec((nbb, HP, 1, 2 * dk), st),
                   pl.BlockSpec((nbb, 1, H), lambda b, c: (b, 0, 0))],
        out_shape=[jax.ShapeDtypeStruct((B, T, VW), F32),
                   jax.ShapeDtypeStruct((B, HP, 2 * dk, dv), F32),
                   jax.ShapeDtypeStruct((B, HP, 1, 2 * dk), F32),
                   jax.ShapeDtypeStruct((B, 1, H), F32)],
        compiler_params=_params("arbitrary", "arbitrary"),
        name="mlstm_chunked",
    )(q, k, v, ic, ir, bc, br, c0, n0p, m0p)
    return h, Cn.reshape(B, H, dk, dv), nn.reshape(B, H, dk), mn.reshape(B, H)


def _candidate_tiles(n):
    runs = []
    for a in range(n // 2):
        cnt = n // (a + 1)
        for b0 in range(0, cnt, 8):
            runs.append((a, b0, 0, min(8, cnt - b0)))
    runs.append((-1, 0, 0, n // 2))
    tiles, pos = {}, 0
    for a, b0, src, cnt in runs:
        while cnt:
            take = min(cnt, 8 - pos % 8)
            tiles.setdefault(pos // 8, []).append((a, b0, (pos - src) % 8, pos % 8, pos % 8 + take))
            pos, src, cnt = pos + take, src + take, cnt - take
    return [tiles[q] for q in sorted(tiles)]


def _extract_topk(ss, payloads, n):
    ss = list(ss)
    R = ss[0].shape[0]
    ridx = lax.broadcasted_iota(I32, ss[0].shape, 0).astype(F32)
    vals = [[] for _ in ss]
    pays = [[] for _ in ss]
    for it in range(n):
        for c in range(len(ss)):
            s = ss[c]
            m = jnp.max(s, axis=0, keepdims=True)
            am = jnp.min(jnp.where(s == m, ridx, float(R)), axis=0, keepdims=True)
            hit = ridx == am
            if payloads[c] is None:
                pays[c].append(am)
            else:
                pays[c].append(jnp.max(jnp.where(hit, payloads[c], -1.0), axis=0, keepdims=True))
            vals[c].append(m)
            if it + 1 < n:
                ss[c] = jnp.where(hit, NEG_INF, s)
    return vals, pays


def _peer_topk_kernel(x_ref, sc_ref, sh_ref, gain_ref, wq_ref, k1_ref, k2_ref,
                      hb_ref, e_ref, g_ref, q_scr, v1_scr, i1_scr, v2_scr, i2_scr, *, n_heads, n_keys):
    TM = x_ref.shape[0]
    K = PEER_TOPK
    half = k1_ref.shape[2]
    hb = _modnorm(x_ref[...], gain_ref[...], sc_ref[...], sh_ref[...]).astype(BF16)
    hb_ref[...] = hb
    q_scr[...] = _dot(hb, wq_ref[...]).astype(BF16)
    sub8 = lax.broadcasted_iota(I32, (8, LANES), 0)

    def head_body(hd, _):
        c0 = pl.multiple_of(hd * 2 * half, 2 * half)
        c1 = pl.multiple_of(hd * 2 * half + half, half)
        s1 = _dot_nt(k1_ref[hd], q_scr[:, pl.ds(c0, half)])
        s2 = _dot_nt(k2_ref[hd], q_scr[:, pl.ds(c1, half)])
        NB = TM // LANES
        blocks = [slice(blk * LANES, (blk + 1) * LANES) for blk in range(NB)]
        vals, idxs = _extract_topk([s[:, ls] for ls in blocks for s in (s1, s2)], [None] * (2 * NB), K)
        for blk in range(NB):
            for side, (v_scr, i_scr) in enumerate(((v1_scr, i1_scr), (v2_scr, i2_scr))):
                for r in range(K):
                    v_scr[blk, r:r + 1, :] = vals[2 * blk + side][r]
                    i_scr[blk, r:r + 1, :] = idxs[2 * blk + side][r]
        cands, pays = [], []
        for blk in range(NB):
            pieces = {}

            def piece(a, b0, shift, blk=blk, pieces=pieces):
                key = (a, b0, shift)
                if key not in pieces:
                    if a >= 0:
                        cv = v1_scr[blk, a:a + 1, :] + v2_scr[blk, b0:b0 + 8, :]
                        pv = i1_scr[blk, a:a + 1, :] * n_keys + i2_scr[blk, b0:b0 + 8, :]
                    else:
                        cv = v1_scr[blk, K // 2:K, :] + v2_scr[blk, 0:1, :]
                        pv = i1_scr[blk, K // 2:K, :] * n_keys + i2_scr[blk, 0:1, :]
                    if shift:
                        cv, pv = pltpu.roll(cv, shift, 0), pltpu.roll(pv, shift, 0)
                    pieces[key] = (cv, pv)
                return pieces[key]

            cand, pay = [], []
            for segs in _candidate_tiles(K):
                cv = jnp.full((8, LANES), NEG_INF, F32)
                pv = jnp.zeros((8, LANES), F32)
                for a, b0, shift, lo, hi in segs:
                    pc, pp_ = piece(a, b0, shift)
                    if (lo, hi) == (0, 8):
                        cv, pv = pc, pp_
                    else:
                        inside = (sub8 >= lo) & (sub8 < hi)
                        cv, pv = jnp.where(inside, pc, cv), jnp.where(inside, pp_, pv)
                cand.append(cv)
                pay.append(pv)
            cands.append(jnp.concatenate(cand, axis=0))
            pays.append(jnp.concatenate(pay, axis=0))
        tops, topes = _extract_topk(cands, pays, K)
        out_rows = pl.ds(pl.multiple_of(hd * K, K), K)
        for blk in range(NB):
            top_s, top_e = tops[blk], topes[blk]
            ex = [jnp.exp(t - top_s[0]) for t in top_s]
            den = ex[0]
            for t in ex[1:]:
                den = den + t
            inv = 1.0 / den
            for r in range(K):
                i1_scr[blk, r:r + 1, :] = top_e[r]
                v1_scr[blk, r:r + 1, :] = ex[r] * inv
            e_ref[out_rows, blocks[blk]] = i1_scr[blk].astype(I32)
            g_ref[out_rows, blocks[blk]] = v1_scr[blk]
        return 0

    lax.fori_loop(0, n_heads, head_body, 0)


def peer_retrieve(x, scale, shift, gain, w_q, sub_k1, sub_k2, *, tm, tiles_per_group):
    N, D = x.shape
    HP, NK, half = sub_k1.shape
    assert NK == LANES and half == LANES and PEER_TOPK % 8 == 0
    R = scale.shape[1]
    QW = w_q.shape[1]
    J = HP * PEER_TOPK
    const = lambda i: (0, 0)
    const3 = lambda i: (0, 0, 0)
    kspec = pl.BlockSpec((HP, NK, half), const3)
    return pl.pallas_call(
        functools.partial(_peer_topk_kernel, n_heads=HP, n_keys=NK),
        grid=(N // tm,),
        in_specs=[pl.BlockSpec((tm, D), lambda i: (i, 0)),
                  _mod_spec(R, D, tiles_per_group), _mod_spec(R, D, tiles_per_group),
                  pl.BlockSpec((1, D), const),
                  pl.BlockSpec((D, QW), const),
                  kspec, kspec],
        out_specs=[pl.BlockSpec((tm, D), lambda i: (i, 0)),
                   pl.BlockSpec((J, tm), lambda i: (0, i)),
                   pl.BlockSpec((J, tm), lambda i: (0, i))],
        out_shape=[jax.ShapeDtypeStruct((N, D), BF16),
                   jax.ShapeDtypeStruct((J, N), I32),
                   jax.ShapeDtypeStruct((J, N), F32)],
        scratch_shapes=[pltpu.VMEM((tm, QW), BF16),
                        ] + [pltpu.VMEM((tm // LANES, PEER_TOPK, LANES), F32)] * 4,
        compiler_params=_params("arbitrary"),
        name="peer_retrieve",
    )(x, scale, shift, gain.reshape(1, D), w_q.astype(BF16), sub_k1.astype(BF16), sub_k2.astype(BF16))


def _peer_mix_kernel(hb_ref, i1n_ref, i2n_ref, gn_ref, i1c_ref, i2c_ref, ut_ref, v_ref, x_ref, g2_ref, o_ref,
                     a_scr, w_scr, hbuf_scr, y_scr, *, n_keys, tg):
    TM = hb_ref.shape[0]
    TE = ut_ref.shape[1]
    J = i1n_ref.shape[1]
    NE = pl.num_programs(1) // 2
    NT = pl.num_programs(0) - 1
    t = pl.program_id(0)
    j = pl.program_id(1)
    NP2 = n_keys // 2
    W2 = 2 * n_keys
    per_step = TM // NE
    rows = lax.broadcasted_iota(I32, (NP2, J), 0)
    cols = lax.broadcasted_iota(I32, (W2, J), 0)

    n_groups = per_step // tg

    def group_grids(g):
        t0 = pl.multiple_of(j * per_step + g * tg, tg)
        grids = []
        for u in range(tg):
            i1 = i1c_ref[pl.ds(t0 + u, 1), :]
            i2 = i2c_ref[pl.ds(t0 + u, 1), :]
            w = w_scr[pl.ds(t0 + u, 1), :]
            p1 = jnp.where(rows == (i1 >> 1), w, 0.0).astype(BF16)
            p2 = jnp.where(cols == ((i1 & 1) * n_keys + i2), 1.0, 0.0).astype(BF16)
            grids.append(_dot_nt(p1, p2).astype(BF16))
        return jnp.stack(grids, axis=0)

    def group_store(g, stacked):
        t0 = pl.multiple_of(j * per_step + g * tg, tg)
        hbuf_scr[:, pl.ds(t0, tg), :] = pltpu.einshape("tpc->ptc", stacked)

    def build_slice():
        for g in range(n_groups):
            group_store(g, group_grids(g))

    def activations(with_build=False):
        CW = MXU_DIM
        n_chunks = TE // CW
        hb = hb_ref[...]
        i1 = i1n_ref[...]
        i2 = i2n_ref[...]
        acc = a_scr[...]
        events = {}
        if with_build:
            for e in range(2 * n_groups):
                events.setdefault(min(e * n_chunks // (2 * n_groups), n_chunks - 1), []).append(e)
        pending = {}

        def pick(acc, k, act):
            for c in range(CW // n_keys):
                r = j * (TE // n_keys) + k * (CW // n_keys) + c
                acc = jnp.where(i1 == r, jnp.take_along_axis(act[:, c * n_keys:(c + 1) * n_keys], i2, axis=1), acc)
            return acc

        prev = None
        for k in range(n_chunks):
            act = _dot(hb, ut_ref[:, k * CW:(k + 1) * CW])
            for e in events.get(k, []):
                if e % 2 == 0:
                    pending[e // 2] = group_grids(e // 2)
                else:
                    group_store(e // 2, pending.pop(e // 2))
            if prev is not None:
                acc = pick(acc, k - 1, prev)
            prev = act
        a_scr[...] = pick(acc, n_chunks - 1, prev)

    @pl.when((j == 0) & (t < NT))
    def _():
        a_scr[...] = jnp.zeros(a_scr.shape, F32)

    @pl.when((j < NE) & (t > 0) & (t < NT))
    def _():
        activations(with_build=True)

    @pl.when((j < NE) & (t == 0))
    def _():
        activations()

    @pl.when((j < NE) & (t == NT))
    def _():
        build_slice()

    @pl.when((j == NE - 1) & (t < NT))
    def _():
        a = a_scr[...]
        w_scr[...] = gn_ref[...] * (0.5 * a * (1.0 + lax.erf(a * math.sqrt(0.5))))

    @pl.when((j == NE) & (t > 0))
    def _():
        y_scr[...] = jnp.zeros(y_scr.shape, F32)

    @pl.when((j >= NE) & (t > 0))
    def _():
        jj = j - NE
        y = y_scr[...]
        for c in range(TE // W2):
            y = y + _dot(hbuf_scr[jj * (TE // W2) + c], v_ref[c * W2:(c + 1) * W2, :])
        y_scr[...] = y

    @pl.when((j == 2 * NE - 1) & (t > 0))
    def _():
        o_ref[...] = x_ref[...] + g2_ref[...] * y_scr[...]


def peer_mix(hb, i1, i2, g, ut_tab, v_tab, x, g2, *, tm, te, tiles_per_group, n_keys):
    N, D = x.shape
    E = v_tab.shape[0]
    J = i1.shape[1]
    NE = E // te
    NT = N // tm
    R = g2.shape[1]
    tg = 16
    assert te % (2 * n_keys) == 0 and n_keys == LANES and tm % (NE * tg) == 0
    nxt = lambda t, j: (jnp.minimum(t, NT - 1), 0)
    cur = lambda t, j: (jnp.maximum(t - 1, 0), 0)
    return pl.pallas_call(
        functools.partial(_peer_mix_kernel, n_keys=n_keys, tg=tg),
        grid=(NT + 1, 2 * NE),
        in_specs=[pl.BlockSpec((tm, D), nxt),
                  pl.BlockSpec((tm, J), nxt), pl.BlockSpec((tm, J), nxt), pl.BlockSpec((tm, J), nxt),
                  pl.BlockSpec((tm, J), cur), pl.BlockSpec((tm, J), cur),
                  pl.BlockSpec((D, te), lambda t, j: (0, jnp.where(t < NT, jnp.minimum(j, NE - 1), NE - 1))),
                  pl.BlockSpec((te, D), lambda t, j: (jnp.where(t > 0, jnp.maximum(j - NE, 0), 0), 0)),
                  pl.BlockSpec((tm, D), cur),
                  pl.BlockSpec((None, R, D), lambda t, j: (jnp.maximum(t - 1, 0) // tiles_per_group, 0, 0))],
        out_specs=pl.BlockSpec((tm, D), cur),
        out_shape=jax.ShapeDtypeStruct((N, D), F32),
        scratch_shapes=[pltpu.VMEM((tm, J), F32), pltpu.VMEM((tm, J), F32),
                        pltpu.VMEM((n_keys // 2, tm, 2 * n_keys), BF16),
                        pltpu.VMEM((tm, D), F32)],
        compiler_params=_params("arbitrary", "arbitrary"),
        name="peer_mix",
    )(hb, i1, i2, g, i1, i2, ut_tab, v_tab, x, g2)


def _trunk(x3, mods, past, params, tables, *, tm):
    (norm_mix, norm_ffn, fox_w_in, fox_b_f, fox_q_norm, fox_k_norm, fox_w_out,
     ml_w_in, ml_b_i, ml_b_f, ml_h_norm, ml_w_out, peer_w_q, peer_sub_k1, peer_sub_k2) = params
    ut_bf, v_bf = tables
    B, T, D = x3.shape
    N = B * T
    depth = norm_mix.shape[0]
    n_fox_heads = fox_b_f.shape[1]
    n_keys = peer_sub_k1.shape[2]
    x = x3.reshape(N, D)
    if T % tm == 0:
        tpg, rows = T // tm, 1
        expand = lambda a: a.reshape(B, 1, D)
    else:
        assert N == tm
        tpg, rows = 1, N
        expand = lambda a: jnp.repeat(a, T, axis=0).reshape(1, N, D)
    big_tm = 2 * tm if tpg % 2 == 0 else tm
    big_tpg = tpg * tm // big_tm
    new_k, new_v, new_lf, new_C, new_n, new_m = [], [], [], [], [], []
    for layer in range(depth):
        sh1, sc1, g1, sh2, sc2, g2 = [expand(a) for a in jnp.split(mods[layer], 6, axis=-1)]
        j = layer // 2
        if layer % 2 == 0:
            q, k, v, kb, vb, sg, lf = fox_project(
                x, sc1, sh1, norm_mix[layer], fox_w_in[j], fox_b_f[j], fox_q_norm[j], fox_k_norm[j],
                tm=big_tm, tiles_per_group=big_tpg)
            if past is None:
                F = seg_cumsum(lf.reshape(B, T, n_fox_heads).transpose(0, 2, 1), T).transpose(0, 2, 1)
                o = fox_attend_prompt(q.reshape(B, T, D), kb.reshape(B, T, D), vb.reshape(B, T, D), F,
                                      n_heads=n_fox_heads)
            else:
                cache_kt, cache_vt, cache_cf, _, _, _, page_table = past
                o = fox_attend_sample(q.reshape(B, T, D), kb.reshape(B, T, D), vb.reshape(B, T, D),
                                      lf.reshape(B, T, n_fox_heads), cache_kt, cache_vt, cache_cf,
                                      page_table, j, n_heads=n_fox_heads)
            x = out_project(o.reshape(N, D), sg, fox_w_out[j], x, g1, jnp.ones((D,), F32),
                            tm=big_tm, tiles_per_group=big_tpg, head_norm=False)
            new_k.append(k.reshape(B, T, n_fox_heads, -1))
            new_v.append(v.reshape(B, T, n_fox_heads, -1))
            new_lf.append(lf.reshape(B, T, n_fox_heads))
        else:
            n_ml_heads = ml_b_i.shape[1]
            v_width = ml_w_out.shape[1]
            q, k, v, so, il, fl = ml_project(x, sc1, sh1, norm_mix[layer], ml_w_in[j], ml_b_i[j], ml_b_f[j],
                                             tm=big_tm, tiles_per_group=big_tpg, v_width=v_width)
            QW = q.shape[1]
            dk = QW // n_ml_heads
            dv = v_width // n_ml_heads
            q3, k3, v3 = q.reshape(B, T, QW), k.reshape(B, T, QW), v.reshape(B, T, v_width)
            il3, fl3 = il.reshape(B, T, n_ml_heads), fl.reshape(B, T, n_ml_heads)
            if past is None:
                C0 = jnp.zeros((B, n_ml_heads, dk, dv), F32)
                n0 = jnp.zeros((B, n_ml_heads, dk), F32)
                m0 = jnp.zeros((B, n_ml_heads), F32)
                hh, C, n_, m_ = mlstm_chunked(q3, k3, v3, il3, fl3, C0, n0, m0, 64)
            else:
                _, _, _, state_C, state_n, state_m, _ = past
                Tp = 16
                pad3 = ((0, 0), (0, Tp - T), (0, 0))
                hh, C, n_, m_ = mlstm_chunked(
                    jnp.pad(q3, pad3), jnp.pad(k3, pad3), jnp.pad(v3, pad3),
                    jnp.pad(il3, pad3, constant_values=-1e30), jnp.pad(fl3, pad3),
                    state_C[j], state_n[j], state_m[j], Tp)
                hh = hh[:, :T]
            x = out_project(hh.reshape(N, v_width), so, ml_w_out[j], x, g1, ml_h_norm[j].reshape(-1),
                            tm=big_tm, tiles_per_group=big_tpg, head_norm=True)
            new_C.append(C)
            new_n.append(n_)
            new_m.append(m_)
        hb, e, g = peer_retrieve(x, sc2, sh2, norm_ffn[layer], peer_w_q[layer], peer_sub_k1[layer],
                                 peer_sub_k2[layer], tm=big_tm, tiles_per_group=big_tpg)
        eT = e.T
        x = peer_mix(hb, eT // n_keys, eT % n_keys, g.T, ut_bf[layer], v_bf[layer], x, g2,
                     tm=big_tm, te=2048, tiles_per_group=big_tpg, n_keys=n_keys)
    return (x.reshape(B, T, D), jnp.stack(new_k), jnp.stack(new_v), jnp.stack(new_lf),
            jnp.stack(new_C), jnp.stack(new_n), jnp.stack(new_m))


def kernel(x_prompt, x_sample, cache_k, cache_v, cache_logf, state_C, state_n, state_m, page_table,
           c_prompt, c_sample, ada_w, ada_b, norm_mix, norm_ffn, fox_w_in, fox_b_f, fox_q_norm, fox_k_norm,
           fox_w_out, ml_w_in, ml_b_i, ml_b_f, ml_h_norm, ml_w_out, peer_w_q, peer_sub_k1, peer_sub_k2,
           peer_u, peer_v):
    params = (norm_mix, norm_ffn, fox_w_in, fox_b_f, fox_q_norm, fox_k_norm, fox_w_out,
              ml_w_in, ml_b_i, ml_b_f, ml_h_norm, ml_w_out, peer_w_q, peer_sub_k1, peer_sub_k2)
    tables = (peer_u.astype(BF16).transpose(0, 2, 1), peer_v.astype(BF16))
    Bp = c_prompt.shape[0]
    mods = adaln(jnp.concatenate([c_prompt, c_sample], axis=0), ada_w, ada_b)
    y_p, k_p, v_p, lf_p, C_p, n_p, m_p = _trunk(x_prompt, mods[:, :Bp], None, params, tables, tm=256)
    past = prepare_cache(cache_k, cache_v, cache_logf) + (state_C, state_n, state_m, page_table)
    y_s, k_s, v_s, lf_s, C_s, n_s, m_s = _trunk(x_sample, mods[:, Bp:], past, params, tables,
                                               tm=x_sample.shape[0] * x_sample.shape[1])
    return (y_p, y_s, k_p, v_p, lf_p, C_p, n_p, m_p, k_s, v_s, lf_s, C_s, n_s, m_s)
```

```python
import functools
import math

import jax
import jax.numpy as jnp
from jax import lax
from jax.experimental import pallas as pl
from jax.experimental.pallas import tpu as pltpu

F32 = jnp.float32
BF16 = jnp.bfloat16
I32 = jnp.int32

RMS_EPS = 1e-6
LANES = 128
MXU_DIM = 256
VMEM_LIMIT = 52 * 1024 * 1024
PEER_TOPK = 16
PAGE_SIZE = 128
NEG_INF = float("-inf")


def _dot(a, b):
    return jnp.dot(a, b, preferred_element_type=F32)


def _dot_nt(a, b):
    return lax.dot_general(a, b, (((1,), (1,)), ((), ())), preferred_element_type=F32)


def _dot_tn(a, b):
    return lax.dot_general(a, b, (((0,), (0,)), ((), ())), preferred_element_type=F32)


def _split2(a):
    hi = a.astype(BF16)
    lo = (a - hi.astype(F32)).astype(BF16)
    return hi, lo


def _dot3(a, b):
    ah, al = _split2(a)
    bh, bl = _split2(b)
    return _dot(ah, bh) + _dot(ah, bl) + _dot(al, bh)


def _modnorm(x, gain, scale, shift):
    ms = jnp.mean(x * x, axis=-1, keepdims=True)
    return x * lax.rsqrt(ms + RMS_EPS) * gain * (1.0 + scale) + shift


def _log_sigmoid(z):
    return jnp.minimum(z, 0.0) - jnp.log1p(jnp.exp(-jnp.abs(z)))


def _params(*sem):
    return pltpu.CompilerParams(dimension_semantics=sem, vmem_limit_bytes=VMEM_LIMIT)


def _mod_spec(rows, d, tiles_per_group):
    return pl.BlockSpec((None, rows, d), lambda i: (i // tiles_per_group, 0, 0))


def _adaln_kernel(c_ref, w_ref, b_ref, o_ref):
    c = c_ref[...]
    o_ref[...] = _dot3(c * jax.nn.sigmoid(c), w_ref[...]) + b_ref[...]


def adaln(c, ada_w, ada_b, tn=1536):
    L, D, D6 = ada_w.shape
    R = c.shape[0]
    return pl.pallas_call(
        _adaln_kernel,
        grid=(L, D6 // tn),
        in_specs=[pl.BlockSpec((R, D), lambda l, j: (0, 0)),
                  pl.BlockSpec((None, D, tn), lambda l, j: (l, 0, j)),
                  pl.BlockSpec((None, 1, tn), lambda l, j: (l, 0, j))],
        out_specs=pl.BlockSpec((None, R, tn), lambda l, j: (l, 0, j)),
        out_shape=jax.ShapeDtypeStruct((L, R, D6), F32),
        compiler_params=_params("arbitrary", "arbitrary"),
        name="adaln",
    )(c, ada_w, ada_b.reshape(L, 1, D6))


def _segcumsum_kernel(x_ref, o_ref, *, seg):
    C, T = x_ref.shape
    lane = lax.broadcasted_iota(I32, (C, LANES), 1)
    w = min(seg, LANES)
    carry = None
    for blk in range(T // LANES):
        x = x_ref[:, blk * LANES:(blk + 1) * LANES]
        k = 1
        while k < w:
            x = x + jnp.where((lane & (w - 1)) >= k, pltpu.roll(x, k, 1), 0.0)
            k *= 2
        if seg > LANES:
            if blk % (seg // LANES) != 0:
                x = x + carry
            carry = x[:, LANES - 1:LANES]
        o_ref[:, blk * LANES:(blk + 1) * LANES] = x


def seg_cumsum(x, seg):
    R, C, T = x.shape
    assert T % LANES == 0 and (seg & (seg - 1)) == 0
    assert T % seg == 0 and (seg <= LANES or seg % LANES == 0)
    return pl.pallas_call(
        functools.partial(_segcumsum_kernel, seg=seg),
        grid=(R,),
        in_specs=[pl.BlockSpec((None, C, T), lambda r: (r, 0, 0))],
        out_specs=pl.BlockSpec((None, C, T), lambda r: (r, 0, 0)),
        out_shape=jax.ShapeDtypeStruct((R, C, T), F32),
        compiler_params=_params("arbitrary"),
        name="seg_cumsum",
    )(x)


def _pair_headnorm(blk, gain2, hd):
    lane = lax.broadcasted_iota(I32, blk.shape, 1)
    lo = lane < hd
    sq = blk * blk
    s_lo = jnp.sum(jnp.where(lo, sq, 0.0), axis=-1, keepdims=True)
    s_hi = jnp.sum(jnp.where(lo, 0.0, sq), axis=-1, keepdims=True)
    inv = lax.rsqrt(jnp.where(lo, s_lo, s_hi) * (1.0 / hd) + RMS_EPS)
    return blk * inv * gain2


def _fox_proj_kernel(x_ref, sc_ref, sh_ref, gain_ref, w_ref, wfh_ref, wfl_ref, bf_ref, qg_ref, kg_ref,
                     q_ref, k_ref, v_ref, kb_ref, vb_ref, sg_ref, lf_ref, *, n_heads):
    D = x_ref.shape[1]
    hd = D // n_heads
    h = _modnorm(x_ref[...], gain_ref[...], sc_ref[...], sh_ref[...])
    hh, hl = _split2(h)
    qk_scale = hd ** -0.5
    q = _dot(hh, w_ref[:, 0:D])
    for c in range(D // LANES):
        sl = slice(c * LANES, (c + 1) * LANES)
        q_ref[:, sl] = (_pair_headnorm(q[:, sl], qg_ref[...], hd) * qk_scale).astype(BF16)
    k = _dot(hh, w_ref[:, D:2 * D])
    for c in range(D // LANES):
        sl = slice(c * LANES, (c + 1) * LANES)
        kn = _pair_headnorm(k[:, sl], kg_ref[...], hd)
        k_ref[:, sl] = kn
        kb_ref[:, sl] = kn.astype(BF16)
    v = _dot(hh, w_ref[:, 2 * D:3 * D])
    v_ref[...] = v
    vb_ref[...] = v.astype(BF16)
    sg_ref[...] = jax.nn.sigmoid(_dot(hh, w_ref[:, 3 * D:4 * D]))
    f = _dot(hh, wfh_ref[...]) + _dot(hh, wfl_ref[...]) + _dot(hl, wfh_ref[...])
    lf_ref[...] = _log_sigmoid(f[:, 0:n_heads] + bf_ref[...])


def fox_project(x, scale, shift, gain, w_in, b_f, q_gain, k_gain, *, tm, tiles_per_group):
    N, D = x.shape
    H = b_f.shape[0]
    hd = D // H
    assert 2 * hd == LANES
    R = scale.shape[1]
    w_main = jnp.concatenate([w_in[:, :3 * D], w_in[:, 3 * D + H:]], axis=1).astype(BF16)
    w_f = jnp.pad(w_in[:, 3 * D:3 * D + H], ((0, 0), (0, LANES - H)))
    wfh = w_f.astype(BF16)
    wfl = (w_f - wfh.astype(F32)).astype(BF16)
    const = lambda i: (0, 0)
    row = lambda i: (i, 0)
    outs = pl.pallas_call(
        functools.partial(_fox_proj_kernel, n_heads=H),
        grid=(N // tm,),
        in_specs=[pl.BlockSpec((tm, D), row),
                  _mod_spec(R, D, tiles_per_group), _mod_spec(R, D, tiles_per_group),
                  pl.BlockSpec((1, D), const),
                  pl.BlockSpec((D, 4 * D), const),
                  pl.BlockSpec((D, LANES), const), pl.BlockSpec((D, LANES), const),
                  pl.BlockSpec((1, H), const),
                  pl.BlockSpec((1, LANES), const), pl.BlockSpec((1, LANES), const)],
        out_specs=[pl.BlockSpec((tm, D), row)] * 6 + [pl.BlockSpec((tm, H), row)],
        out_shape=[jax.ShapeDtypeStruct((N, D), BF16),
                   jax.ShapeDtypeStruct((N, D), F32), jax.ShapeDtypeStruct((N, D), F32),
                   jax.ShapeDtypeStruct((N, D), BF16), jax.ShapeDtypeStruct((N, D), BF16),
                   jax.ShapeDtypeStruct((N, D), F32),
                   jax.ShapeDtypeStruct((N, H), F32)],
        compiler_params=_params("arbitrary"),
        name="fox_project",
    )(x, scale, shift, gain.reshape(1, D), w_main, wfh, wfl, b_f.reshape(1, H),
      jnp.tile(q_gain, 2).reshape(1, LANES), jnp.tile(k_gain, 2).reshape(1, LANES))
    return outs


def _fox_attn_kernel(q_ref, k_ref, v_ref, fc_ref, fr_ref, o_ref, *, hd, n_sub, hps):
    TQ = q_ref.shape[0]
    SQ = TQ // n_sub
    qi = pl.program_id(2)
    lane = lax.broadcasted_iota(I32, (SQ, LANES), 1)
    row = lax.broadcasted_iota(I32, (SQ, TQ), 0)
    col = lax.broadcasted_iota(I32, (SQ, TQ), 1)
    chains = [(hh, sb) for hh in range(hps) for sb in range(n_sub)]
    nch = len(chains)
    qms, fqs = [], []
    for hh, sb in chains:
        pb, ph = hh // 2, hh % 2
        q = q_ref[sb * SQ:(sb + 1) * SQ, pb * LANES:(pb + 1) * LANES]
        qms.append(jnp.where((lane >= ph * hd) & (lane < (ph + 1) * hd), q, jnp.zeros_like(q)))
        fqs.append(fc_ref[sb * SQ:(sb + 1) * SQ, hh:hh + 1])

    def block_scores(j):
        start = pl.multiple_of(j * TQ, TQ)
        kjs = [k_ref[pl.ds(start, TQ), pb * LANES:(pb + 1) * LANES] for pb in range(hps // 2)]
        return tuple(_dot_nt(qms[c], kjs[hh // 2]) + (fqs[c] - fr_ref[hh:hh + 1, pl.ds(start, TQ)])
                     for c, (hh, sb) in enumerate(chains))

    def update(j, state, scores, masked):
        start = pl.multiple_of(j * TQ, TQ)
        vjs = [v_ref[pl.ds(start, TQ), pb * LANES:(pb + 1) * LANES] for pb in range(hps // 2)]
        new = []
        for c, (hh, sb) in enumerate(chains):
            m, l, acc = state[c]
            s = jnp.where(col <= row + sb * SQ, scores[c], NEG_INF) if masked else scores[c]
            m_new = jnp.maximum(m, jnp.max(s, axis=-1, keepdims=True))
            alpha = jnp.exp(m - m_new)
            p = jnp.exp(s - m_new)
            new.append((m_new, alpha * l + jnp.sum(p, axis=-1, keepdims=True),
                        alpha * acc + _dot(p.astype(BF16), vjs[hh // 2])))
        return tuple(new)

    def step(j, carry):
        state, scores = carry
        nxt = block_scores(j + 1)
        return update(j, state, scores, False), nxt

    init = tuple((jnp.full((SQ, 1), NEG_INF, F32), jnp.zeros((SQ, 1), F32), jnp.zeros((SQ, LANES), F32))
                 for _ in chains)
    state, scores = lax.fori_loop(0, qi, step, (init, block_scores(0)))
    final = update(qi, state, scores, True)
    for pb in range(hps // 2):
        for sb in range(n_sub):
            even = final[(2 * pb) * n_sub + sb]
            odd = final[(2 * pb + 1) * n_sub + sb]
            o_ref[sb * SQ:(sb + 1) * SQ, pb * LANES:(pb + 1) * LANES] = jnp.where(
                lane < hd, even[2] / even[1], odd[2] / odd[1])


def fox_attend_prompt(q, kb, vb, F, *, n_heads, tq=512, hps=2, n_sub=2):
    B, S, D = q.shape
    hd = D // n_heads
    assert 2 * hd == LANES and hps % 2 == 0 and n_heads % hps == 0
    HG = n_heads // hps
    W = hps * hd
    fcol = F.reshape(B, S, HG, hps).transpose(0, 2, 1, 3)
    frow = fcol.transpose(0, 1, 3, 2)
    return pl.pallas_call(
        functools.partial(_fox_attn_kernel, hd=hd, n_sub=n_sub, hps=hps),
        grid=(B, HG, S // tq),
        in_specs=[pl.BlockSpec((None, tq, W), lambda b, h, i: (b, i, h)),
                  pl.BlockSpec((None, S, W), lambda b, h, i: (b, 0, h)),
                  pl.BlockSpec((None, S, W), lambda b, h, i: (b, 0, h)),
                  pl.BlockSpec((None, None, tq, hps), lambda b, h, i: (b, h, i, 0)),
                  pl.BlockSpec((None, None, hps, S), lambda b, h, i: (b, h, 0, 0))],
        out_specs=pl.BlockSpec((None, tq, W), lambda b, h, i: (b, i, h)),
        out_shape=jax.ShapeDtypeStruct((B, S, D), F32),
        compiler_params=_params("arbitrary", "arbitrary", "arbitrary"),
        name="fox_attend_prompt",
    )(q, kb, vb, fcol, frow)


def _fox_decode_kernel(pt_ref, qbd_ref, *refs, n_heads, pp):
    del pt_ref
    kt_refs, vt_refs, cf_refs = refs[:pp], refs[pp:2 * pp], refs[2 * pp:3 * pp]
    kn_ref, vn_ref, cn_ref, o_ref, m_scr, l_scr, acc_scr, carry_scr = refs[3 * pp:]
    H = n_heads
    step = pl.program_id(1)
    n_steps = pl.num_programs(1)
    R, D = qbd_ref.shape
    T = R // H
    P = cf_refs[0].shape[1]

    @pl.when(step == 0)
    def _():
        m_scr[...] = jnp.full(m_scr.shape, NEG_INF, F32)
        l_scr[...] = jnp.zeros(l_scr.shape, F32)
        acc_scr[...] = jnp.zeros(acc_scr.shape, F32)
        carry_scr[...] = jnp.zeros(carry_scr.shape, F32)

    qbd = qbd_ref[...]

    def merge(m_i, l_i, acc_i):
        m = m_scr[...]
        m_new = jnp.maximum(m, m_i)
        a_old = jnp.exp(m - m_new)
        a_new = jnp.exp(m_i - m_new)
        l_scr[...] = a_old * l_scr[...] + a_new * l_i
        acc_scr[...] = a_old * acc_scr[...] + a_new * acc_i
        m_scr[...] = m_new

    def scores(s, fk):
        return s - jnp.concatenate([fk] * T, axis=0)

    carry = carry_scr[...]
    ss = []
    for i in range(pp):
        cf = cf_refs[i][...]
        ss.append(scores(_dot(qbd, kt_refs[i][...].astype(BF16)), cf + carry))
        carry = carry + cf[:, P - 1:P]
    carry_scr[...] = carry
    m_s = jnp.max(ss[0], axis=-1, keepdims=True)
    for s in ss[1:]:
        m_s = jnp.maximum(m_s, jnp.max(s, axis=-1, keepdims=True))
    l_s = jnp.zeros_like(m_s)
    acc_s = jnp.zeros(acc_scr.shape, F32)
    for i, s in enumerate(ss):
        pr = jnp.exp(s - m_s)
        l_s = l_s + jnp.sum(pr, axis=-1, keepdims=True)
        acc_s = acc_s + _dot_nt(pr.astype(BF16), vt_refs[i][...].astype(BF16))
    merge(m_s, l_s, acc_s)

    @pl.when(step == n_steps - 1)
    def _():
        s2 = scores(_dot_nt(qbd, kn_ref[...]), cn_ref[...] + carry_scr[...])
        key = lax.broadcasted_iota(I32, s2.shape, 1)
        t = lax.broadcasted_iota(I32, s2.shape, 0) // H
        s2 = jnp.where(key <= t, s2, NEG_INF)
        m2 = jnp.max(s2, axis=-1, keepdims=True)
        p2 = jnp.exp(s2 - m2)
        merge(m2, jnp.sum(p2, axis=-1, keepdims=True), _dot(p2.astype(BF16), vn_ref[...]))
        out = acc_scr[...] / l_scr[...]
        rh = lax.broadcasted_iota(I32, (R, D), 0) % H
        ch = lax.broadcasted_iota(I32, (R, D), 1) // (D // H)
        out = jnp.where(rh == ch, out, 0.0)
        o_ref[...] = jnp.sum(out.reshape(T, H, D), axis=1)


def fox_attend_sample(q, kb_new, vb_new, lf_new, cache_kt, cache_vt, cache_cf, page_table, j, *, n_heads):
    Bd, T, D = q.shape
    H = n_heads
    hd = D // H
    n_pages = page_table.shape[1]
    n_fox, n_pool, _, P = cache_cf.shape
    R = T * H
    pp = next(c for c in (16, 8, 4, 2, 1) if n_pages % c == 0)
    assert T <= P
    q4 = q.reshape(Bd, T, H, hd)
    eye = jnp.eye(H, dtype=q.dtype)
    qbd = (q4[:, :, :, None, :] * eye[None, None, :, :, None]).reshape(Bd, R, D)
    pad = ((0, 0), (0, P - T), (0, 0))
    kn = jnp.pad(kb_new, pad)
    vn = jnp.pad(vb_new, pad)
    cn = seg_cumsum(jnp.pad(lf_new.transpose(0, 2, 1), ((0, 0), (0, 0), (0, P - T))), P)
    pt = page_table.reshape(-1).astype(I32)

    def page_spec(shape, i):
        return pl.BlockSpec((None, None) + shape, lambda b, s, pt: (j, pt[b * n_pages + s * pp + i], 0, 0))

    per_seq = lambda shape: pl.BlockSpec((None,) + shape, lambda b, s, pt: (b, 0, 0))
    grid_spec = pltpu.PrefetchScalarGridSpec(
        num_scalar_prefetch=1,
        grid=(Bd, n_pages // pp),
        in_specs=([per_seq((R, D))]
                  + [page_spec((D, P), i) for i in range(pp)]
                  + [page_spec((D, P), i) for i in range(pp)]
                  + [page_spec((H, P), i) for i in range(pp)]
                  + [per_seq((P, D)), per_seq((P, D)), per_seq((H, P))]),
        out_specs=per_seq((T, D)),
        scratch_shapes=[pltpu.VMEM((R, 1), F32), pltpu.VMEM((R, 1), F32),
                        pltpu.VMEM((R, D), F32), pltpu.VMEM((H, 1), F32)],
    )
    return pl.pallas_call(
        functools.partial(_fox_decode_kernel, n_heads=H, pp=pp),
        grid_spec=grid_spec,
        out_shape=jax.ShapeDtypeStruct((Bd, T, D), F32),
        compiler_params=_params("arbitrary", "arbitrary"),
        name="fox_attend_sample",
    )(pt, qbd, *([cache_kt] * pp), *([cache_vt] * pp), *([cache_cf] * pp), kn, vn, cn)


def prepare_cache(cache_k, cache_v, cache_logf):
    n_fox, n_pool, P, H, hd = cache_k.shape
    kt = cache_k.transpose(0, 1, 3, 4, 2).reshape(n_fox, n_pool, H * hd, P)
    vt = cache_v.transpose(0, 1, 3, 4, 2).reshape(n_fox, n_pool, H * hd, P)
    rows = n_fox * n_pool * H
    group = 2048 if rows % 2048 == 0 else H
    cf = seg_cumsum(cache_logf.transpose(0, 1, 3, 2).reshape(rows // group, group, P), P)
    return kt, vt, cf.reshape(n_fox, n_pool, H, P)


def _outproj_kernel(a_ref, gate_ref, w_ref, x_ref, g_ref, hn_ref, o_ref, *, head_norm):
    a = a_ref[...]
    D = a.shape[1]
    if head_norm:
        blocks = []
        for c in range(D // LANES):
            sl = slice(c * LANES, (c + 1) * LANES)
            blk = a[:, sl]
            ms = jnp.mean(blk * blk, axis=-1, keepdims=True)
            blocks.append(blk * lax.rsqrt(ms + RMS_EPS) * hn_ref[:, sl])
        a = jnp.concatenate(blocks, axis=-1)
    y = (a * gate_ref[...]).astype(BF16)
    o_ref[...] = x_ref[...] + g_ref[...] * _dot(y, w_ref[...])


def out_project(a, gate, w_out, x, g, head_gain, *, tm, tiles_per_group, head_norm):
    N, D = x.shape
    R = g.shape[1]
    row = lambda i: (i, 0)
    const = lambda i: (0, 0)
    return pl.pallas_call(
        functools.partial(_outproj_kernel, head_norm=head_norm),
        grid=(N // tm,),
        in_specs=[pl.BlockSpec((tm, D), row), pl.BlockSpec((tm, D), row),
                  pl.BlockSpec((D, D), const), pl.BlockSpec((tm, D), row),
                  _mod_spec(R, D, tiles_per_group), pl.BlockSpec((1, D), const)],
        out_specs=pl.BlockSpec((tm, D), row),
        out_shape=jax.ShapeDtypeStruct((N, D), F32),
        compiler_params=_params("arbitrary"),
        name="out_project",
    )(a, gate, w_out.astype(BF16), x, g, head_gain.reshape(1, D))


def _ml_proj_kernel(x_ref, sc_ref, sh_ref, gain_ref, w_ref, wgh_ref, wgl_ref, bi_ref, bf_ref,
                    q_ref, k_ref, v_ref, so_ref, il_ref, fl_ref, *, n_heads, qk_width):
    D = x_ref.shape[1]
    QW = qk_width
    h = _modnorm(x_ref[...], gain_ref[...], sc_ref[...], sh_ref[...])
    hh, hl = _split2(h)
    k_scale = (QW // n_heads) ** -0.5
    q_ref[...] = _dot(hh, w_ref[:, 0:QW]).astype(BF16)
    k_ref[...] = (_dot(hh, w_ref[:, QW:2 * QW]) * k_scale).astype(BF16)
    v_ref[...] = _dot(hh, w_ref[:, 2 * QW:2 * QW + D])
    so_ref[...] = jax.nn.sigmoid(_dot(hh, w_ref[:, 2 * QW + D:2 * QW + 2 * D]))
    gts = _dot(hh, wgh_ref[...]) + _dot(hh, wgl_ref[...]) + _dot(hl, wgh_ref[...])
    il_ref[...] = gts[:, 0:n_heads] + bi_ref[...]
    fl_ref[...] = _log_sigmoid(gts[:, n_heads:2 * n_heads] + bf_ref[...])


def ml_project(x, scale, shift, gain, w_in, b_i, b_f, *, tm, tiles_per_group, v_width):
    N, D = x.shape
    H = b_i.shape[0]
    QW = (w_in.shape[1] - 2 * v_width - 2 * H) // 2
    R = scale.shape[1]
    g0 = 2 * QW + v_width
    w_main = jnp.concatenate([w_in[:, :g0], w_in[:, g0 + 2 * H:]], axis=1).astype(BF16)
    w_g = jnp.pad(w_in[:, g0:g0 + 2 * H], ((0, 0), (0, LANES - 2 * H)))
    wgh = w_g.astype(BF16)
    wgl = (w_g - wgh.astype(F32)).astype(BF16)
    const = lambda i: (0, 0)
    row = lambda i: (i, 0)
    WM = w_main.shape[1]
    return pl.pallas_call(
        functools.partial(_ml_proj_kernel, n_heads=H, qk_width=QW),
        grid=(N // tm,),
        in_specs=[pl.BlockSpec((tm, D), row),
                  _mod_spec(R, D, tiles_per_group), _mod_spec(R, D, tiles_per_group),
                  pl.BlockSpec((1, D), const),
                  pl.BlockSpec((D, WM), const),
                  pl.BlockSpec((D, LANES), const), pl.BlockSpec((D, LANES), const),
                  pl.BlockSpec((1, H), const), pl.BlockSpec((1, H), const)],
        out_specs=[pl.BlockSpec((tm, QW), row), pl.BlockSpec((tm, QW), row),
                   pl.BlockSpec((tm, v_width), row), pl.BlockSpec((tm, v_width), row),
                   pl.BlockSpec((tm, H), row), pl.BlockSpec((tm, H), row)],
        out_shape=[jax.ShapeDtypeStruct((N, QW), BF16), jax.ShapeDtypeStruct((N, QW), BF16),
                   jax.ShapeDtypeStruct((N, v_width), F32), jax.ShapeDtypeStruct((N, v_width), F32),
                   jax.ShapeDtypeStruct((N, H), F32), jax.ShapeDtypeStruct((N, H), F32)],
        compiler_params=_params("arbitrary"),
        name="ml_project",
    )(x, scale, shift, gain.reshape(1, D), w_main, wgh, wgl, b_i.reshape(1, H), b_f.reshape(1, H))


def _mlstm_kernel(q_ref, k_ref, v_ref, ic_ref, ir_ref, bc_ref, br_ref, c0_ref, n0_ref, m0_ref,
                  h_ref, c_ref, n_ref, m_ref, *, n_heads, dk):
    nbb, L = q_ref.shape[0], q_ref.shape[1]
    ci = pl.program_id(1)

    @pl.when(ci == 0)
    def _():
        c_ref[...] = c0_ref[...]
        n_ref[...] = n0_ref[...]
        m_ref[...] = m0_ref[...]

    tri = lax.broadcasted_iota(I32, (L, L), 1) <= lax.broadcasted_iota(I32, (L, L), 0)
    lane = lax.broadcasted_iota(I32, (L, LANES), 1)
    lane1 = lax.broadcasted_iota(I32, (1, LANES), 1)
    rsub = lax.broadcasted_iota(I32, (LANES, LANES), 0)
    dv = LANES
    m_in = [m_ref[bb] for bb in range(nbb)]
    units = [(bb, h) for bb in range(nbb) for h in range(n_heads)]
    pairs = [(bb, pr) for bb in range(nbb) for pr in range(n_heads // 2)]
    qp = {(bb, pr): q_ref[bb, :, pr * LANES:(pr + 1) * LANES] for bb, pr in pairs}
    kp = {(bb, pr): k_ref[bb, :, pr * LANES:(pr + 1) * LANES] for bb, pr in pairs}
    c_in = {u: c_ref[u[0], u[1]] for u in pairs}
    n_in = {u: n_ref[u[0], u[1]] for u in pairs}
    sel = [(lane >= hh * dk) & (lane < (hh + 1) * dk) for hh in range(2)]
    qm = {(bb, h): jnp.where(sel[h % 2], qp[(bb, h // 2)], jnp.zeros_like(qp[(bb, h // 2)])) for bb, h in units}
    qk = {(bb, h): _dot_nt(qm[(bb, h)], kp[(bb, h // 2)]) for bb, h in units}
    qc = {(bb, h): _dot(qm[(bb, h)], c_in[(bb, h // 2)].astype(BF16)) for bb, h in units}
    st = {}
    for bb, h in units:
        b_l = bc_ref[bb][:, h:h + 1]
        b_s = br_ref[bb][h:h + 1, :]
        ig_s = ir_ref[bb][h:h + 1, :]
        ig_l = ic_ref[bb][:, h:h + 1]
        m = m_in[bb][:, h:h + 1]
        dm = jnp.where(tri, b_l - b_s + ig_s, NEG_INF)
        inter = b_l + m
        m_t = jnp.maximum(inter, jnp.max(dm, axis=-1, keepdims=True))
        w_inter = jnp.exp(inter - m_t)
        a = jnp.exp(dm - m_t) * qk[(bb, h)]
        b_last = b_l[L - 1:L, :]
        m_new = jnp.maximum(b_last + m, jnp.max(b_last - b_s + ig_s, axis=-1, keepdims=True))
        wk = jnp.exp(b_last - b_l + ig_l - m_new)
        decay = jnp.exp(b_last + m - m_new)
        vh = v_ref[bb, :, h * dv:(h + 1) * dv]
        st[(bb, h)] = (a, m_t, w_inter, m_new, wk, decay, vh)
    av = {u: _dot(st[u][0].astype(BF16), st[u][6].astype(BF16)) for u in units}
    upd = {(bb, h): _dot_tn(kp[(bb, h // 2)], (st[(bb, h)][4] * st[(bb, h)][6]).astype(BF16)) for bb, h in units}
    for bb, pr in pairs:
        c_pair, n_pair = c_in[(bb, pr)], n_in[(bb, pr)]
        qpf = qp[(bb, pr)].astype(F32)
        kpf = kp[(bb, pr)].astype(F32)
        new_c, new_n = [], []
        for hh in range(2):
            h = 2 * pr + hh
            a, m_t, w_inter, m_new, wk, decay, vh = st[(bb, h)]
            num = w_inter * qc[(bb, h)] + av[(bb, h)]
            qn = jnp.sum(jnp.where(sel[hh], qpf, 0.0) * n_pair, axis=-1, keepdims=True)
            den = w_inter * qn + jnp.sum(a, axis=-1, keepdims=True)
            h_ref[bb, :, h * dv:(h + 1) * dv] = num / jnp.maximum(jnp.abs(den), jnp.exp(-m_t))
            new_c.append(decay * c_pair + upd[(bb, h)])
            new_n.append(decay * n_pair + jnp.sum(wk * kpf, axis=0, keepdims=True))
            m_ref[bb, :, h:h + 1] = m_new
        c_ref[bb, pr] = jnp.where(rsub < dk, new_c[0], new_c[1])
        n_ref[bb, pr] = jnp.where(lane1 < dk, new_n[0], new_n[1])


def mlstm_chunked(q, k, v, i_log, f_log, C0, n0, m0, chunk):
    B, T, _ = q.shape
    H = i_log.shape[-1]
    dk = C0.shape[2]
    dv = C0.shape[3]
    assert 2 * dk == LANES and dv == LANES
    nc = T // chunk
    L = chunk
    Tp = -(-T // LANES) * LANES
    fT = jnp.pad(f_log.transpose(0, 2, 1), ((0, 0), (0, 0), (0, Tp - T)))
    bT = seg_cumsum(fT, chunk)[:, :, :T]
    br = bT.reshape(B, H, nc, L).transpose(0, 2, 1, 3)
    bc = br.transpose(0, 1, 3, 2)
    ic = i_log.reshape(B, nc, L, H)
    ir = ic.transpose(0, 1, 3, 2)
    HP = H // 2
    c0 = C0.reshape(B, HP, 2 * dk, dv)
    n0p = n0.reshape(B, HP, 1, 2 * dk)
    m0p = m0.reshape(B, 1, H)
    QW = H * dk
    VW = H * dv
    nbb = next(c for c in (4, 2, 1) if B % c == 0)
    col_spec = pl.BlockSpec((nbb, None, L, H), lambda b, c: (b, c, 0, 0))
    row_spec = pl.BlockSpec((nbb, None, H, L), lambda b, c: (b, c, 0, 0))
    st = lambda b, c: (b, 0, 0, 0)
    h, Cn, nn, mn = pl.pallas_call(
        functools.partial(_mlstm_kernel, n_heads=H, dk=dk),
        grid=(B // nbb, nc),
        in_specs=[pl.BlockSpec((nbb, L, QW), lambda b, c: (b, c, 0)),
                  pl.BlockSpec((nbb, L, QW), lambda b, c: (b, c, 0)),
                  pl.BlockSpec((nbb, L, VW), lambda b, c: (b, c, 0)),
                  col_spec, row_spec, col_spec, row_spec,
                  pl.BlockSpec((nbb, HP, 2 * dk, dv), st),
                  pl.BlockSpec((nbb, HP, 1, 2 * dk), st),
                  pl.BlockSpec((nbb, 1, H), lambda b, c: (b, 0, 0))],
        out_specs=[pl.BlockSpec((nbb, L, VW), lambda b, c: (b, c, 0)),
                   pl.BlockSpec((nbb, HP, 2 * dk, dv), st),
                   pl.BlockSpec((nbb, HP, 1, 2 * dk), st),
                   pl.BlockSpec((nbb, 1, H), lambda b, c: (b, 0, 0))],
        out_shape=[jax.ShapeDtypeStruct((B, T, VW), F32),
                   jax.ShapeDtypeStruct((B, HP, 2 * dk, dv), F32),
                   jax.ShapeDtypeStruct((B, HP, 1, 2 * dk), F32),
                   jax.ShapeDtypeStruct((B, 1, H), F32)],
        compiler_params=_params("arbitrary", "arbitrary"),
        name="mlstm_chunked",
    )(q, k, v, ic, ir, bc, br, c0, n0p, m0p)
    return h, Cn.reshape(B, H, dk, dv), nn.reshape(B, H, dk), mn.reshape(B, H)


def _candidate_tiles(n):
    runs = []
    for a in range(n // 2):
        cnt = n // (a + 1)
        for b0 in range(0, cnt, 8):
            runs.append((a, b0, 0, min(8, cnt - b0)))
    runs.append((-1, 0, 0, n // 2))
    tiles, pos = {}, 0
    for a, b0, src, cnt in runs:
        while cnt:
            take = min(cnt, 8 - pos % 8)
            tiles.setdefault(pos // 8, []).append((a, b0, (pos - src) % 8, pos % 8, pos % 8 + take))
            pos, src, cnt = pos + take, src + take, cnt - take
    return [tiles[q] for q in sorted(tiles)]


def _extract_topk(ss, payloads, n):
    ss = list(ss)
    R = ss[0].shape[0]
    ridx = lax.broadcasted_iota(I32, ss[0].shape, 0).astype(F32)
    vals = [[] for _ in ss]
    pays = [[] for _ in ss]
    for it in range(n):
        for c in range(len(ss)):
            s = ss[c]
            m = jnp.max(s, axis=0, keepdims=True)
            am = jnp.min(jnp.where(s == m, ridx, float(R)), axis=0, keepdims=True)
            hit = ridx == am
            if payloads[c] is None:
                pays[c].append(am)
            else:
                pays[c].append(jnp.max(jnp.where(hit, payloads[c], -1.0), axis=0, keepdims=True))
            vals[c].append(m)
            if it + 1 < n:
                ss[c] = jnp.where(hit, NEG_INF, s)
    return vals, pays


def _peer_topk_kernel(x_ref, sc_ref, sh_ref, gain_ref, wq_ref, k1_ref, k2_ref,
                      hb_ref, e_ref, g_ref, q_scr, v1_scr, i1_scr, v2_scr, i2_scr, *, n_heads, n_keys):
    TM = x_ref.shape[0]
    K = PEER_TOPK
    half = k1_ref.shape[2]
    hb = _modnorm(x_ref[...], gain_ref[...], sc_ref[...], sh_ref[...]).astype(BF16)
    hb_ref[...] = hb
    q_scr[...] = _dot(hb, wq_ref[...]).astype(BF16)
    sub8 = lax.broadcasted_iota(I32, (8, LANES), 0)

    def head_body(hd, _):
        c0 = pl.multiple_of(hd * 2 * half, 2 * half)
        c1 = pl.multiple_of(hd * 2 * half + half, half)
        s1 = _dot_nt(k1_ref[hd], q_scr[:, pl.ds(c0, half)])
        s2 = _dot_nt(k2_ref[hd], q_scr[:, pl.ds(c1, half)])
        NB = TM // LANES
        blocks = [slice(blk * LANES, (blk + 1) * LANES) for blk in range(NB)]
        vals, idxs = _extract_topk([s[:, ls] for ls in blocks for s in (s1, s2)], [None] * (2 * NB), K)
        for blk in range(NB):
            for side, (v_scr, i_scr) in enumerate(((v1_scr, i1_scr), (v2_scr, i2_scr))):
                for r in range(K):
                    v_scr[blk, r:r + 1, :] = vals[2 * blk + side][r]
                    i_scr[blk, r:r + 1, :] = idxs[2 * blk + side][r]
        cands, pays = [], []
        for blk in range(NB):
            pieces = {}

            def piece(a, b0, shift, blk=blk, pieces=pieces):
                key = (a, b0, shift)
                if key not in pieces:
                    if a >= 0:
                        cv = v1_scr[blk, a:a + 1, :] + v2_scr[blk, b0:b0 + 8, :]
                        pv = i1_scr[blk, a:a + 1, :] * n_keys + i2_scr[blk, b0:b0 + 8, :]
                    else:
                        cv = v1_scr[blk, K // 2:K, :] + v2_scr[blk, 0:1, :]
                        pv = i1_scr[blk, K // 2:K, :] * n_keys + i2_scr[blk, 0:1, :]
                    if shift:
                        cv, pv = pltpu.roll(cv, shift, 0), pltpu.roll(pv, shift, 0)
                    pieces[key] = (cv, pv)
                return pieces[key]

            cand, pay = [], []
            for segs in _candidate_tiles(K):
                cv = jnp.full((8, LANES), NEG_INF, F32)
                pv = jnp.zeros((8, LANES), F32)
                for a, b0, shift, lo, hi in segs:
                    pc, pp_ = piece(a, b0, shift)
                    if (lo, hi) == (0, 8):
                        cv, pv = pc, pp_
                    else:
                        inside = (sub8 >= lo) & (sub8 < hi)
                        cv, pv = jnp.where(inside, pc, cv), jnp.where(inside, pp_, pv)
                cand.append(cv)
                pay.append(pv)
            cands.append(jnp.concatenate(cand, axis=0))
            pays.append(jnp.concatenate(pay, axis=0))
        tops, topes = _extract_topk(cands, pays, K)
        out_rows = pl.ds(pl.multiple_of(hd * K, K), K)
        for blk in range(NB):
            top_s, top_e = tops[blk], topes[blk]
            ex = [jnp.exp(t - top_s[0]) for t in top_s]
            den = ex[0]
            for t in ex[1:]:
                den = den + t
            inv = 1.0 / den
            for r in range(K):
                i1_scr[blk, r:r + 1, :] = top_e[r]
                v1_scr[blk, r:r + 1, :] = ex[r] * inv
            e_ref[out_rows, blocks[blk]] = i1_scr[blk].astype(I32)
            g_ref[out_rows, blocks[blk]] = v1_scr[blk]
        return 0

    lax.fori_loop(0, n_heads, head_body, 0)


def peer_retrieve(x, scale, shift, gain, w_q, sub_k1, sub_k2, *, tm, tiles_per_group):
    N, D = x.shape
    HP, NK, half = sub_k1.shape
    assert NK == LANES and half == LANES and PEER_TOPK % 8 == 0
    R = scale.shape[1]
    QW = w_q.shape[1]
    J = HP * PEER_TOPK
    const = lambda i: (0, 0)
    const3 = lambda i: (0, 0, 0)
    kspec = pl.BlockSpec((HP, NK, half), const3)
    return pl.pallas_call(
        functools.partial(_peer_topk_kernel, n_heads=HP, n_keys=NK),
        grid=(N // tm,),
        in_specs=[pl.BlockSpec((tm, D), lambda i: (i, 0)),
                  _mod_spec(R, D, tiles_per_group), _mod_spec(R, D, tiles_per_group),
                  pl.BlockSpec((1, D), const),
                  pl.BlockSpec((D, QW), const),
                  kspec, kspec],
        out_specs=[pl.BlockSpec((tm, D), lambda i: (i, 0)),
                   pl.BlockSpec((J, tm), lambda i: (0, i)),
                   pl.BlockSpec((J, tm), lambda i: (0, i))],
        out_shape=[jax.ShapeDtypeStruct((N, D), BF16),
                   jax.ShapeDtypeStruct((J, N), I32),
                   jax.ShapeDtypeStruct((J, N), F32)],
        scratch_shapes=[pltpu.VMEM((tm, QW), BF16),
                        ] + [pltpu.VMEM((tm // LANES, PEER_TOPK, LANES), F32)] * 4,
        compiler_params=_params("arbitrary"),
        name="peer_retrieve",
    )(x, scale, shift, gain.reshape(1, D), w_q.astype(BF16), sub_k1.astype(BF16), sub_k2.astype(BF16))


def _peer_mix_kernel(hb_ref, i1n_ref, i2n_ref, gn_ref, i1c_ref, i2c_ref, ut_ref, v_ref, x_ref, g2_ref, o_ref,
                     a_scr, w_scr, hbuf_scr, y_scr, *, n_keys, tg):
    TM = hb_ref.shape[0]
    TE = ut_ref.shape[1]
    J = i1n_ref.shape[1]
    NE = pl.num_programs(1) // 2
    NT = pl.num_programs(0) - 1
    t = pl.program_id(0)
    j = pl.program_id(1)
    NP2 = n_keys // 2
    W2 = 2 * n_keys
    per_step = TM // NE
    rows = lax.broadcasted_iota(I32, (NP2, J), 0)
    cols = lax.broadcasted_iota(I32, (W2, J), 0)

    n_groups = per_step // tg

    def group_grids(g):
        t0 = pl.multiple_of(j * per_step + g * tg, tg)
        grids = []
        for u in range(tg):
            i1 = i1c_ref[pl.ds(t0 + u, 1), :]
            i2 = i2c_ref[pl.ds(t0 + u, 1), :]
            w = w_scr[pl.ds(t0 + u, 1), :]
            p1 = jnp.where(rows == (i1 >> 1), w, 0.0).astype(BF16)
            p2 = jnp.where(cols == ((i1 & 1) * n_keys + i2), 1.0, 0.0).astype(BF16)
            grids.append(_dot_nt(p1, p2).astype(BF16))
        return jnp.stack(grids, axis=0)

    def group_store(g, stacked):
        t0 = pl.multiple_of(j * per_step + g * tg, tg)
        hbuf_scr[:, pl.ds(t0, tg), :] = pltpu.einshape("tpc->ptc", stacked)

    def build_slice():
        for g in range(n_groups):
            group_store(g, group_grids(g))

    def activations(with_build=False):
        CW = MXU_DIM
        n_chunks = TE // CW
        hb = hb_ref[...]
        i1 = i1n_ref[...]
        i2 = i2n_ref[...]
        acc = a_scr[...]
        events = {}
        if with_build:
            for e in range(2 * n_groups):
                events.setdefault(min(e * n_chunks // (2 * n_groups), n_chunks - 1), []).append(e)
        pending = {}

        def pick(acc, k, act):
            for c in range(CW // n_keys):
                r = j * (TE // n_keys) + k * (CW // n_keys) + c
                acc = jnp.where(i1 == r, jnp.take_along_axis(act[:, c * n_keys:(c + 1) * n_keys], i2, axis=1), acc)
            return acc

        prev = None
        for k in range(n_chunks):
            act = _dot(hb, ut_ref[:, k * CW:(k + 1) * CW])
            for e in events.get(k, []):
                if e % 2 == 0:
                    pending[e // 2] = group_grids(e // 2)
                else:
                    group_store(e // 2, pending.pop(e // 2))
            if prev is not None:
                acc = pick(acc, k - 1, prev)
            prev = act
        a_scr[...] = pick(acc, n_chunks - 1, prev)

    @pl.when((j == 0) & (t < NT))
    def _():
        a_scr[...] = jnp.zeros(a_scr.shape, F32)

    @pl.when((j < NE) & (t > 0) & (t < NT))
    def _():
        activations(with_build=True)

    @pl.when((j < NE) & (t == 0))
    def _():
        activations()

    @pl.when((j < NE) & (t == NT))
    def _():
        build_slice()

    @pl.when((j == NE - 1) & (t < NT))
    def _():
        a = a_scr[...]
        w_scr[...] = gn_ref[...] * (0.5 * a * (1.0 + lax.erf(a * math.sqrt(0.5))))

    @pl.when((j == NE) & (t > 0))
    def _():
        y_scr[...] = jnp.zeros(y_scr.shape, F32)

    @pl.when((j >= NE) & (t > 0))
    def _():
        jj = j - NE
        y = y_scr[...]
        for c in range(TE // W2):
            y = y + _dot(hbuf_scr[jj * (TE // W2) + c], v_ref[c * W2:(c + 1) * W2, :])
        y_scr[...] = y

    @pl.when((j == 2 * NE - 1) & (t > 0))
    def _():
        o_ref[...] = x_ref[...] + g2_ref[...] * y_scr[...]


def peer_mix(hb, i1, i2, g, ut_tab, v_tab, x, g2, *, tm, te, tiles_per_group, n_keys):
    N, D = x.shape
    E = v_tab.shape[0]
    J = i1.shape[1]
    NE = E // te
    NT = N // tm
    R = g2.shape[1]
    tg = 16
    assert te % (2 * n_keys) == 0 and n_keys == LANES and tm % (NE * tg) == 0
    nxt = lambda t, j: (jnp.minimum(t, NT - 1), 0)
    cur = lambda t, j: (jnp.maximum(t - 1, 0), 0)
    return pl.pallas_call(
        functools.partial(_peer_mix_kernel, n_keys=n_keys, tg=tg),
        grid=(NT + 1, 2 * NE),
        in_specs=[pl.BlockSpec((tm, D), nxt),
                  pl.BlockSpec((tm, J), nxt), pl.BlockSpec((tm, J), nxt), pl.BlockSpec((tm, J), nxt),
                  pl.BlockSpec((tm, J), cur), pl.BlockSpec((tm, J), cur),
                  pl.BlockSpec((D, te), lambda t, j: (0, jnp.where(t < NT, jnp.minimum(j, NE - 1), NE - 1))),
                  pl.BlockSpec((te, D), lambda t, j: (jnp.where(t > 0, jnp.maximum(j - NE, 0), 0), 0)),
                  pl.BlockSpec((tm, D), cur),
                  pl.BlockSpec((None, R, D), lambda t, j: (jnp.maximum(t - 1, 0) // tiles_per_group, 0, 0))],
        out_specs=pl.BlockSpec((tm, D), cur),
        out_shape=jax.ShapeDtypeStruct((N, D), F32),
        scratch_shapes=[pltpu.VMEM((tm, J), F32), pltpu.VMEM((tm, J), F32),
                        pltpu.VMEM((n_keys // 2, tm, 2 * n_keys), BF16),
                        pltpu.VMEM((tm, D), F32)],
        compiler_params=_params("arbitrary", "arbitrary"),
        name="peer_mix",
    )(hb, i1, i2, g, i1, i2, ut_tab, v_tab, x, g2)


def _trunk(x3, mods, past, params, tables, *, tm):
    (norm_mix, norm_ffn, fox_w_in, fox_b_f, fox_q_norm, fox_k_norm, fox_w_out,
     ml_w_in, ml_b_i, ml_b_f, ml_h_norm, ml_w_out, peer_w_q, peer_sub_k1, peer_sub_k2) = params
    ut_bf, v_bf = tables
    B, T, D = x3.shape
    N = B * T
    depth = norm_mix.shape[0]
    n_fox_heads = fox_b_f.shape[1]
    n_keys = peer_sub_k1.shape[2]
    x = x3.reshape(N, D)
    if T % tm == 0:
        tpg, rows = T // tm, 1
        expand = lambda a: a.reshape(B, 1, D)
    else:
        assert N == tm
        tpg, rows = 1, N
        expand = lambda a: jnp.repeat(a, T, axis=0).reshape(1, N, D)
    big_tm = 2 * tm if tpg % 2 == 0 else tm
    big_tpg = tpg * tm // big_tm
    new_k, new_v, new_lf, new_C, new_n, new_m = [], [], [], [], [], []
    for layer in range(depth):
        sh1, sc1, g1, sh2, sc2, g2 = [expand(a) for a in jnp.split(mods[layer], 6, axis=-1)]
        j = layer // 2
        if layer % 2 == 0:
            q, k, v, kb, vb, sg, lf = fox_project(
                x, sc1, sh1, norm_mix[layer], fox_w_in[j], fox_b_f[j], fox_q_norm[j], fox_k_norm[j],
                tm=big_tm, tiles_per_group=big_tpg)
            if past is None:
                F = seg_cumsum(lf.reshape(B, T, n_fox_heads).transpose(0, 2, 1), T).transpose(0, 2, 1)
                o = fox_attend_prompt(q.reshape(B, T, D), kb.reshape(B, T, D), vb.reshape(B, T, D), F,
                                      n_heads=n_fox_heads)
            else:
                cache_kt, cache_vt, cache_cf, _, _, _, page_table = past
                o = fox_attend_sample(q.reshape(B, T, D), kb.reshape(B, T, D), vb.reshape(B, T, D),
                                      lf.reshape(B, T, n_fox_heads), cache_kt, cache_vt, cache_cf,
                                      page_table, j, n_heads=n_fox_heads)
            x = out_project(o.reshape(N, D), sg, fox_w_out[j], x, g1, jnp.ones((D,), F32),
                            tm=big_tm, tiles_per_group=big_tpg, head_norm=False)
            new_k.append(k.reshape(B, T, n_fox_heads, -1))
            new_v.append(v.reshape(B, T, n_fox_heads, -1))
            new_lf.append(lf.reshape(B, T, n_fox_heads))
        else:
            n_ml_heads = ml_b_i.shape[1]
            v_width = ml_w_out.shape[1]
            q, k, v, so, il, fl = ml_project(x, sc1, sh1, norm_mix[layer], ml_w_in[j], ml_b_i[j], ml_b_f[j],
                                             tm=big_tm, tiles_per_group=big_tpg, v_width=v_width)
            QW = q.shape[1]
            dk = QW // n_ml_heads
            dv = v_width // n_ml_heads
            q3, k3, v3 = q.reshape(B, T, QW), k.reshape(B, T, QW), v.reshape(B, T, v_width)
            il3, fl3 = il.reshape(B, T, n_ml_heads), fl.reshape(B, T, n_ml_heads)
            if past is None:
                C0 = jnp.zeros((B, n_ml_heads, dk, dv), F32)
                n0 = jnp.zeros((B, n_ml_heads, dk), F32)
                m0 = jnp.zeros((B, n_ml_heads), F32)
                hh, C, n_, m_ = mlstm_chunked(q3, k3, v3, il3, fl3, C0, n0, m0, 64)
            else:
                _, _, _, state_C, state_n, state_m, _ = past
                Tp = 16
                pad3 = ((0, 0), (0, Tp - T), (0, 0))
                hh, C, n_, m_ = mlstm_chunked(
                    jnp.pad(q3, pad3), jnp.pad(k3, pad3), jnp.pad(v3, pad3),
                    jnp.pad(il3, pad3, constant_values=-1e30), jnp.pad(fl3, pad3),
                    state_C[j], state_n[j], state_m[j], Tp)
                hh = hh[:, :T]
            x = out_project(hh.reshape(N, v_width), so, ml_w_out[j], x, g1, ml_h_norm[j].reshape(-1),
                            tm=big_tm, tiles_per_group=big_tpg, head_norm=True)
            new_C.append(C)
            new_n.append(n_)
            new_m.append(m_)
        hb, e, g = peer_retrieve(x, sc2, sh2, norm_ffn[layer], peer_w_q[layer], peer_sub_k1[layer],
                                 peer_sub_k2[layer], tm=big_tm, tiles_per_group=big_tpg)
        eT = e.T
        x = peer_mix(hb, eT // n_keys, eT % n_keys, g.T, ut_bf[layer], v_bf[layer], x, g2,
                     tm=big_tm, te=2048, tiles_per_group=big_tpg, n_keys=n_keys)
    return (x.reshape(B, T, D), jnp.stack(new_k), jnp.stack(new_v), jnp.stack(new_lf),
            jnp.stack(new_C), jnp.stack(new_n), jnp.stack(new_m))


def kernel(x_prompt, x_sample, cache_k, cache_v, cache_logf, state_C, state_n, state_m, page_table,
           c_prompt, c_sample, ada_w, ada_b, norm_mix, norm_ffn, fox_w_in, fox_b_f, fox_q_norm, fox_k_norm,
           fox_w_out, ml_w_in, ml_b_i, ml_b_f, ml_h_norm, ml_w_out, peer_w_q, peer_sub_k1, peer_sub_k2,
           peer_u, peer_v):
    params = (norm_mix, norm_ffn, fox_w_in, fox_b_f, fox_q_norm, fox_k_norm, fox_w_out,
              ml_w_in, ml_b_i, ml_b_f, ml_h_norm, ml_w_out, peer_w_q, peer_sub_k1, peer_sub_k2)
    tables = (peer_u.astype(BF16).transpose(0, 2, 1), peer_v.astype(BF16))
    Bp = c_prompt.shape[0]
    mods = adaln(jnp.concatenate([c_prompt, c_sample], axis=0), ada_w, ada_b)
    y_p, k_p, v_p, lf_p, C_p, n_p, m_p = _trunk(x_prompt, mods[:, :Bp], None, params, tables, tm=256)
    past = prepare_cache(cache_k, cache_v, cache_logf) + (state_C, state_n, state_m, page_table)
    y_s, k_s, v_s, lf_s, C_s, n_s, m_s = _trunk(x_sample, mods[:, Bp:], past, params, tables,
                                               tm=x_sample.shape[0] * x_sample.shape[1])
    return (y_p, y_s, k_p, v_p, lf_p, C_p, n_p, m_p, k_s, v_s, lf_s, C_s, n_s, m_s)
```
